```python
import math
import jax, jax.numpy as jnp
from jax import lax
import numpy as np

D_MODEL = 1024
BATCH = 2
SEQ = 16384
DEPTH = 4

GRID_W = 64
CTX_LEN = 256

F_GROUPS = 4
F_GROUP_DIM = D_MODEL // 8
F_WIDTH = F_GROUPS * F_GROUP_DIM
HEAD_DIM = 64
N_HEADS = (D_MODEL // 2) // HEAD_DIM
N_KV_HEADS = N_HEADS // 4
GQ = N_HEADS // N_KV_HEADS
Q_WIDTH = N_HEADS * HEAD_DIM
KV_WIDTH = N_KV_HEADS * HEAD_DIM
KV_START = F_WIDTH + Q_WIDTH
MIX_IN = F_WIDTH + Q_WIDTH + 2 * KV_WIDTH
MIX_OUT = F_WIDTH + Q_WIDTH
Q_BLOCK = 128
ROPE_THETA = 10000.0
AXIS_ROT_DIM = HEAD_DIM // 2
ATTN_SCALE = HEAD_DIM ** -0.5

D_INNER = 2 * D_MODEL
SSM_HEAD_DIM = 64
N_SSM_HEADS = D_INNER // SSM_HEAD_DIM
N_GROUPS = 4
HEADS_PER_GROUP = N_SSM_HEADS // N_GROUPS
D_STATE = 128
D_CONV = 5
CONV_PAD = D_CONV // 2
CONV_DIM = D_INNER + 2 * N_GROUPS * D_STATE
SSM_IN = D_INNER + CONV_DIM + 2 * N_SSM_HEADS
CHUNK = 128
DT_MIN = 0.001
DT_MAX = 0.1
A_MIN = 1.0
A_MAX = 16.0

D_FF = 2816
N_EXPERTS = 8
TOP_K = 2
D_FF_EXPERT = 3584

N_EVEN = (DEPTH + 1) // 2
N_ODD = DEPTH // 2
EPS = 1e-6

kernel_name = "hybrid_fourier_gqa_ssd_moe_dit"


def _rms_norm(x, w):
    xf = x.astype(jnp.float32)
    y = xf * lax.rsqrt(jnp.mean(xf * xf, axis=-1, keepdims=True) + EPS)
    return (y * w.astype(jnp.float32)).astype(x.dtype)


def _modulate(h, shift, scale):
    return h * (1.0 + scale) + shift


def _axial_rope_tables(rows):
    row = jnp.repeat(jnp.arange(rows, dtype=jnp.float32), GRID_W)
    col = jnp.tile(jnp.arange(GRID_W, dtype=jnp.float32), rows)
    freqs = ROPE_THETA ** (-jnp.arange(0, AXIS_ROT_DIM, 2, dtype=jnp.float32) / AXIS_ROT_DIM)
    ang = jnp.concatenate([row[:, None] * freqs, col[:, None] * freqs], axis=-1)
    return jnp.cos(ang), jnp.sin(ang)


def _rope(x, cos, sin):
    xf = x.astype(jnp.float32).reshape(*x.shape[:-1], HEAD_DIM // 2, 2)
    x1, x2 = xf[..., 0], xf[..., 1]
    c = cos[None, :, None, :]
    s = sin[None, :, None, :]
    out = jnp.stack([x1 * c - x2 * s, x1 * s + x2 * c], axis=-1).reshape(x.shape)
    return out.astype(x.dtype)


def _attend(q, k, v):
    s = jnp.einsum('bqkgd,bskd->bkgqs', q, k).astype(jnp.float32) * ATTN_SCALE
    p = jax.nn.softmax(s, axis=-1).astype(v.dtype)
    return jnp.einsum('bkgqs,bskd->bqkgd', p, v)


def _blocked_attention(q, k, v):
    b, s = q.shape[:2]
    qb = q.reshape(b, s // Q_BLOCK, Q_BLOCK, N_KV_HEADS, GQ, HEAD_DIM).swapaxes(0, 1)
    o = lax.map(lambda blk: _attend(blk, k, v), qb)
    return o.swapaxes(0, 1).reshape(b, s, Q_WIDTH)


def _fourier_mix(u):
    b, l, _ = u.shape
    ug = u.astype(jnp.float32).reshape(b, l, F_GROUPS, F_GROUP_DIM)
    f = jnp.fft.fft2(ug, axes=(1, 3), norm='ortho').real
    return f.reshape(b, l, F_WIDTH).astype(u.dtype)


def _fourier_attention_mixer(hx, hc, w_in, q_norm_w, k_norm_w, w_out, cos, sin, need_ctx):
    b, s, _ = hx.shape
    lc = hc.shape[1]
    px = hx @ w_in
    u_x = px[..., :F_WIDTH]
    q_x = px[..., F_WIDTH:KV_START].reshape(b, s, N_HEADS, HEAD_DIM)
    k_x = px[..., KV_START:KV_START + KV_WIDTH].reshape(b, s, N_KV_HEADS, HEAD_DIM)
    v_x = px[..., KV_START + KV_WIDTH:].reshape(b, s, N_KV_HEADS, HEAD_DIM)
    q_x = _rope(_rms_norm(q_x, q_norm_w), cos, sin)
    k_x = _rope(_rms_norm(k_x, k_norm_w), cos, sin)
    pc = hc @ w_in[:, KV_START:]
    k_c = _rms_norm(pc[..., :KV_WIDTH].reshape(b, lc, N_KV_HEADS, HEAD_DIM), k_norm_w)
    v_c = pc[..., KV_WIDTH:].reshape(b, lc, N_KV_HEADS, HEAD_DIM)
    k_all = jnp.concatenate([k_c, k_x], axis=1)
    v_all = jnp.concatenate([v_c, v_x], axis=1)
    a_x = _blocked_attention(q_x.reshape(b, s, N_KV_HEADS, GQ, HEAD_DIM), k_all, v_all)
    y_x = jnp.concatenate([_fourier_mix(u_x), a_x], axis=-1) @ w_out
    if not need_ctx:
        return y_x, None
    pcq = hc @ w_in[:, :KV_START]
    u_c = pcq[..., :F_WIDTH]
    q_c = _rms_norm(pcq[..., F_WIDTH:].reshape(b, lc, N_HEADS, HEAD_DIM), q_norm_w)
    a_c = _attend(q_c.reshape(b, lc, N_KV_HEADS, GQ, HEAD_DIM), k_c, v_c).reshape(b, lc, Q_WIDTH)
    y_c = jnp.concatenate([_fourier_mix(u_c), a_c], axis=-1) @ w_out
    return y_x, y_c


def _dwconv(u, w, bias):
    ch = u.shape[-1]
    out = lax.conv_general_dilated(u, w[:, None, :], window_strides=(1,),
                                   padding=[(CONV_PAD, CONV_PAD)],
                                   dimension_numbers=('NWC', 'WIO', 'NWC'),
                                   feature_group_count=ch)
    return out + bias


def _ssd_scan(x, dt, A, B, C, h0, return_y):
    b, l, g, hg, p = x.shape
    n = B.shape[-1]
    nc = l // CHUNK
    f32 = jnp.float32
    x = x.astype(f32).reshape(b, nc, CHUNK, g, hg, p)
    dt = dt.reshape(b, nc, CHUNK, g, hg)
    B = B.astype(f32).reshape(b, nc, CHUNK, g, n)
    C = C.astype(f32).reshape(b, nc, CHUNK, g, n)
    a_cum = jnp.cumsum(dt * A, axis=2)
    xdt = x * dt[..., None]
    decay_end = jnp.exp(a_cum[:, :, -1:] - a_cum)
    states = jnp.einsum('bcqgn,bcqgh,bcqghp->bcghpn', B, decay_end, xdt)
    chunk_decay = jnp.exp(a_cum[:, :, -1])

    def step(h, inp):
        dec, st = inp
        return dec[..., None, None] * h + st, h

    final, prev = lax.scan(step, h0, (jnp.moveaxis(chunk_decay, 1, 0), jnp.moveaxis(states, 1, 0)))
    if not return_y:
        return None, final
    prev = jnp.moveaxis(prev, 0, 1)
    idx = jnp.arange(CHUNK)
    mask = (idx[:, None] >= idx[None, :])[:, :, None, None]
    seg = a_cum[:, :, :, None] - a_cum[:, :, None]
    decay = jnp.exp(jnp.where(mask, seg, -jnp.inf))
    cb = jnp.einsum('bcign,bcjgn->bcijg', C, B)
    y_diag = jnp.einsum('bcijg,bcijgh,bcjghp->bcighp', cb, decay, xdt)
    y_off = jnp.einsum('bcign,bcghpn,bcigh->bcighp', C, prev, jnp.exp(a_cum))
    return (y_diag + y_off).reshape(b, l, g, hg, p), final


def _ssd_branch(h, w_in, conv_w, conv_b, dt_bias, A, d_skip, norm_w, w_out, h0_f, h0_b, with_out):
    b, l, _ = h.shape
    if with_out:
        proj = h @ w_in
        z = proj[..., :D_INNER]
        rest = proj[..., D_INNER:]
    else:
        rest = h @ w_in[:, D_INNER:]
    xbc = jax.nn.silu(_dwconv(rest[..., :CONV_DIM], conv_w, conv_b))
    xs = xbc[..., :D_INNER].reshape(b, l, N_GROUPS, HEADS_PER_GROUP, SSM_HEAD_DIM)
    bm = xbc[..., D_INNER:D_INNER + N_GROUPS * D_STATE].reshape(b, l, N_GROUPS, D_STATE)
    cm = xbc[..., D_INNER + N_GROUPS * D_STATE:].reshape(b, l, N_GROUPS, D_STATE)
    dt = jax.nn.softplus(rest[..., CONV_DIM:].astype(jnp.float32).reshape(b, l, 2, N_GROUPS, HEADS_PER_GROUP)
                         + dt_bias.astype(jnp.float32).reshape(2, N_GROUPS, HEADS_PER_GROUP))
    flip = lambda a: jnp.flip(a, axis=1)
    y_f, s_f = _ssd_scan(xs, dt[:, :, 0], A[0], bm, cm, h0_f, with_out)
    y_b, s_b = _ssd_scan(flip(xs), flip(dt[:, :, 1]), A[1], flip(bm), flip(cm), h0_b, with_out)
    if not with_out:
        return None, s_f, s_b
    y = y_f + flip(y_b) + d_skip.astype(jnp.float32).reshape(N_GROUPS, HEADS_PER_GROUP, 1) * xs.astype(jnp.float32)
    gy = (y.reshape(b, l, D_INNER) * jax.nn.silu(z.astype(jnp.float32))).reshape(b, l, N_GROUPS, D_INNER // N_GROUPS)
    gy = _rms_norm(gy, norm_w.reshape(N_GROUPS, D_INNER // N_GROUPS)).reshape(b, l, D_INNER).astype(h.dtype)
    return gy @ w_out, s_f, s_b


def _ssd_mixer(hx, hc, w_in, conv_w, conv_b, dt_bias, a_log, d_skip, norm_w, w_out, need_ctx):
    A = -jnp.exp(a_log.astype(jnp.float32)).reshape(2, N_GROUPS, HEADS_PER_GROUP)
    zeros = jnp.zeros((hc.shape[0], N_GROUPS, HEADS_PER_GROUP, SSM_HEAD_DIM, D_STATE), jnp.float32)
    y_c, s_f, s_b = _ssd_branch(hc, w_in, conv_w, conv_b, dt_bias, A, d_skip, norm_w, w_out,
                                zeros, zeros, need_ctx)
    y_x, _, _ = _ssd_branch(hx, w_in, conv_w, conv_b, dt_bias, A, d_skip, norm_w, w_out,
                            s_f, s_b, True)
    return y_x, y_c


def _swiglu(h, w_gu, w_down):
    gu = h @ w_gu
    f = w_gu.shape[-1] // 2
    return (jax.nn.silu(gu[..., :f]) * gu[..., f:]) @ w_down


def _moe(h, router_w, w_gu, w_down):
    shape = h.shape
    t = h.reshape(-1, shape[-1])
    logits = (t @ router_w).astype(jnp.float32)
    top_v, top_i = lax.top_k(logits, TOP_K)
    gates = jax.nn.softmax(top_v, axis=-1)
    combine = jnp.einsum('nk,nke->ne', gates,
                         jax.nn.one_hot(top_i, N_EXPERTS, dtype=jnp.float32)).astype(t.dtype)
    out = jnp.zeros_like(t)
    for e in range(N_EXPERTS):
        out = out + combine[:, e:e + 1] * _swiglu(t, w_gu[e], w_down[e])
    return out.reshape(shape)


def setup_inputs(seed: int = 0) -> dict:
    key = jax.random.key(seed)
    ks = jax.random.split(key, 24)
    f32 = jnp.float32

    def nrm(k, shape, scale):
        return jax.random.normal(k, shape, f32) * scale

    def gain(k, shape):
        return 1.0 + 0.05 * jax.random.normal(k, shape, f32)

    dt0 = jnp.exp(jax.random.uniform(ks[16], (N_ODD, 2, N_SSM_HEADS), f32, math.log(DT_MIN), math.log(DT_MAX)))
    return {
        "x": nrm(ks[0], (BATCH, SEQ, D_MODEL), 1.0),
        "c": nrm(ks[1], (BATCH, D_MODEL), 1.0),
        "ctx": nrm(ks[2], (BATCH, CTX_LEN, D_MODEL), 1.0),
        "c_ctx": nrm(ks[3], (D_MODEL,), 1.0),
        "ada_w": nrm(ks[4], (DEPTH, D_MODEL, 6 * D_MODEL), 0.5 * D_MODEL ** -0.5),
        "ada_b": nrm(ks[5], (DEPTH, 6 * D_MODEL), 0.01),
        "norm_w": gain(ks[6], (DEPTH, 4, D_MODEL)),
        "mix_in_w": nrm(ks[7], (N_EVEN, D_MODEL, MIX_IN), D_MODEL ** -0.5),
        "q_norm_w": gain(ks[8], (N_EVEN, HEAD_DIM)),
        "k_norm_w": gain(ks[9], (N_EVEN, HEAD_DIM)),
        "mix_out_w": nrm(ks[10], (N_EVEN, MIX_OUT, D_MODEL), MIX_OUT ** -0.5),
        "ffn_gu_w": nrm(ks[11], (N_EVEN, D_MODEL, 2 * D_FF), D_MODEL ** -0.5),
        "ffn_down_w": nrm(ks[12], (N_EVEN, D_FF, D_MODEL), D_FF ** -0.5),
        "ssm_in_w": nrm(ks[13], (N_ODD, D_MODEL, SSM_IN), D_MODEL ** -0.5),
        "ssm_conv_w": nrm(ks[14], (N_ODD, D_CONV, CONV_DIM), D_CONV ** -0.5),
        "ssm_conv_b": nrm(ks[15], (N_ODD, CONV_DIM), 0.01),
        "ssm_dt_bias": dt0 + jnp.log(-jnp.expm1(-dt0)),
        "ssm_a_log": jnp.log(jax.random.uniform(ks[17], (N_ODD, 2, N_SSM_HEADS), f32, A_MIN, A_MAX)),
        "ssm_d": gain(ks[18], (N_ODD, N_SSM_HEADS)),
        "ssm_norm_w": gain(ks[19], (N_ODD, D_INNER)),
        "ssm_out_w": nrm(ks[20], (N_ODD, D_INNER, D_MODEL), D_INNER ** -0.5),
        "router_w": nrm(ks[21], (N_ODD, D_MODEL, N_EXPERTS), D_MODEL ** -0.5),
        "moe_gu_w": nrm(ks[22], (N_ODD, N_EXPERTS, D_MODEL, 2 * D_FF_EXPERT), D_MODEL ** -0.5),
        "moe_down_w": nrm(ks[23], (N_ODD, N_EXPERTS, D_FF_EXPERT, D_MODEL), D_FF_EXPERT ** -0.5),
    }


def reference(x, c, ctx, c_ctx, ada_w, ada_b, norm_w, mix_in_w, q_norm_w, k_norm_w, mix_out_w,
              ffn_gu_w, ffn_down_w, ssm_in_w, ssm_conv_w, ssm_conv_b, ssm_dt_bias, ssm_a_log,
              ssm_d, ssm_norm_w, ssm_out_w, router_w, moe_gu_w, moe_down_w):
    rows = x.shape[1] // GRID_W
    cos, sin = _axial_rope_tables(rows)
    silu_c = jax.nn.silu(c)
    silu_cc = jax.nn.silu(c_ctx)
    for i in range(DEPTH):
        last = i == DEPTH - 1
        j = i // 2
        mx = jnp.split((silu_c @ ada_w[i] + ada_b[i])[:, None, :], 6, axis=-1)
        mc = jnp.split((silu_cc @ ada_w[i] + ada_b[i])[None, None, :], 6, axis=-1)
        hx = _modulate(_rms_norm(x, norm_w[i, 0]), mx[0], mx[1])
        hc = _modulate(_rms_norm(ctx, norm_w[i, 0]), mc[0], mc[1])
        if i % 2 == 0:
            yx, yc = _fourier_attention_mixer(hx, hc, mix_in_w[j], q_norm_w[j], k_norm_w[j],
                                              mix_out_w[j], cos, sin, not last)
        else:
            yx, yc = _ssd_mixer(hx, hc, ssm_in_w[j], ssm_conv_w[j], ssm_conv_b[j], ssm_dt_bias[j],
                                ssm_a_log[j], ssm_d[j], ssm_norm_w[j], ssm_out_w[j], not last)
        x = x + mx[2] * _rms_norm(yx, norm_w[i, 1])
        hx = _modulate(_rms_norm(x, norm_w[i, 2]), mx[3], mx[4])
        if i % 2 == 0:
            fx = _swiglu(hx, ffn_gu_w[j], ffn_down_w[j])
        else:
            fx = _moe(hx, router_w[j], moe_gu_w[j], moe_down_w[j])
        x = x + mx[5] * _rms_norm(fx, norm_w[i, 3])
        if not last:
            ctx = ctx + mc[2] * _rms_norm(yc, norm_w[i, 1])
            hc = _modulate(_rms_norm(ctx, norm_w[i, 2]), mc[3], mc[4])
            if i % 2 == 0:
                fc = _swiglu(hc, ffn_gu_w[j], ffn_down_w[j])
            else:
                fc = _moe(hc, router_w[j], moe_gu_w[j], moe_down_w[j])
            ctx = ctx + mc[5] * _rms_norm(fc, norm_w[i, 3])
    return x
```

```python
import functools

import numpy as np
import jax
import jax.numpy as jnp
from jax import lax
from jax.experimental import pallas as pl
from jax.experimental.pallas import tpu as pltpu

F32 = jnp.float32
BF16 = jnp.bfloat16

EPS = 1e-6
GRID_W = 64
HEAD_DIM = 64
N_HEADS = 8
N_KV_HEADS = 2
GQ = N_HEADS // N_KV_HEADS
F_GROUPS = 4
F_GROUP_DIM = 128
F_WIDTH = F_GROUPS * F_GROUP_DIM
Q_WIDTH = N_HEADS * HEAD_DIM
KV_WIDTH = N_KV_HEADS * HEAD_DIM
ROPE_THETA = 10000.0
AXIS_ROT_DIM = HEAD_DIM // 2
ATTN_SCALE = HEAD_DIM ** -0.5
SSM_HEAD_DIM = 64
N_SSM_HEADS = 32
N_GROUPS = 4
HEADS_PER_GROUP = N_SSM_HEADS // N_GROUPS
D_STATE = 128
D_CONV = 5
CONV_PAD = D_CONV // 2
CHUNK = 128
N_EXPERTS = 8
LANES = 128
SUBLANES = 8
FFT_N2 = 128
VMEM_LIMIT = 56 * 1024 * 1024


def _cparams(*sem):
    return pltpu.CompilerParams(dimension_semantics=sem, vmem_limit_bytes=VMEM_LIMIT)


def _pick(n, candidates):
    for c in candidates:
        if n % c == 0:
            return c
    return n


def _sigmoid(x):
    return 1.0 / (1.0 + jnp.exp(-x))


def _silu(x):
    return x * _sigmoid(x)


def _rms(x, w):
    return x * lax.rsqrt(jnp.mean(x * x, axis=-1, keepdims=True) + EPS) * w


def _norm_mod(x, nw, shift, scale):
    return _rms(x, nw) * (1.0 + scale) + shift


def _dot(a, b):
    return jnp.dot(a, b, preferred_element_type=F32)


def _split_bf16(x, parts):
    out = []
    r = x
    for _ in range(parts):
        p = r.astype(BF16)
        out.append(p)
        r = r - p.astype(F32)
    return out


def _dot_split_lhs(x, m, parts):
    acc = None
    for p in _split_bf16(x, parts):
        t = _dot(p, m)
        acc = t if acc is None else acc + t
    return acc


def _dot_split_rhs(m, x, parts):
    acc = None
    for p in _split_bf16(x, parts):
        t = _dot(m, p)
        acc = t if acc is None else acc + t
    return acc


def _mod_kernel(c_ref, w_ref, b_ref, o_ref):
    s = _silu(c_ref[...])
    o_ref[...] = jnp.dot(s, w_ref[...], preferred_element_type=F32,
                         precision=lax.Precision.HIGHEST) + b_ref[...]


def _modulation(cvec, ada_w, ada_b):
    depth, d, n = ada_w.shape
    tn = _pick(n, (1536, 1024, 512))
    rows = cvec.shape[0]
    return pl.pallas_call(
        _mod_kernel,
        out_shape=jax.ShapeDtypeStruct((depth, rows, n), F32),
        grid=(depth, n // tn),
        in_specs=[pl.BlockSpec((rows, d), lambda i, j: (0, 0)),
                  pl.BlockSpec((None, d, tn), lambda i, j: (i, 0, j)),
                  pl.BlockSpec((None, 1, tn), lambda i, j: (i, 0, j))],
        out_specs=pl.BlockSpec((None, rows, tn), lambda i, j: (i, 0, j)),
        compiler_params=_cparams("parallel", "parallel"),
        name="adaln_modulation",
    )(cvec, ada_w, ada_b.reshape(depth, 1, n))


def _nml_kernel(n_w, x_ref, nw_ref, sh_ref, sc_ref, *refs):
    h = _norm_mod(x_ref[...], nw_ref[...], sh_ref[...], sc_ref[...]).astype(BF16)
    for w_ref, o_ref in zip(refs[:n_w], refs[n_w:]):
        o_ref[...] = _dot(h, w_ref[...]).astype(o_ref.dtype)


def _norm_mod_linear(x, nw, shift, scale, weights, out_dtypes, tm_max=512):
    b, s, d = x.shape
    tm = _pick(s, (tm_max, 256, 128))
    n_w = len(weights)
    vec = pl.BlockSpec((None, 1, d), lambda bi, i: (bi, 0, 0))
    in_specs = [pl.BlockSpec((None, tm, d), lambda bi, i: (bi, i, 0)),
                pl.BlockSpec((1, d), lambda bi, i: (0, 0)), vec, vec]
    in_specs += [pl.BlockSpec(w.shape, lambda bi, i: (0, 0)) for w in weights]
    out_shape = [jax.ShapeDtypeStruct((b, s, w.shape[1]), dt) for w, dt in zip(weights, out_dtypes)]
    out_specs = [pl.BlockSpec((None, tm, w.shape[1]), lambda bi, i: (bi, i, 0)) for w in weights]
    return pl.pallas_call(
        functools.partial(_nml_kernel, n_w),
        out_shape=out_shape,
        grid=(b, s // tm),
        in_specs=in_specs,
        out_specs=out_specs,
        compiler_params=_cparams("parallel", "parallel"),
        name="norm_mod_linear",
    )(x, nw, shift, scale, *weights)


def _qkprep_kernel(qkv_ref, cos_ref, sin_ref, qw_ref, kw_ref, bd_ref, q_ref, k_ref, v_ref):
    tm = qkv_ref.shape[0]
    lane = lax.broadcasted_iota(jnp.int32, (tm, LANES), 1)
    even = (lane % 2) == 0
    low = lane < HEAD_DIM
    cos = cos_ref[...]
    sin = sin_ref[...]
    bd = bd_ref[...]

    def prep(xb, w):
        ms = _dot_split_lhs(xb * xb, bd, 2)
        y = xb * lax.rsqrt(ms + EPS) * w
        swapped = jnp.where(even, pltpu.roll(y, LANES - 1, 1), pltpu.roll(y, 1, 1))
        return y * cos + swapped * sin

    qw = qw_ref[...]
    for cblk in range(Q_WIDTH // LANES):
        r = prep(qkv_ref[:, cblk * LANES:(cblk + 1) * LANES], qw) * ATTN_SCALE
        r_sw = pltpu.roll(r, HEAD_DIM, 1)
        kvh = (2 * cblk) // GQ
        keep = low if kvh == 0 else jnp.logical_not(low)
        for j in range(2):
            src = r if j == kvh else r_sw
            q_ref[2 * cblk + j] = jnp.where(keep, src, 0.0).astype(q_ref.dtype)
    k_ref[...] = prep(qkv_ref[:, Q_WIDTH:Q_WIDTH + KV_WIDTH], kw_ref[...]).astype(k_ref.dtype)
    v_ref[...] = qkv_ref[:, Q_WIDTH + KV_WIDTH:Q_WIDTH + 2 * KV_WIDTH].astype(v_ref.dtype)


def _qk_prepare(qkv, cos_t, sin_t, qw, kw):
    b, s, _ = qkv.shape
    tm = _pick(s, (512, 256, 128))
    bd = np.kron(np.eye(2, dtype=np.float32), np.full((HEAD_DIM, HEAD_DIM), 1.0 / HEAD_DIM, np.float32))
    row = lambda bi, i: (i, 0)
    const = lambda bi, i: (0, 0)
    return pl.pallas_call(
        _qkprep_kernel,
        out_shape=[jax.ShapeDtypeStruct((b, N_HEADS, s, LANES), BF16),
                   jax.ShapeDtypeStruct((b, s, KV_WIDTH), BF16),
                   jax.ShapeDtypeStruct((b, s, KV_WIDTH), BF16)],
        grid=(b, s // tm),
        in_specs=[pl.BlockSpec((None, tm, qkv.shape[2]), lambda bi, i: (bi, i, 0)),
                  pl.BlockSpec((tm, LANES), row), pl.BlockSpec((tm, LANES), row),
                  pl.BlockSpec((1, LANES), const), pl.BlockSpec((1, LANES), const),
                  pl.BlockSpec((LANES, LANES), const)],
        out_specs=[pl.BlockSpec((None, N_HEADS, tm, LANES), lambda bi, i: (bi, 0, i, 0)),
                   pl.BlockSpec((None, tm, KV_WIDTH), lambda bi, i: (bi, i, 0)),
                   pl.BlockSpec((None, tm, KV_WIDTH), lambda bi, i: (bi, i, 0))],
        compiler_params=_cparams("parallel", "parallel"),
        name="qk_norm_rope",
    )(qkv, cos_t, sin_t, qw, kw, jnp.asarray(bd, BF16))


def _attn_kernel(nk, q_ref, k_ref, v_ref, o_ref, m_scr, l_scr, acc_scr):
    kv = pl.program_id(2)

    @pl.when(kv == 0)
    def _():
        m_scr[...] = jnp.full(m_scr.shape, -jnp.inf, F32)
        l_scr[...] = jnp.zeros(l_scr.shape, F32)
        acc_scr[...] = jnp.zeros(acc_scr.shape, F32)

    k = k_ref[...]
    v = v_ref[...]

    def head(hd, carry):
        s = lax.dot_general(q_ref[hd], k, (((1,), (1,)), ((), ())), preferred_element_type=F32)
        m_prev = m_scr[hd]
        m_new = jnp.maximum(m_prev, jnp.max(s, axis=-1, keepdims=True))
        p = jnp.exp(s - m_new)
        alpha = jnp.exp(m_prev - m_new)
        l_scr[hd] = alpha * l_scr[hd] + jnp.sum(p, axis=-1, keepdims=True)
        acc_scr[hd] = alpha * acc_scr[hd] + _dot(p.astype(BF16), v)
        m_scr[hd] = m_new
        return carry

    lax.fori_loop(0, N_HEADS, head, 0)

    @pl.when(kv == nk - 1)
    def _():
        tq = o_ref.shape[0]
        low = lax.broadcasted_iota(jnp.int32, (tq, LANES), 1) < HEAD_DIM
        for g in range(GQ):
            o0 = acc_scr[g] / l_scr[g]
            o1 = acc_scr[GQ + g] / l_scr[GQ + g]
            o_ref[:, g * LANES:(g + 1) * LANES] = jnp.where(low, o0, o1).astype(o_ref.dtype)


def _attention(q, k, v):
    b, _, s, _ = q.shape
    skv = k.shape[1]
    tq = _pick(s, (512, 256, 128))
    tk = _pick(skv, (1280, 640, 256, 128))
    nk = skv // tk
    return pl.pallas_call(
        functools.partial(_attn_kernel, nk),
        out_shape=jax.ShapeDtypeStruct((b, s, Q_WIDTH), BF16),
        grid=(b, s // tq, nk),
        in_specs=[pl.BlockSpec((None, N_HEADS, tq, LANES), lambda bi, i, j: (bi, 0, i, 0)),
                  pl.BlockSpec((None, tk, KV_WIDTH), lambda bi, i, j: (bi, j, 0)),
                  pl.BlockSpec((None, tk, KV_WIDTH), lambda bi, i, j: (bi, j, 0))],
        out_specs=pl.BlockSpec((None, tq, Q_WIDTH), lambda bi, i, j: (bi, i, 0)),
        scratch_shapes=[pltpu.VMEM((N_HEADS, tq, 1), F32), pltpu.VMEM((N_HEADS, tq, 1), F32),
                        pltpu.VMEM((N_HEADS, tq, LANES), F32)],
        compiler_params=_cparams("parallel", "parallel", "arbitrary"),
        name="flash_attention",
    )(q, k, v)


def _dft_cos_sin(n):
    idx = np.arange(n, dtype=np.int64)
    ang = 2.0 * np.pi * ((idx[:, None] * idx[None, :]) % n).astype(np.float64) / n
    return np.cos(ang), np.sin(ang)


def _fft1_kernel(x_ref, w_ref, yr_ref, yi_ref):
    n1 = x_ref.shape[0]
    y = _dot(w_ref[...], x_ref[...].astype(BF16))
    yr_ref[...] = y[:n1]
    yi_ref[...] = y[n1:]


def _fft2_kernel(kb, scale, yr_ref, yi_ref, tc_ref, ts_ref, w2_ref, wc_ref, o_ref):
    n2 = FFT_N2
    w2 = w2_ref[...]
    wc = wc_ref[...]
    for j in range(kb):
        yr = yr_ref[j]
        yi = yi_ref[j]
        c = tc_ref[:, j:j + 1]
        s = ts_ref[:, j:j + 1]
        z = jnp.concatenate([yr * c + yi * s, yi * c - yr * s], axis=0).astype(BF16)
        gq = _dot(w2, z)
        for g in range(F_GROUPS):
            cols = slice(g * F_GROUP_DIM, (g + 1) * F_GROUP_DIM)
            gg = jnp.concatenate([gq[:n2, cols], gq[n2:, cols]], axis=1).astype(BF16)
            o_ref[:, j * F_WIDTH + g * F_GROUP_DIM:j * F_WIDTH + (g + 1) * F_GROUP_DIM] = _dot(gg, wc) * scale


def _fourier_mix_long(u):
    b, l, fw = u.shape
    n2 = FFT_N2
    n1 = l // n2
    cols = n2 * fw
    c1, s1 = _dft_cos_sin(n1)
    w1 = jnp.asarray(np.concatenate([c1, -s1], axis=0), BF16)
    c2, s2 = _dft_cos_sin(n2)
    w2 = jnp.asarray(np.block([[c2, s2], [-s2, c2]]), BF16)
    cc, sc = _dft_cos_sin(F_GROUP_DIM)
    wc = jnp.asarray(np.concatenate([cc, sc], axis=0), BF16)
    k1 = np.arange(n1, dtype=np.int64)
    nn = np.arange(n2, dtype=np.int64)
    ang = 2.0 * np.pi * ((nn[:, None] * k1[None, :]) % l).astype(np.float64) / l
    kb = SUBLANES
    tw_c = jnp.asarray(np.cos(ang).reshape(n2, n1 // kb, kb).transpose(1, 0, 2), F32)
    tw_s = jnp.asarray(np.sin(ang).reshape(n2, n1 // kb, kb).transpose(1, 0, 2), F32)
    tn = _pick(cols, (4096,))
    x2 = u.reshape(b, n1, cols)
    yr, yi = pl.pallas_call(
        _fft1_kernel,
        out_shape=[jax.ShapeDtypeStruct((b, n1, cols), F32)] * 2,
        grid=(b, cols // tn),
        in_specs=[pl.BlockSpec((None, n1, tn), lambda bi, j: (bi, 0, j)),
                  pl.BlockSpec((2 * n1, n1), lambda bi, j: (0, 0))],
        out_specs=[pl.BlockSpec((None, n1, tn), lambda bi, j: (bi, 0, j))] * 2,
        compiler_params=_cparams("parallel", "parallel"),
        name="fft_stage1",
    )(x2, w1)
    yr = yr.reshape(b, n1, n2, fw)
    yi = yi.reshape(b, n1, n2, fw)
    scale = float(1.0 / np.sqrt(float(l) * F_GROUP_DIM))
    yblk = pl.BlockSpec((None, kb, n2, fw), lambda bi, j: (bi, j, 0, 0))
    tblk = pl.BlockSpec((None, n2, kb), lambda bi, j: (j, 0, 0))
    out = pl.pallas_call(
        functools.partial(_fft2_kernel, kb, scale),
        out_shape=jax.ShapeDtypeStruct((b, n2, n1 * fw), F32),
        grid=(b, n1 // kb),
        in_specs=[yblk, yblk, tblk, tblk,
                  pl.BlockSpec((2 * n2, 2 * n2), lambda bi, j: (0, 0)),
                  pl.BlockSpec((2 * F_GROUP_DIM, F_GROUP_DIM), lambda bi, j: (0, 0))],
        out_specs=pl.BlockSpec((None, n2, kb * fw), lambda bi, j: (bi, 0, j)),
        compiler_params=_cparams("parallel", "parallel"),
        name="fft_stage2",
    )(yr, yi, tw_c, tw_s, w2, wc)
    return out.reshape(b, l, fw)


def _fftc_kernel(scale, u_ref, wl_ref, wc_ref, o_ref):
    fw = u_ref.shape[1]
    ab = _dot(u_ref[...].astype(BF16), wc_ref[...])
    st = jnp.concatenate([ab[:, :fw], ab[:, fw:]], axis=0).astype(BF16)
    o_ref[...] = _dot(wl_ref[...], st) * scale


def _fourier_mix_short(u):
    b, l, fw = u.shape
    cl, sl = _dft_cos_sin(l)
    wl = jnp.asarray(np.concatenate([cl, -sl], axis=1), BF16)
    cc, sc = _dft_cos_sin(F_GROUP_DIM)
    eye = np.eye(F_GROUPS)
    wc = jnp.asarray(np.concatenate([np.kron(eye, cc), np.kron(eye, sc)], axis=1), BF16)
    scale = float(1.0 / np.sqrt(float(l) * F_GROUP_DIM))
    return pl.pallas_call(
        functools.partial(_fftc_kernel, scale),
        out_shape=jax.ShapeDtypeStruct((b, l, fw), F32),
        grid=(b,),
        in_specs=[pl.BlockSpec((None, l, fw), lambda bi: (bi, 0, 0)),
                  pl.BlockSpec((l, 2 * l), lambda bi: (0, 0)),
                  pl.BlockSpec((fw, 2 * fw), lambda bi: (0, 0))],
        out_specs=pl.BlockSpec((None, l, fw), lambda bi: (bi, 0, 0)),
        compiler_params=_cparams("parallel"),
        name="fft_direct",
    )(u, wl, wc)


def _mixout_kernel(f_ref, a_ref, wf_ref, wa_ref, x_ref, gate_ref, nw_ref, o_ref):
    y = _dot(f_ref[...].astype(BF16), wf_ref[...]) + _dot(a_ref[...].astype(BF16), wa_ref[...])
    o_ref[...] = x_ref[...] + gate_ref[...] * _rms(y, nw_ref[...])


def _mixer_out(f, a, wf, wa, x, gate, nw):
    b, s, d = x.shape
    tm = _pick(s, (512, 256, 128))
    row = lambda w: pl.BlockSpec((None, tm, w), lambda bi, i: (bi, i, 0))
    const = lambda shp: pl.BlockSpec(shp, lambda bi, i: (0, 0))
    return pl.pallas_call(
        _mixout_kernel,
        out_shape=jax.ShapeDtypeStruct((b, s, d), F32),
        grid=(b, s // tm),
        in_specs=[row(f.shape[2]), row(a.shape[2]), const(wf.shape), const(wa.shape), row(d),
                  pl.BlockSpec((None, 1, d), lambda bi, i: (bi, 0, 0)), const((1, d))],
        out_specs=row(d),
        compiler_params=_cparams("parallel", "parallel"),
        name="mixer_out_residual",
    )(f, a, wf, wa, x, gate, nw)


def _ffn_kernel(n_e, n_f, use_comb, *refs):
    if use_comb:
        (x_ref, nw1_ref, sh_ref, sc_ref, wg_ref, wu_ref, wd_ref, gate_ref, nw2_ref, comb_ref,
         o_ref, h_scr, acc_scr) = refs
    else:
        (x_ref, nw1_ref, sh_ref, sc_ref, wg_ref, wu_ref, wd_ref, gate_ref, nw2_ref,
         o_ref, h_scr, acc_scr) = refs
    e = pl.program_id(2)
    f = pl.program_id(3)

    @pl.when((e == 0) & (f == 0))
    def _():
        h_scr[...] = _norm_mod(x_ref[...], nw1_ref[...], sh_ref[...], sc_ref[...]).astype(BF16)
        acc_scr[...] = jnp.zeros(acc_scr.shape, F32)

    h = h_scr[...]
    a = _silu(_dot(h, wg_ref[...])) * _dot(h, wu_ref[...])
    if use_comb:
        comb = comb_ref[...]
        lane = lax.broadcasted_iota(jnp.int32, comb.shape, 1)
        a = a * jnp.sum(jnp.where(lane == e, comb, 0.0), axis=-1, keepdims=True)
    acc_scr[...] += _dot(a.astype(BF16), wd_ref[...])

    @pl.when((e == n_e - 1) & (f == n_f - 1))
    def _():
        o_ref[...] = x_ref[...] + gate_ref[...] * _rms(acc_scr[...], nw2_ref[...])


def _ffn(x, nw1, shift, scale, w_gu, w_down, gate, nw2, comb=None):
    b, s, d = x.shape
    n_e, _, f2 = w_gu.shape
    ff = f2 // 2
    tm = _pick(s, (1024, 512, 256))
    tf = _pick(ff, (512, 256))
    n_f = ff // tf
    vec = pl.BlockSpec((None, 1, d), lambda bi, i, e, f: (bi, 0, 0))
    const = pl.BlockSpec((1, d), lambda bi, i, e, f: (0, 0))
    row = pl.BlockSpec((None, tm, d), lambda bi, i, e, f: (bi, i, 0))
    in_specs = [row, const, vec, vec,
                pl.BlockSpec((None, d, tf), lambda bi, i, e, f: (e, 0, f)),
                pl.BlockSpec((None, d, tf), lambda bi, i, e, f: (e, 0, f + n_f)),
                pl.BlockSpec((None, tf, d), lambda bi, i, e, f: (e, f, 0)),
                vec, const]
    args = [x, nw1, shift, scale, w_gu, w_gu, w_down, gate, nw2]
    if comb is not None:
        in_specs.append(pl.BlockSpec((None, tm, LANES), lambda bi, i, e, f: (bi, i, 0)))
        args.append(comb)
    return pl.pallas_call(
        functools.partial(_ffn_kernel, n_e, n_f, comb is not None),
        out_shape=jax.ShapeDtypeStruct((b, s, d), F32),
        grid=(b, s // tm, n_e, n_f),
        in_specs=in_specs,
        out_specs=row,
        scratch_shapes=[pltpu.VMEM((tm, d), BF16), pltpu.VMEM((tm, d), F32)],
        compiler_params=_cparams("parallel", "parallel", "arbitrary", "arbitrary"),
        name="swiglu_residual",
    )(*args)


def _router_kernel(x_ref, nw_ref, sh_ref, sc_ref, rw_ref, comb_ref):
    h = _norm_mod(x_ref[...], nw_ref[...], sh_ref[...], sc_ref[...])
    h_hi, h_lo = _split_bf16(h, 2)
    w_hi, w_lo = _split_bf16(rw_ref[...], 2)
    logits = _dot(h_hi, w_hi) + _dot(h_hi, w_lo) + _dot(h_lo, w_hi)
    lane = lax.broadcasted_iota(jnp.int32, logits.shape, 1)
    neg = -jnp.inf
    logits = jnp.where(lane < N_EXPERTS, logits, neg)
    m1 = jnp.max(logits, axis=-1, keepdims=True)
    i1 = jnp.min(jnp.where(logits == m1, lane, LANES), axis=-1, keepdims=True)
    rest = jnp.where(lane == i1, neg, logits)
    m2 = jnp.max(rest, axis=-1, keepdims=True)
    i2 = jnp.min(jnp.where(rest == m2, lane, LANES), axis=-1, keepdims=True)
    e2 = jnp.exp(m2 - m1)
    den = 1.0 + e2
    comb_ref[...] = jnp.where(lane == i1, 1.0 / den, 0.0) + jnp.where(lane == i2, e2 / den, 0.0)


def _router(x, nw, shift, scale, rw_pad):
    b, s, d = x.shape
    tm = _pick(s, (512, 256, 128))
    vec = pl.BlockSpec((None, 1, d), lambda bi, i: (bi, 0, 0))
    return pl.pallas_call(
        _router_kernel,
        out_shape=jax.ShapeDtypeStruct((b, s, LANES), F32),
        grid=(b, s // tm),
        in_specs=[pl.BlockSpec((None, tm, d), lambda bi, i: (bi, i, 0)),
                  pl.BlockSpec((1, d), lambda bi, i: (0, 0)), vec, vec,
                  pl.BlockSpec((d, LANES), lambda bi, i: (0, 0))],
        out_specs=pl.BlockSpec((None, tm, LANES), lambda bi, i: (bi, i, 0)),
        compiler_params=_cparams("parallel", "parallel"),
        name="router_top2",
    )(x, nw, shift, scale, rw_pad)


def _conv_kernel(nt, prev_ref, cur_ref, next_ref, w_ref, b_ref, xs_ref, bm_ref, cm_ref, buf):
    i = pl.program_id(1)
    tm = cur_ref.shape[0]
    halo = SUBLANES
    buf[halo:halo + tm, :] = cur_ref[...]
    buf[0:halo, :] = jnp.where(i > 0, prev_ref[...], 0.0)
    buf[halo + tm:2 * halo + tm, :] = jnp.where(i < nt - 1, next_ref[...], 0.0)
    acc = None
    for j in range(D_CONV):
        start = halo - CONV_PAD + j
        t = w_ref[j:j + 1, :] * buf[start:start + tm, :]
        acc = t if acc is None else acc + t
    y = _silu(acc + b_ref[...])
    d_inner = xs_ref.shape[1]
    gn = bm_ref.shape[1]
    xs_ref[...] = y[:, :d_inner]
    bm_ref[...] = y[:, d_inner:d_inner + gn]
    cm_ref[...] = y[:, d_inner + gn:]


def _conv_silu(xbc, conv_w, conv_b, d_inner):
    b, s, c = xbc.shape
    gn = (c - d_inner) // 2
    tm = _pick(s, (512, 256, 128))
    nt = s // tm
    hb = tm // SUBLANES
    last = s // SUBLANES - 1
    row = lambda w: pl.BlockSpec((None, tm, w), lambda bi, i: (bi, i, 0))
    return pl.pallas_call(
        functools.partial(_conv_kernel, nt),
        out_shape=[jax.ShapeDtypeStruct((b, s, d_inner), F32),
                   jax.ShapeDtypeStruct((b, s, gn), F32), jax.ShapeDtypeStruct((b, s, gn), F32)],
        grid=(b, nt),
        in_specs=[pl.BlockSpec((None, SUBLANES, c), lambda bi, i: (bi, jnp.maximum(i * hb - 1, 0), 0)),
                  row(c),
                  pl.BlockSpec((None, SUBLANES, c), lambda bi, i: (bi, jnp.minimum((i + 1) * hb, last), 0)),
                  pl.BlockSpec((D_CONV, c), lambda bi, i: (0, 0)),
                  pl.BlockSpec((1, c), lambda bi, i: (0, 0))],
        out_specs=[row(d_inner), row(gn), row(gn)],
        scratch_shapes=[pltpu.VMEM((tm + 2 * SUBLANES, c), F32)],
        compiler_params=_cparams("parallel", "parallel"),
        name="dwconv_silu",
    )(xbc, xbc, xbc, conv_w, conv_b)


def _softplus(x):
    return jnp.maximum(x, 0.0) + jnp.log(1.0 + jnp.exp(-jnp.abs(x)))


def _ssd_kernel(direction, nc, xs_ref, bm_ref, cm_ref, dt_ref, dtb_ref, a_ref, h0_ref, e_ref,
                y_ref, hT_ref, st_scr):
    c = pl.program_id(1)

    @pl.when(c == 0)
    def _():
        st_scr[...] = h0_ref[...]

    q = CHUNK
    fwd = direction == 0
    dt = _softplus(dt_ref[...] + dtb_ref[...])
    d_a = dt * a_ref[...]
    row = lax.broadcasted_iota(jnp.int32, (q, q), 0)
    col = lax.broadcasted_iota(jnp.int32, (q, q), 1)
    causal = (row >= col) if fwd else (row <= col)
    tri = jnp.where(causal, 1.0, 0.0).astype(BF16)
    a_cum = _dot_split_rhs(tri, d_a, 3)
    a_cum_t = a_cum.T
    expand = e_ref[...]
    dt_x = _dot_split_lhs(dt, expand, 2)
    a_x = _dot_split_lhs(a_cum, expand, 3)
    last = q - 1 if fwd else 0
    a_last = a_x[last:last + 1, :]
    xdt = xs_ref[...] * dt_x
    xdt_b = xdt.astype(BF16)
    w_state = (jnp.exp(a_last - a_x) * xdt).astype(BF16)
    exp_a = jnp.exp(a_x)
    exp_last = jnp.exp(a_last)
    low = lax.broadcasted_iota(jnp.int32, (q, LANES), 1) < SSM_HEAD_DIM
    gw = HEADS_PER_GROUP * SSM_HEAD_DIM
    for g in range(N_GROUPS):
        gs = slice(g * gw, (g + 1) * gw)
        b_g = bm_ref[:, g * D_STATE:(g + 1) * D_STATE]
        c_b = cm_ref[:, g * D_STATE:(g + 1) * D_STATE].astype(BF16)
        cb = lax.dot_general(c_b, b_g.astype(BF16), (((1,), (1,)), ((), ())), preferred_element_type=F32)
        st_g = st_scr[:, gs]
        y_off = _dot(c_b, st_g.astype(BF16)) * exp_a[:, gs]
        s_new = _dot(b_g.T.astype(BF16), w_state[:, gs])
        for pr in range(HEADS_PER_GROUP // 2):
            ys = []
            for j in range(2):
                hl = direction * N_SSM_HEADS + g * HEADS_PER_GROUP + 2 * pr + j
                seg = a_cum[:, hl:hl + 1] - a_cum_t[hl:hl + 1, :]
                dec = jnp.exp(jnp.where(causal, seg, -jnp.inf))
                ys.append(_dot((cb * dec).astype(BF16), xdt_b[:, g * gw + pr * LANES:g * gw + (pr + 1) * LANES]))
            y_ref[:, g * gw + pr * LANES:g * gw + (pr + 1) * LANES] = (
                jnp.where(low, ys[0], ys[1]) + y_off[:, pr * LANES:(pr + 1) * LANES])
        st_scr[:, gs] = exp_last[:, gs] * st_g + s_new

    @pl.when(c == nc - 1)
    def _():
        hT_ref[...] = st_scr[...]


def _ssd_scan(direction, xs, bm, cm, dt_raw, dt_bias, a_neg, h0):
    b, s, d_inner = xs.shape
    gn = bm.shape[2]
    nc = s // CHUNK
    heads = np.arange(d_inner) // SSM_HEAD_DIM + direction * N_SSM_HEADS
    expand = jnp.asarray(np.arange(LANES)[:, None] == heads[None, :], BF16)
    if direction == 0:
        cidx = lambda bi, c: (bi, c, 0)
    else:
        cidx = lambda bi, c: (bi, nc - 1 - c, 0)
    const = lambda bi, c: (0, 0)
    st_spec = pl.BlockSpec((None, D_STATE, d_inner), lambda bi, c: (bi, 0, 0))
    return pl.pallas_call(
        functools.partial(_ssd_kernel, direction, nc),
        out_shape=[jax.ShapeDtypeStruct((b, s, d_inner), F32),
                   jax.ShapeDtypeStruct((b, D_STATE, d_inner), F32)],
        grid=(b, nc),
        in_specs=[pl.BlockSpec((None, CHUNK, d_inner), cidx),
                  pl.BlockSpec((None, CHUNK, gn), cidx), pl.BlockSpec((None, CHUNK, gn), cidx),
                  pl.BlockSpec((None, CHUNK, LANES), cidx),
                  pl.BlockSpec((1, LANES), const), pl.BlockSpec((1, LANES), const),
                  st_spec, pl.BlockSpec((LANES, d_inner), const)],
        out_specs=[pl.BlockSpec((None, CHUNK, d_inner), cidx), st_spec],
        scratch_shapes=[pltpu.VMEM((D_STATE, d_inner), F32)],
        compiler_params=_cparams("parallel", "arbitrary"),
        name="ssd_scan",
    )(xs, bm, cm, dt_raw, dt_bias, a_neg, h0, expand)


def _ssdout_kernel(yf_ref, yb_ref, xs_ref, z_ref, dsk_ref, gnw_ref, w_ref, x_ref, gate_ref, nw_ref, o_ref):
    y = yf_ref[...] + yb_ref[...] + dsk_ref[...] * xs_ref[...]
    gy = y * _silu(z_ref[...])
    gnw = gnw_ref[...]
    gw = gy.shape[1] // N_GROUPS
    parts = []
    for g in range(N_GROUPS):
        parts.append(_rms(gy[:, g * gw:(g + 1) * gw], gnw[:, g * gw:(g + 1) * gw]).astype(BF16))
    out = _dot(jnp.concatenate(parts, axis=1), w_ref[...])
    o_ref[...] = x_ref[...] + gate_ref[...] * _rms(out, nw_ref[...])


def _ssd_out(yf, yb, xs, z, d_skip, gnw, w_out, x, gate, nw):
    b, s, d = x.shape
    d_inner = xs.shape[2]
    tm = _pick(s, (256, 128))
    row = lambda w: pl.BlockSpec((None, tm, w), lambda bi, i: (bi, i, 0))
    const = lambda shp: pl.BlockSpec(shp, lambda bi, i: (0, 0))
    return pl.pallas_call(
        _ssdout_kernel,
        out_shape=jax.ShapeDtypeStruct((b, s, d), F32),
        grid=(b, s // tm),
        in_specs=[row(d_inner), row(d_inner), row(d_inner), row(d_inner),
                  const((1, d_inner)), const((1, d_inner)), const(w_out.shape), row(d),
                  pl.BlockSpec((None, 1, d), lambda bi, i: (bi, 0, 0)), const((1, d))],
        out_specs=row(d),
        compiler_params=_cparams("parallel", "parallel"),
        name="ssd_gate_out_residual",
    )(yf, yb, xs, z, d_skip, gnw, w_out, x, gate, nw)


def _rope_tables(seq):
    rows = seq // GRID_W
    row = jnp.repeat(jnp.arange(rows, dtype=F32), GRID_W)
    col = jnp.tile(jnp.arange(GRID_W, dtype=F32), rows)
    freqs = ROPE_THETA ** (-jnp.arange(0, AXIS_ROT_DIM, 2, dtype=F32) / AXIS_ROT_DIM)
    ang = jnp.concatenate([row[:, None] * freqs, col[:, None] * freqs], axis=-1)
    cos = jnp.repeat(jnp.cos(ang), 2, axis=-1)
    sin = jnp.repeat(jnp.sin(ang), 2, axis=-1) * jnp.tile(jnp.asarray([-1.0, 1.0], F32), HEAD_DIM // 2)
    return jnp.tile(cos, (1, LANES // HEAD_DIM)), jnp.tile(sin, (1, LANES // HEAD_DIM))


def _mod_vectors(mods_i, batch):
    d = mods_i.shape[1] // 6
    mx = [mods_i[:batch, k * d:(k + 1) * d].reshape(batch, 1, d) for k in range(6)]
    mc = [jnp.broadcast_to(mods_i[batch, k * d:(k + 1) * d].reshape(1, 1, d), (batch, 1, d)) for k in range(6)]
    return mx, mc


def _even_mixer(x, ctx, nw0, nw1, mx, mc, w_in, qnw, knw, w_out, rope, need_ctx):
    w_u = w_in[:, :F_WIDTH].astype(BF16)
    w_qkv = w_in[:, F_WIDTH:].astype(BF16)
    qw = jnp.tile(qnw.reshape(1, HEAD_DIM), (1, LANES // HEAD_DIM))
    kw = jnp.tile(knw.reshape(1, HEAD_DIM), (1, LANES // HEAD_DIM))
    w_f = w_out[:F_WIDTH].astype(BF16)
    d = w_out.shape[1]
    w_a = (w_out[F_WIDTH:].reshape(N_KV_HEADS, GQ, HEAD_DIM, d).transpose(1, 0, 2, 3)
           .reshape(Q_WIDTH, d).astype(BF16))
    lc = ctx.shape[1]
    ones = jnp.ones((lc, LANES), F32)
    zeros = jnp.zeros((lc, LANES), F32)

    u_c, qkv_c = _norm_mod_linear(ctx, nw0, mc[0], mc[1], [w_u, w_qkv], [F32, F32])
    q_c, k_c, v_c = _qk_prepare(qkv_c, ones, zeros, qw, kw)
    u_x, qkv_x = _norm_mod_linear(x, nw0, mx[0], mx[1], [w_u, w_qkv], [F32, F32])
    q_x, k_x, v_x = _qk_prepare(qkv_x, rope[0], rope[1], qw, kw)
    k_all = jnp.concatenate([k_c, k_x], axis=1)
    v_all = jnp.concatenate([v_c, v_x], axis=1)
    a_x = _attention(q_x, k_all, v_all)
    f_x = _fourier_mix_long(u_x)
    x_new = _mixer_out(f_x, a_x, w_f, w_a, x, mx[2], nw1)
    if not need_ctx:
        return x_new, ctx
    a_c = _attention(q_c, k_c, v_c)
    f_c = _fourier_mix_short(u_c)
    return x_new, _mixer_out(f_c, a_c, w_f, w_a, ctx, mc[2], nw1)


def _ssd_branch(h_in, nw0, shift, scale, w_z, w_xbc, w_dt, conv_w, conv_b, dt_bias, a_neg, h0_f, h0_b):
    z, xbc, dt_raw = _norm_mod_linear(h_in, nw0, shift, scale, [w_z, w_xbc, w_dt], [F32, F32, F32], tm_max=256)
    xs, bm, cm = _conv_silu(xbc, conv_w, conv_b, w_z.shape[1])
    y_f, s_f = _ssd_scan(0, xs, bm, cm, dt_raw, dt_bias, a_neg, h0_f)
    y_b, s_b = _ssd_scan(1, xs, bm, cm, dt_raw, dt_bias, a_neg, h0_b)
    return z, xs, y_f, y_b, s_f, s_b


def _odd_mixer(x, ctx, nw0, nw1, mx, mc, w_in, conv_w, conv_b, dt_bias, a_log, d_skip, gnw, w_out, need_ctx):
    d_inner = w_out.shape[0]
    conv_dim = conv_w.shape[1]
    w_z = w_in[:, :d_inner].astype(BF16)
    w_xbc = w_in[:, d_inner:d_inner + conv_dim].astype(BF16)
    n_dt = w_in.shape[1] - d_inner - conv_dim
    w_dt = jnp.pad(w_in[:, d_inner + conv_dim:], ((0, 0), (0, LANES - n_dt))).astype(BF16)
    dtb = jnp.pad(dt_bias.reshape(1, n_dt), ((0, 0), (0, LANES - n_dt)))
    a_neg = jnp.pad(-jnp.exp(a_log.astype(F32)).reshape(1, n_dt), ((0, 0), (0, LANES - n_dt)))
    cb = conv_b.reshape(1, conv_dim)
    dsk = jnp.repeat(d_skip, SSM_HEAD_DIM).reshape(1, d_inner)
    gw = gnw.reshape(1, d_inner)
    w_o = w_out.astype(BF16)
    b = x.shape[0]
    zeros = jnp.zeros((b, D_STATE, d_inner), F32)

    z_c, xs_c, yf_c, yb_c, s_f, s_b = _ssd_branch(ctx, nw0, mc[0], mc[1], w_z, w_xbc, w_dt, conv_w, cb,
                                                  dtb, a_neg, zeros, zeros)
    z_x, xs_x, yf_x, yb_x, _, _ = _ssd_branch(x, nw0, mx[0], mx[1], w_z, w_xbc, w_dt, conv_w, cb,
                                              dtb, a_neg, s_f, s_b)
    x_new = _ssd_out(yf_x, yb_x, xs_x, z_x, dsk, gw, w_o, x, mx[2], nw1)
    if not need_ctx:
        return x_new, ctx
    return x_new, _ssd_out(yf_c, yb_c, xs_c, z_c, dsk, gw, w_o, ctx, mc[2], nw1)


def kernel(x, c, ctx, c_ctx, ada_w, ada_b, norm_w, mix_in_w, q_norm_w, k_norm_w, mix_out_w, ffn_gu_w, ffn_down_w, ssm_in_w, ssm_conv_w, ssm_conv_b, ssm_dt_bias, ssm_a_log, ssm_d, ssm_norm_w, ssm_out_w, router_w, moe_gu_w, moe_down_w):
    batch, seq, d = x.shape
    depth = ada_w.shape[0]
    rope = _rope_tables(seq)
    cvec = jnp.zeros((SUBLANES, d), F32).at[:batch].set(c).at[batch].set(c_ctx)
    mods = _modulation(cvec, ada_w, ada_b)
    for i in range(depth):
        last = i == depth - 1
        j = i // 2
        mx, mc = _mod_vectors(mods[i], batch)
        nw = [norm_w[i, k].reshape(1, d) for k in range(4)]
        if i % 2 == 0:
            x, ctx_mid = _even_mixer(x, ctx, nw[0], nw[1], mx, mc, mix_in_w[j], q_norm_w[j], k_norm_w[j],
                                     mix_out_w[j], rope, not last)
            w_gu = ffn_gu_w[j].astype(BF16)[None]
            w_dn = ffn_down_w[j].astype(BF16)[None]
            x = _ffn(x, nw[2], mx[3], mx[4], w_gu, w_dn, mx[5], nw[3])
            if not last:
                ctx = _ffn(ctx_mid, nw[2], mc[3], mc[4], w_gu, w_dn, mc[5], nw[3])
        else:
            x, ctx_mid = _odd_mixer(x, ctx, nw[0], nw[1], mx, mc, ssm_in_w[j], ssm_conv_w[j], ssm_conv_b[j],
                                    ssm_dt_bias[j], ssm_a_log[j], ssm_d[j], ssm_norm_w[j], ssm_out_w[j],
                                    not last)
            rw = jnp.pad(router_w[j], ((0, 0), (0, LANES - N_EXPERTS)))
            w_gu = moe_gu_w[j].astype(BF16)
            w_dn = moe_down_w[j].astype(BF16)
            comb = _router(x, nw[2], mx[3], mx[4], rw)
            x = _ffn(x, nw[2], mx[3], mx[4], w_gu, w_dn, mx[5], nw[3], comb)
            if not last:
                comb_c = _router(ctx_mid, nw[2], mc[3], mc[4], rw)
                ctx = _ffn(ctx_mid, nw[2], mc[3], mc[4], w_gu, w_dn, mc[5], nw[3], comb_c)
    return x
```

```python
import functools

import numpy as np
import jax
import jax.numpy as jnp
from jax import lax
from jax.experimental import pallas as pl
from jax.experimental.pallas import tpu as pltpu

F32 = jnp.float32
BF16 = jnp.bfloat16

EPS = 1e-6
GRID_W = 64
HEAD_DIM = 64
N_HEADS = 8
N_KV_HEADS = 2
GQ = N_HEADS // N_KV_HEADS
F_GROUPS = 4
F_GROUP_DIM = 128
F_WIDTH = F_GROUPS * F_GROUP_DIM
Q_WIDTH = N_HEADS * HEAD_DIM
KV_WIDTH = N_KV_HEADS * HEAD_DIM
ROPE_THETA = 10000.0
AXIS_ROT_DIM = HEAD_DIM // 2
ATTN_SCALE = HEAD_DIM ** -0.5
LOG2_E = 1.4426950408889634
SSM_HEAD_DIM = 64
N_SSM_HEADS = 32
N_GROUPS = 4
HEADS_PER_GROUP = N_SSM_HEADS // N_GROUPS
D_STATE = 128
D_CONV = 5
CONV_PAD = D_CONV // 2
CHUNK = 128
N_EXPERTS = 8
LANES = 128
SUBLANES = 8
FFT_N2 = 128
VMEM_LIMIT = 56 * 1024 * 1024


def _cparams(*sem):
    return pltpu.CompilerParams(dimension_semantics=sem, vmem_limit_bytes=VMEM_LIMIT)


def _pick(n, candidates):
    for c in candidates:
        if n % c == 0:
            return c
    return n


def _sigmoid(x):
    return 1.0 / (1.0 + jnp.exp(-x))


def _silu(x):
    return x * _sigmoid(x)


def _rms(x, w):
    return x * lax.rsqrt(jnp.mean(x * x, axis=-1, keepdims=True) + EPS) * w


def _norm_mod(x, nw, shift, scale):
    return _rms(x, nw) * (1.0 + scale) + shift


def _dot(a, b):
    return jnp.dot(a, b, preferred_element_type=F32)


def _split_bf16(x, parts):
    out = []
    r = x
    for _ in range(parts):
        p = r.astype(BF16)
        out.append(p)
        r = r - p.astype(F32)
    return out


def _dot_split_lhs(x, m, parts):
    acc = None
    for p in _split_bf16(x, parts):
        t = _dot(p, m)
        acc = t if acc is None else acc + t
    return acc


def _dot_split_rhs(m, x, parts):
    acc = None
    for p in _split_bf16(x, parts):
        t = _dot(m, p)
        acc = t if acc is None else acc + t
    return acc


def _mod_kernel(c_ref, w_ref, b_ref, o_ref):
    s = _silu(c_ref[...])
    o_ref[...] = jnp.dot(s, w_ref[...], preferred_element_type=F32,
                         precision=lax.Precision.HIGHEST) + b_ref[...]


def _modulation(cvec, ada_w, ada_b):
    depth, d, n = ada_w.shape
    tn = _pick(n, (1536, 1024, 512))
    rows = cvec.shape[0]
    return pl.pallas_call(
        _mod_kernel,
        out_shape=jax.ShapeDtypeStruct((depth, rows, n), F32),
        grid=(depth, n // tn),
        in_specs=[pl.BlockSpec((rows, d), lambda i, j: (0, 0)),
                  pl.BlockSpec((None, d, tn), lambda i, j: (i, 0, j)),
                  pl.BlockSpec((None, 1, tn), lambda i, j: (i, 0, j))],
        out_specs=pl.BlockSpec((None, rows, tn), lambda i, j: (i, 0, j)),
        compiler_params=_cparams("parallel", "parallel"),
        name="adaln_modulation",
    )(cvec, ada_w, ada_b.reshape(depth, 1, n))


def _nml_kernel(n_w, x_ref, nw_ref, sh_ref, sc_ref, *refs):
    h = _norm_mod(x_ref[...], nw_ref[...], sh_ref[...], sc_ref[...]).astype(BF16)
    for w_ref, o_ref in zip(refs[:n_w], refs[n_w:]):
        o_ref[...] = _dot(h, w_ref[...]).astype(o_ref.dtype)


def _norm_mod_linear(x, nw, shift, scale, weights, out_dtypes, tm_max=512):
    b, s, d = x.shape
    tm = _pick(s, (tm_max, 256, 128))
    n_w = len(weights)
    vec = pl.BlockSpec((None, 1, d), lambda bi, i: (bi, 0, 0))
    in_specs = [pl.BlockSpec((None, tm, d), lambda bi, i: (bi, i, 0)),
                pl.BlockSpec((1, d), lambda bi, i: (0, 0)), vec, vec]
    in_specs += [pl.BlockSpec(w.shape, lambda bi, i: (0, 0)) for w in weights]
    out_shape = [jax.ShapeDtypeStruct((b, s, w.shape[1]), dt) for w, dt in zip(weights, out_dtypes)]
    out_specs = [pl.BlockSpec((None, tm, w.shape[1]), lambda bi, i: (bi, i, 0)) for w in weights]
    return pl.pallas_call(
        functools.partial(_nml_kernel, n_w),
        out_shape=out_shape,
        grid=(b, s // tm),
        in_specs=in_specs,
        out_specs=out_specs,
        compiler_params=_cparams("parallel", "parallel"),
        name="norm_mod_linear",
    )(x, nw, shift, scale, *weights)


def _qkprep_kernel(qkv_ref, cos_ref, sin_ref, qw_ref, kw_ref, bd_ref, q_ref, k_ref, v_ref):
    tm = qkv_ref.shape[0]
    lane = lax.broadcasted_iota(jnp.int32, (tm, LANES), 1)
    even = (lane % 2) == 0
    low = lane < HEAD_DIM
    cos = cos_ref[...]
    sin = sin_ref[...]
    bd = bd_ref[...]

    def prep(xb, w):
        ms = _dot_split_lhs(xb * xb, bd, 2)
        y = xb * lax.rsqrt(ms + EPS) * w
        swapped = jnp.where(even, pltpu.roll(y, LANES - 1, 1), pltpu.roll(y, 1, 1))
        return y * cos + swapped * sin

    qw = qw_ref[...]
    for cblk in range(Q_WIDTH // LANES):
        r = prep(qkv_ref[:, cblk * LANES:(cblk + 1) * LANES], qw) * (ATTN_SCALE * LOG2_E)
        r_sw = pltpu.roll(r, HEAD_DIM, 1)
        kvh = (2 * cblk) // GQ
        keep = low if kvh == 0 else jnp.logical_not(low)
        for j in range(2):
            src = r if j == kvh else r_sw
            q_ref[2 * cblk + j] = jnp.where(keep, src, 0.0).astype(q_ref.dtype)
    k_ref[...] = prep(qkv_ref[:, Q_WIDTH:Q_WIDTH + KV_WIDTH], kw_ref[...]).astype(k_ref.dtype)
    v_ref[...] = qkv_ref[:, Q_WIDTH + KV_WIDTH:Q_WIDTH + 2 * KV_WIDTH].astype(v_ref.dtype)


def _qk_prepare(qkv, cos_t, sin_t, qw, kw):
    b, s, _ = qkv.shape
    tm = _pick(s, (512, 256, 128))
    bd = np.kron(np.eye(2, dtype=np.float32), np.full((HEAD_DIM, HEAD_DIM), 1.0 / HEAD_DIM, np.float32))
    row = lambda bi, i: (i, 0)
    const = lambda bi, i: (0, 0)
    return pl.pallas_call(
        _qkprep_kernel,
        out_shape=[jax.ShapeDtypeStruct((b, N_HEADS, s, LANES), BF16),
                   jax.ShapeDtypeStruct((b, s, KV_WIDTH), BF16),
                   jax.ShapeDtypeStruct((b, s, KV_WIDTH), BF16)],
        grid=(b, s // tm),
        in_specs=[pl.BlockSpec((None, tm, qkv.shape[2]), lambda bi, i: (bi, i, 0)),
                  pl.BlockSpec((tm, LANES), row), pl.BlockSpec((tm, LANES), row),
                  pl.BlockSpec((1, LANES), const), pl.BlockSpec((1, LANES), const),
                  pl.BlockSpec((LANES, LANES), const)],
        out_specs=[pl.BlockSpec((None, N_HEADS, tm, LANES), lambda bi, i: (bi, 0, i, 0)),
                   pl.BlockSpec((None, tm, KV_WIDTH), lambda bi, i: (bi, i, 0)),
                   pl.BlockSpec((None, tm, KV_WIDTH), lambda bi, i: (bi, i, 0))],
        compiler_params=_cparams("parallel", "parallel"),
        name="qk_norm_rope",
    )(qkv, cos_t, sin_t, qw, kw, jnp.asarray(bd, BF16))


def _attn_kernel(nk, q_ref, k_ref, v_ref, o_ref, m_scr, l_scr, acc_scr):
    kv = pl.program_id(2)

    @pl.when(kv == 0)
    def _():
        m_scr[...] = jnp.full(m_scr.shape, -jnp.inf, F32)
        l_scr[...] = jnp.zeros(l_scr.shape, F32)
        acc_scr[...] = jnp.zeros(acc_scr.shape, F32)

    k = k_ref[...]
    v = v_ref[...]
    n_t = k.shape[0] // LANES

    def head(hd, carry):
        s = lax.dot_general(q_ref[hd], k, (((1,), (1,)), ((), ())), preferred_element_type=F32)
        tiles = [s[:, t * LANES:(t + 1) * LANES] for t in range(n_t)]
        smax = tiles[0]
        for t in range(1, n_t):
            smax = jnp.maximum(smax, tiles[t])
        m_prev = m_scr[hd]
        m_new = jnp.maximum(m_prev, jnp.max(smax, axis=-1, keepdims=True))
        alpha = jnp.exp2(m_prev - m_new)
        ps = []
        lsum = None
        for t in range(n_t):
            p_t = jnp.exp2(tiles[t] - m_new)
            lsum = p_t if lsum is None else lsum + p_t
            ps.append(p_t.astype(BF16))
        l_scr[hd] = alpha * l_scr[hd] + lsum
        acc_scr[hd] = alpha * acc_scr[hd] + _dot(jnp.concatenate(ps, axis=1), v)
        m_scr[hd] = m_new
        return carry

    lax.fori_loop(0, N_HEADS, head, 0, unroll=4)

    @pl.when(kv == nk - 1)
    def _():
        tq = o_ref.shape[0]
        low = lax.broadcasted_iota(jnp.int32, (tq, LANES), 1) < HEAD_DIM
        for g in range(GQ):
            o0 = acc_scr[g] / jnp.sum(l_scr[g], axis=-1, keepdims=True)
            o1 = acc_scr[GQ + g] / jnp.sum(l_scr[GQ + g], axis=-1, keepdims=True)
            o_ref[:, g * LANES:(g + 1) * LANES] = jnp.where(low, o0, o1).astype(o_ref.dtype)


def _attention(q, k, v):
    b, _, s, _ = q.shape
    skv = k.shape[1]
    tq = _pick(s, (512, 256, 128))
    tk = _pick(skv, (1280, 640, 256, 128))
    nk = skv // tk
    return pl.pallas_call(
        functools.partial(_attn_kernel, nk),
        out_shape=jax.ShapeDtypeStruct((b, s, Q_WIDTH), BF16),
        grid=(b, s // tq, nk),
        in_specs=[pl.BlockSpec((None, N_HEADS, tq, LANES), lambda bi, i, j: (bi, 0, i, 0)),
                  pl.BlockSpec((None, tk, KV_WIDTH), lambda bi, i, j: (bi, j, 0)),
                  pl.BlockSpec((None, tk, KV_WIDTH), lambda bi, i, j: (bi, j, 0))],
        out_specs=pl.BlockSpec((None, tq, Q_WIDTH), lambda bi, i, j: (bi, i, 0)),
        scratch_shapes=[pltpu.VMEM((N_HEADS, tq, LANES), F32), pltpu.VMEM((N_HEADS, tq, LANES), F32),
                        pltpu.VMEM((N_HEADS, tq, LANES), F32)],
        compiler_params=_cparams("parallel", "parallel", "arbitrary"),
        name="flash_attention",
    )(q, k, v)


def _dft_cos_sin(n):
    idx = np.arange(n, dtype=np.int64)
    ang = 2.0 * np.pi * ((idx[:, None] * idx[None, :]) % n).astype(np.float64) / n
    return np.cos(ang), np.sin(ang)


def _fft1_kernel(x_ref, w_ref, yr_ref, yi_ref):
    n1 = x_ref.shape[0]
    y = _dot(w_ref[...], x_ref[...].astype(BF16))
    yr_ref[...] = y[:n1]
    yi_ref[...] = y[n1:]


def _fft2_kernel(kb, scale, yr_ref, yi_ref, tc_ref, ts_ref, w2_ref, wc_ref, o_ref):
    n2 = FFT_N2
    w2 = w2_ref[...]
    wc = wc_ref[...]
    for j in range(kb):
        yr = yr_ref[j]
        yi = yi_ref[j]
        c = tc_ref[:, j:j + 1]
        s = ts_ref[:, j:j + 1]
        z = jnp.concatenate([yr * c + yi * s, yi * c - yr * s], axis=0).astype(BF16)
        gq = _dot(w2, z)
        for g in range(F_GROUPS):
            cols = slice(g * F_GROUP_DIM, (g + 1) * F_GROUP_DIM)
            gg = jnp.concatenate([gq[:n2, cols], gq[n2:, cols]], axis=1).astype(BF16)
            o_ref[:, j * F_WIDTH + g * F_GROUP_DIM:j * F_WIDTH + (g + 1) * F_GROUP_DIM] = _dot(gg, wc) * scale


def _fourier_mix_long(u):
    b, l, fw = u.shape
    n2 = FFT_N2
    n1 = l // n2
    cols = n2 * fw
    c1, s1 = _dft_cos_sin(n1)
    w1 = jnp.asarray(np.concatenate([c1, -s1], axis=0), BF16)
    c2, s2 = _dft_cos_sin(n2)
    w2 = jnp.asarray(np.block([[c2, s2], [-s2, c2]]), BF16)
    cc, sc = _dft_cos_sin(F_GROUP_DIM)
    wc = jnp.asarray(np.concatenate([cc, sc], axis=0), BF16)
    k1 = np.arange(n1, dtype=np.int64)
    nn = np.arange(n2, dtype=np.int64)
    ang = 2.0 * np.pi * ((nn[:, None] * k1[None, :]) % l).astype(np.float64) / l
    kb = SUBLANES
    tw_c = jnp.asarray(np.cos(ang).reshape(n2, n1 // kb, kb).transpose(1, 0, 2), F32)
    tw_s = jnp.asarray(np.sin(ang).reshape(n2, n1 // kb, kb).transpose(1, 0, 2), F32)
    tn = _pick(cols, (4096,))
    x2 = u.reshape(b, n1, cols)
    yr, yi = pl.pallas_call(
        _fft1_kernel,
        out_shape=[jax.ShapeDtypeStruct((b, n1, cols), F32)] * 2,
        grid=(b, cols // tn),
        in_specs=[pl.BlockSpec((None, n1, tn), lambda bi, j: (bi, 0, j)),
                  pl.BlockSpec((2 * n1, n1), lambda bi, j: (0, 0))],
        out_specs=[pl.BlockSpec((None, n1, tn), lambda bi, j: (bi, 0, j))] * 2,
        compiler_params=_cparams("parallel", "parallel"),
        name="fft_stage1",
    )(x2, w1)
    yr = yr.reshape(b, n1, n2, fw)
    yi = yi.reshape(b, n1, n2, fw)
    scale = float(1.0 / np.sqrt(float(l) * F_GROUP_DIM))
    yblk = pl.BlockSpec((None, kb, n2, fw), lambda bi, j: (bi, j, 0, 0))
    tblk = pl.BlockSpec((None, n2, kb), lambda bi, j: (j, 0, 0))
    out = pl.pallas_call(
        functools.partial(_fft2_kernel, kb, scale),
        out_shape=jax.ShapeDtypeStruct((b, n2, n1 * fw), F32),
        grid=(b, n1 // kb),
        in_specs=[yblk, yblk, tblk, tblk,
                  pl.BlockSpec((2 * n2, 2 * n2), lambda bi, j: (0, 0)),
                  pl.BlockSpec((2 * F_GROUP_DIM, F_GROUP_DIM), lambda bi, j: (0, 0))],
        out_specs=pl.BlockSpec((None, n2, kb * fw), lambda bi, j: (bi, 0, j)),
        compiler_params=_cparams("parallel", "parallel"),
        name="fft_stage2",
    )(yr, yi, tw_c, tw_s, w2, wc)
    return out.reshape(b, l, fw)


def _fftc_kernel(scale, u_ref, wl_ref, wc_ref, o_ref):
    fw = u_ref.shape[1]
    ab = _dot(u_ref[...].astype(BF16), wc_ref[...])
    st = jnp.concatenate([ab[:, :fw], ab[:, fw:]], axis=0).astype(BF16)
    o_ref[...] = _dot(wl_ref[...], st) * scale


def _fourier_mix_short(u):
    b, l, fw = u.shape
    cl, sl = _dft_cos_sin(l)
    wl = jnp.asarray(np.concatenate([cl, -sl], axis=1), BF16)
    cc, sc = _dft_cos_sin(F_GROUP_DIM)
    eye = np.eye(F_GROUPS)
    wc = jnp.asarray(np.concatenate([np.kron(eye, cc), np.kron(eye, sc)], axis=1), BF16)
    scale = float(1.0 / np.sqrt(float(l) * F_GROUP_DIM))
    return pl.pallas_call(
        functools.partial(_fftc_kernel, scale),
        out_shape=jax.ShapeDtypeStruct((b, l, fw), F32),
        grid=(b,),
        in_specs=[pl.BlockSpec((None, l, fw), lambda bi: (bi, 0, 0)),
                  pl.BlockSpec((l, 2 * l), lambda bi: (0, 0)),
                  pl.BlockSpec((fw, 2 * fw), lambda bi: (0, 0))],
        out_specs=pl.BlockSpec((None, l, fw), lambda bi: (bi, 0, 0)),
        compiler_params=_cparams("parallel"),
        name="fft_direct",
    )(u, wl, wc)


def _mixout_kernel(f_ref, a_ref, wf_ref, wa_ref, x_ref, gate_ref, nw_ref, o_ref):
    y = _dot(f_ref[...].astype(BF16), wf_ref[...]) + _dot(a_ref[...].astype(BF16), wa_ref[...])
    o_ref[...] = x_ref[...] + gate_ref[...] * _rms(y, nw_ref[...])


def _mixer_out(f, a, wf, wa, x, gate, nw):
    b, s, d = x.shape
    tm = _pick(s, (512, 256, 128))
    row = lambda w: pl.BlockSpec((None, tm, w), lambda bi, i: (bi, i, 0))
    const = lambda shp: pl.BlockSpec(shp, lambda bi, i: (0, 0))
    return pl.pallas_call(
        _mixout_kernel,
        out_shape=jax.ShapeDtypeStruct((b, s, d), F32),
        grid=(b, s // tm),
        in_specs=[row(f.shape[2]), row(a.shape[2]), const(wf.shape), const(wa.shape), row(d),
                  pl.BlockSpec((None, 1, d), lambda bi, i: (bi, 0, 0)), const((1, d))],
        out_specs=row(d),
        compiler_params=_cparams("parallel", "parallel"),
        name="mixer_out_residual",
    )(f, a, wf, wa, x, gate, nw)


def _ffn_kernel(n_e, n_f, use_comb, *refs):
    if use_comb:
        (x_ref, nw1_ref, sh_ref, sc_ref, wg_ref, wu_ref, wd_ref, gate_ref, nw2_ref, comb_ref,
         o_ref, h_scr, acc_scr) = refs
    else:
        (x_ref, nw1_ref, sh_ref, sc_ref, wg_ref, wu_ref, wd_ref, gate_ref, nw2_ref,
         o_ref, h_scr, acc_scr) = refs
    e = pl.program_id(2)
    f = pl.program_id(3)

    @pl.when((e == 0) & (f == 0))
    def _():
        h_scr[...] = _norm_mod(x_ref[...], nw1_ref[...], sh_ref[...], sc_ref[...]).astype(BF16)
        acc_scr[...] = jnp.zeros(acc_scr.shape, F32)

    h = h_scr[...]
    a = _silu(_dot(h, wg_ref[...])) * _dot(h, wu_ref[...])
    if use_comb:
        comb = comb_ref[...]
        lane = lax.broadcasted_iota(jnp.int32, comb.shape, 1)
        a = a * jnp.sum(jnp.where(lane == e, comb, 0.0), axis=-1, keepdims=True)
    acc_scr[...] += _dot(a.astype(BF16), wd_ref[...])

    @pl.when((e == n_e - 1) & (f == n_f - 1))
    def _():
        o_ref[...] = x_ref[...] + gate_ref[...] * _rms(acc_scr[...], nw2_ref[...])


def _ffn(x, nw1, shift, scale, w_gu, w_down, gate, nw2, comb=None):
    b, s, d = x.shape
    n_e, _, f2 = w_gu.shape
    ff = f2 // 2
    tm = _pick(s, (1024, 512, 256))
    tf = _pick(ff, (512, 256))
    n_f = ff // tf
    vec = pl.BlockSpec((None, 1, d), lambda bi, i, e, f: (bi, 0, 0))
    const = pl.BlockSpec((1, d), lambda bi, i, e, f: (0, 0))
    row = pl.BlockSpec((None, tm, d), lambda bi, i, e, f: (bi, i, 0))
    in_specs = [row, const, vec, vec,
                pl.BlockSpec((None, d, tf), lambda bi, i, e, f: (e, 0, f)),
                pl.BlockSpec((None, d, tf), lambda bi, i, e, f: (e, 0, f + n_f)),
                pl.BlockSpec((None, tf, d), lambda bi, i, e, f: (e, f, 0)),
                vec, const]
    args = [x, nw1, shift, scale, w_gu, w_gu, w_down, gate, nw2]
    if comb is not None:
        in_specs.append(pl.BlockSpec((None, tm, LANES), lambda bi, i, e, f: (bi, i, 0)))
        args.append(comb)
    return pl.pallas_call(
        functools.partial(_ffn_kernel, n_e, n_f, comb is not None),
        out_shape=jax.ShapeDtypeStruct((b, s, d), F32),
        grid=(b, s // tm, n_e, n_f),
        in_specs=in_specs,
        out_specs=row,
        scratch_shapes=[pltpu.VMEM((tm, d), BF16), pltpu.VMEM((tm, d), F32)],
        compiler_params=_cparams("parallel", "parallel", "arbitrary", "arbitrary"),
        name="swiglu_residual",
    )(*args)


def _router_kernel(x_ref, nw_ref, sh_ref, sc_ref, rw_ref, comb_ref):
    h = _norm_mod(x_ref[...], nw_ref[...], sh_ref[...], sc_ref[...])
    h_hi, h_lo = _split_bf16(h, 2)
    w_hi, w_lo = _split_bf16(rw_ref[...], 2)
    logits = _dot(h_hi, w_hi) + _dot(h_hi, w_lo) + _dot(h_lo, w_hi)
    lane = lax.broadcasted_iota(jnp.int32, logits.shape, 1)
    neg = -jnp.inf
    logits = jnp.where(lane < N_EXPERTS, logits, neg)
    m1 = jnp.max(logits, axis=-1, keepdims=True)
    i1 = jnp.min(jnp.where(logits == m1, lane, LANES), axis=-1, keepdims=True)
    rest = jnp.where(lane == i1, neg, logits)
    m2 = jnp.max(rest, axis=-1, keepdims=True)
    i2 = jnp.min(jnp.where(rest == m2, lane, LANES), axis=-1, keepdims=True)
    e2 = jnp.exp(m2 - m1)
    den = 1.0 + e2
    comb_ref[...] = jnp.where(lane == i1, 1.0 / den, 0.0) + jnp.where(lane == i2, e2 / den, 0.0)


def _router(x, nw, shift, scale, rw_pad):
    b, s, d = x.shape
    tm = _pick(s, (512, 256, 128))
    vec = pl.BlockSpec((None, 1, d), lambda bi, i: (bi, 0, 0))
    return pl.pallas_call(
        _router_kernel,
        out_shape=jax.ShapeDtypeStruct((b, s, LANES), F32),
        grid=(b, s // tm),
        in_specs=[pl.BlockSpec((None, tm, d), lambda bi, i: (bi, i, 0)),
                  pl.BlockSpec((1, d), lambda bi, i: (0, 0)), vec, vec,
                  pl.BlockSpec((d, LANES), lambda bi, i: (0, 0))],
        out_specs=pl.BlockSpec((None, tm, LANES), lambda bi, i: (bi, i, 0)),
        compiler_params=_cparams("parallel", "parallel"),
        name="router_top2",
    )(x, nw, shift, scale, rw_pad)


def _conv_kernel(nt, prev_ref, cur_ref, next_ref, w_ref, b_ref, xs_ref, bm_ref, cm_ref, buf):
    i = pl.program_id(1)
    tm = cur_ref.shape[0]
    halo = SUBLANES
    buf[halo:halo + tm, :] = cur_ref[...]
    buf[0:halo, :] = jnp.where(i > 0, prev_ref[...], 0.0)
    buf[halo + tm:2 * halo + tm, :] = jnp.where(i < nt - 1, next_ref[...], 0.0)
    acc = None
    for j in range(D_CONV):
        start = halo - CONV_PAD + j
        t = w_ref[j:j + 1, :] * buf[start:start + tm, :]
        acc = t if acc is None else acc + t
    y = _silu(acc + b_ref[...])
    d_inner = xs_ref.shape[1]
    gn = bm_ref.shape[1]
    xs_ref[...] = y[:, :d_inner]
    bm_ref[...] = y[:, d_inner:d_inner + gn]
    cm_ref[...] = y[:, d_inner + gn:]


def _conv_silu(xbc, conv_w, conv_b, d_inner):
    b, s, c = xbc.shape
    gn = (c - d_inner) // 2
    tm = _pick(s, (512, 256, 128))
    nt = s // tm
    hb = tm // SUBLANES
    last = s // SUBLANES - 1
    row = lambda w: pl.BlockSpec((None, tm, w), lambda bi, i: (bi, i, 0))
    return pl.pallas_call(
        functools.partial(_conv_kernel, nt),
        out_shape=[jax.ShapeDtypeStruct((b, s, d_inner), F32),
                   jax.ShapeDtypeStruct((b, s, gn), F32), jax.ShapeDtypeStruct((b, s, gn), F32)],
        grid=(b, nt),
        in_specs=[pl.BlockSpec((None, SUBLANES, c), lambda bi, i: (bi, jnp.maximum(i * hb - 1, 0), 0)),
                  row(c),
                  pl.BlockSpec((None, SUBLANES, c), lambda bi, i: (bi, jnp.minimum((i + 1) * hb, last), 0)),
                  pl.BlockSpec((D_CONV, c), lambda bi, i: (0, 0)),
                  pl.BlockSpec((1, c), lambda bi, i: (0, 0))],
        out_specs=[row(d_inner), row(gn), row(gn)],
        scratch_shapes=[pltpu.VMEM((tm + 2 * SUBLANES, c), F32)],
        compiler_params=_cparams("parallel", "parallel"),
        name="dwconv_silu",
    )(xbc, xbc, xbc, conv_w, conv_b)


def _softplus(x):
    return jnp.maximum(x, 0.0) + jnp.log(1.0 + jnp.exp(-jnp.abs(x)))


def _ssd_kernel(direction, nc, xs_ref, bm_ref, cm_ref, dt_ref, dtb_ref, a_ref, h0_ref, e_ref,
                y_ref, hT_ref, st_scr):
    c = pl.program_id(1)

    @pl.when(c == 0)
    def _():
        st_scr[...] = h0_ref[...]

    q = CHUNK
    fwd = direction == 0
    dt = _softplus(dt_ref[...] + dtb_ref[...])
    d_a = dt * a_ref[...]
    row = lax.broadcasted_iota(jnp.int32, (q, q), 0)
    col = lax.broadcasted_iota(jnp.int32, (q, q), 1)
    causal = (row >= col) if fwd else (row <= col)
    tri = jnp.where(causal, 1.0, 0.0).astype(BF16)
    a_cum = _dot_split_rhs(tri, d_a, 3)
    a_cum_t = a_cum.T
    expand = e_ref[...]
    dt_x = _dot_split_lhs(dt, expand, 2)
    a_x = _dot_split_lhs(a_cum, expand, 3)
    last = q - 1 if fwd else 0
    a_last = a_x[last:last + 1, :]
    xdt = xs_ref[...] * dt_x
    xdt_b = xdt.astype(BF16)
    w_state = (jnp.exp(a_last - a_x) * xdt).astype(BF16)
    exp_a = jnp.exp(a_x)
    exp_last = jnp.exp(a_last)
    low = lax.broadcasted_iota(jnp.int32, (q, LANES), 1) < SSM_HEAD_DIM
    gw = HEADS_PER_GROUP * SSM_HEAD_DIM
    for g in range(N_GROUPS):
        gs = slice(g * gw, (g + 1) * gw)
        b_g = bm_ref[:, g * D_STATE:(g + 1) * D_STATE]
        c_b = cm_ref[:, g * D_STATE:(g + 1) * D_STATE].astype(BF16)
        cb = lax.dot_general(c_b, b_g.astype(BF16), (((1,), (1,)), ((), ())), preferred_element_type=F32)
        st_g = st_scr[:, gs]
        y_off = _dot(c_b, st_g.astype(BF16)) * exp_a[:, gs]
        s_new = _dot(b_g.T.astype(BF16), w_state[:, gs])
        for pr in range(HEADS_PER_GROUP // 2):
            ys = []
            for j in range(2):
                hl = direction * N_SSM_HEADS + g * HEADS_PER_GROUP + 2 * pr + j
                seg = a_cum[:, hl:hl + 1] - a_cum_t[hl:hl + 1, :]
                dec = jnp.exp(jnp.where(causal, seg, -jnp.inf))
                ys.append(_dot((cb * dec).astype(BF16), xdt_b[:, g * gw + pr * LANES:g * gw + (pr + 1) * LANES]))
            y_ref[:, g * gw + pr * LANES:g * gw + (pr + 1) * LANES] = (
                jnp.where(low, ys[0], ys[1]) + y_off[:, pr * LANES:(pr + 1) * LANES])
        st_scr[:, gs] = exp_last[:, gs] * st_g + s_new

    @pl.when(c == nc - 1)
    def _():
        hT_ref[...] = st_scr[...]


def _ssd_scan(direction, xs, bm, cm, dt_raw, dt_bias, a_neg, h0):
    b, s, d_inner = xs.shape
    gn = bm.shape[2]
    nc = s // CHUNK
    heads = np.arange(d_inner) // SSM_HEAD_DIM + direction * N_SSM_HEADS
    expand = jnp.asarray(np.arange(LANES)[:, None] == heads[None, :], BF16)
    if direction == 0:
        cidx = lambda bi, c: (bi, c, 0)
    else:
        cidx = lambda bi, c: (bi, nc - 1 - c, 0)
    const = lambda bi, c: (0, 0)
    st_spec = pl.BlockSpec((None, D_STATE, d_inner), lambda bi, c: (bi, 0, 0))
    return pl.pallas_call(
        functools.partial(_ssd_kernel, direction, nc),
        out_shape=[jax.ShapeDtypeStruct((b, s, d_inner), F32),
                   jax.ShapeDtypeStruct((b, D_STATE, d_inner), F32)],
        grid=(b, nc),
        in_specs=[pl.BlockSpec((None, CHUNK, d_inner), cidx),
                  pl.BlockSpec((None, CHUNK, gn), cidx), pl.BlockSpec((None, CHUNK, gn), cidx),
                  pl.BlockSpec((None, CHUNK, LANES), cidx),
                  pl.BlockSpec((1, LANES), const), pl.BlockSpec((1, LANES), const),
                  st_spec, pl.BlockSpec((LANES, d_inner), const)],
        out_specs=[pl.BlockSpec((None, CHUNK, d_inner), cidx), st_spec],
        scratch_shapes=[pltpu.VMEM((D_STATE, d_inner), F32)],
        compiler_params=_cparams("parallel", "arbitrary"),
        name="ssd_scan",
    )(xs, bm, cm, dt_raw, dt_bias, a_neg, h0, expand)


def _ssdout_kernel(yf_ref, yb_ref, xs_ref, z_ref, dsk_ref, gnw_ref, w_ref, x_ref, gate_ref, nw_ref, o_ref):
    y = yf_ref[...] + yb_ref[...] + dsk_ref[...] * xs_ref[...]
    gy = y * _silu(z_ref[...])
    gnw = gnw_ref[...]
    gw = gy.shape[1] // N_GROUPS
    parts = []
    for g in range(N_GROUPS):
        parts.append(_rms(gy[:, g * gw:(g + 1) * gw], gnw[:, g * gw:(g + 1) * gw]).astype(BF16))
    out = _dot(jnp.concatenate(parts, axis=1), w_ref[...])
    o_ref[...] = x_ref[...] + gate_ref[...] * _rms(out, nw_ref[...])


def _ssd_out(yf, yb, xs, z, d_skip, gnw, w_out, x, gate, nw):
    b, s, d = x.shape
    d_inner = xs.shape[2]
    tm = _pick(s, (256, 128))
    row = lambda w: pl.BlockSpec((None, tm, w), lambda bi, i: (bi, i, 0))
    const = lambda shp: pl.BlockSpec(shp, lambda bi, i: (0, 0))
    return pl.pallas_call(
        _ssdout_kernel,
        out_shape=jax.ShapeDtypeStruct((b, s, d), F32),
        grid=(b, s // tm),
        in_specs=[row(d_inner), row(d_inner), row(d_inner), row(d_inner),
                  const((1, d_inner)), const((1, d_inner)), const(w_out.shape), row(d),
                  pl.BlockSpec((None, 1, d), lambda bi, i: (bi, 0, 0)), const((1, d))],
        out_specs=row(d),
        compiler_params=_cparams("parallel", "parallel"),
        name="ssd_gate_out_residual",
    )(yf, yb, xs, z, d_skip, gnw, w_out, x, gate, nw)


def _rope_tables(seq):
    rows = seq // GRID_W
    row = jnp.repeat(jnp.arange(rows, dtype=F32), GRID_W)
    col = jnp.tile(jnp.arange(GRID_W, dtype=F32), rows)
    freqs = ROPE_THETA ** (-jnp.arange(0, AXIS_ROT_DIM, 2, dtype=F32) / AXIS_ROT_DIM)
    ang = jnp.concatenate([row[:, None] * freqs, col[:, None] * freqs], axis=-1)
    cos = jnp.repeat(jnp.cos(ang), 2, axis=-1)
    sin = jnp.repeat(jnp.sin(ang), 2, axis=-1) * jnp.tile(jnp.asarray([-1.0, 1.0], F32), HEAD_DIM // 2)
    return jnp.tile(cos, (1, LANES // HEAD_DIM)), jnp.tile(sin, (1, LANES // HEAD_DIM))


def _mod_vectors(mods_i, batch):
    d = mods_i.shape[1] // 6
    mx = [mods_i[:batch, k * d:(k + 1) * d].reshape(batch, 1, d) for k in range(6)]
    mc = [jnp.broadcast_to(mods_i[batch, k * d:(k + 1) * d].reshape(1, 1, d), (batch, 1, d)) for k in range(6)]
    return mx, mc


def _even_mixer(x, ctx, nw0, nw1, mx, mc, w_in, qnw, knw, w_out, rope, need_ctx):
    w_u = w_in[:, :F_WIDTH].astype(BF16)
    w_qkv = w_in[:, F_WIDTH:].astype(BF16)
    qw = jnp.tile(qnw.reshape(1, HEAD_DIM), (1, LANES // HEAD_DIM))
    kw = jnp.tile(knw.reshape(1, HEAD_DIM), (1, LANES // HEAD_DIM))
    w_f = w_out[:F_WIDTH].astype(BF16)
    d = w_out.shape[1]
    w_a = (w_out[F_WIDTH:].reshape(N_KV_HEADS, GQ, HEAD_DIM, d).transpose(1, 0, 2, 3)
           .reshape(Q_WIDTH, d).astype(BF16))
    lc = ctx.shape[1]
    ones = jnp.ones((lc, LANES), F32)
    zeros = jnp.zeros((lc, LANES), F32)

    u_c, qkv_c = _norm_mod_linear(ctx, nw0, mc[0], mc[1], [w_u, w_qkv], [F32, F32])
    q_c, k_c, v_c = _qk_prepare(qkv_c, ones, zeros, qw, kw)
    u_x, qkv_x = _norm_mod_linear(x, nw0, mx[0], mx[1], [w_u, w_qkv], [F32, F32])
    q_x, k_x, v_x = _qk_prepare(qkv_x, rope[0], rope[1], qw, kw)
    k_all = jnp.concatenate([k_c, k_x], axis=1)
    v_all = jnp.concatenate([v_c, v_x], axis=1)
    a_x = _attention(q_x, k_all, v_all)
    f_x = _fourier_mix_long(u_x)
    x_new = _mixer_out(f_x, a_x, w_f, w_a, x, mx[2], nw1)
    if not need_ctx:
        return x_new, ctx
    a_c = _attention(q_c, k_c, v_c)
    f_c = _fourier_mix_short(u_c)
    return x_new, _mixer_out(f_c, a_c, w_f, w_a, ctx, mc[2], nw1)


def _ssd_branch(h_in, nw0, shift, scale, w_z, w_xbc, w_dt, conv_w, conv_b, dt_bias, a_neg, h0_f, h0_b):
    z, xbc, dt_raw = _norm_mod_linear(h_in, nw0, shift, scale, [w_z, w_xbc, w_dt], [F32, F32, F32], tm_max=256)
    xs, bm, cm = _conv_silu(xbc, conv_w, conv_b, w_z.shape[1])
    y_f, s_f = _ssd_scan(0, xs, bm, cm, dt_raw, dt_bias, a_neg, h0_f)
    y_b, s_b = _ssd_scan(1, xs, bm, cm, dt_raw, dt_bias, a_neg, h0_b)
    return z, xs, y_f, y_b, s_f, s_b


def _odd_mixer(x, ctx, nw0, nw1, mx, mc, w_in, conv_w, conv_b, dt_bias, a_log, d_skip, gnw, w_out, need_ctx):
    d_inner = w_out.shape[0]
    conv_dim = conv_w.shape[1]
    w_z = w_in[:, :d_inner].astype(BF16)
    w_xbc = w_in[:, d_inner:d_inner + conv_dim].astype(BF16)
    n_dt = w_in.shape[1] - d_inner - conv_dim
    w_dt = jnp.pad(w_in[:, d_inner + conv_dim:], ((0, 0), (0, LANES - n_dt))).astype(BF16)
    dtb = jnp.pad(dt_bias.reshape(1, n_dt), ((0, 0), (0, LANES - n_dt)))
    a_neg = jnp.pad(-jnp.exp(a_log.astype(F32)).reshape(1, n_dt), ((0, 0), (0, LANES - n_dt)))
    cb = conv_b.reshape(1, conv_dim)
    dsk = jnp.repeat(d_skip, SSM_HEAD_DIM).reshape(1, d_inner)
    gw = gnw.reshape(1, d_inner)
    w_o = w_out.astype(BF16)
    b = x.shape[0]
    zeros = jnp.zeros((b, D_STATE, d_inner), F32)

    z_c, xs_c, yf_c, yb_c, s_f, s_b = _ssd_branch(ctx, nw0, mc[0], mc[1], w_z, w_xbc, w_dt, conv_w, cb,
                                                  dtb, a_neg, zeros, zeros)
    z_x, xs_x, yf_x, yb_x, _, _ = _ssd_branch(x, nw0, mx[0], mx[1], w_z, w_xbc, w_dt, conv_w, cb,
                                              dtb, a_neg, s_f, s_b)
    x_new = _ssd_out(yf_x, yb_x, xs_x, z_x, dsk, gw, w_o, x, mx[2], nw1)
    if not need_ctx:
        return x_new, ctx
    return x_new, _ssd_out(yf_c, yb_c, xs_c, z_c, dsk, gw, w_o, ctx, mc[2], nw1)


def kernel(x, c, ctx, c_ctx, ada_w, ada_b, norm_w, mix_in_w, q_norm_w, k_norm_w, mix_out_w, ffn_gu_w, ffn_down_w, ssm_in_w, ssm_conv_w, ssm_conv_b, ssm_dt_bias, ssm_a_log, ssm_d, ssm_norm_w, ssm_out_w, router_w, moe_gu_w, moe_down_w):
    batch, seq, d = x.shape
    depth = ada_w.shape[0]
    rope = _rope_tables(seq)
    cvec = jnp.zeros((SUBLANES, d), F32).at[:batch].set(c).at[batch].set(c_ctx)
    mods = _modulation(cvec, ada_w, ada_b)
    for i in range(depth):
        last = i == depth - 1
        j = i // 2
        mx, mc = _mod_vectors(mods[i], batch)
        nw = [norm_w[i, k].reshape(1, d) for k in range(4)]
        if i % 2 == 0:
            x, ctx_mid = _even_mixer(x, ctx, nw[0], nw[1], mx, mc, mix_in_w[j], q_norm_w[j], k_norm_w[j],
                                     mix_out_w[j], rope, not last)
            w_gu = ffn_gu_w[j].astype(BF16)[None]
            w_dn = ffn_down_w[j].astype(BF16)[None]
            x = _ffn(x, nw[2], mx[3], mx[4], w_gu, w_dn, mx[5], nw[3])
            if not last:
                ctx = _ffn(ctx_mid, nw[2], mc[3], mc[4], w_gu, w_dn, mc[5], nw[3])
        else:
            x, ctx_mid = _odd_mixer(x, ctx, nw[0], nw[1], mx, mc, ssm_in_w[j], ssm_conv_w[j], ssm_conv_b[j],
                                    ssm_dt_bias[j], ssm_a_log[j], ssm_d[j], ssm_norm_w[j], ssm_out_w[j],
                                    not last)
            rw = jnp.pad(router_w[j], ((0, 0), (0, LANES - N_EXPERTS)))
            w_gu = moe_gu_w[j].astype(BF16)
            w_dn = moe_down_w[j].astype(BF16)
            comb = _router(x, nw[2], mx[3], mx[4], rw)
            x = _ffn(x, nw[2], mx[3], mx[4], w_gu, w_dn, mx[5], nw[3], comb)
            if not last:
                comb_c = _router(ctx_mid, nw[2], mc[3], mc[4], rw)
                ctx = _ffn(ctx_mid, nw[2], mc[3], mc[4], w_gu, w_dn, mc[5], nw[3], comb_c)
    return x
```

```python
import functools

import numpy as np
import jax
import jax.numpy as jnp
from jax import lax
from jax.experimental import pallas as pl
from jax.experimental.pallas import tpu as pltpu

F32 = jnp.float32
BF16 = jnp.bfloat16

EPS = 1e-6
GRID_W = 64
HEAD_DIM = 64
N_HEADS = 8
N_KV_HEADS = 2
GQ = N_HEADS // N_KV_HEADS
F_GROUPS = 4
F_GROUP_DIM = 128
F_WIDTH = F_GROUPS * F_GROUP_DIM
Q_WIDTH = N_HEADS * HEAD_DIM
KV_WIDTH = N_KV_HEADS * HEAD_DIM
ROPE_THETA = 10000.0
AXIS_ROT_DIM = HEAD_DIM // 2
ATTN_SCALE = HEAD_DIM ** -0.5
LOG2_E = 1.4426950408889634
SSM_HEAD_DIM = 64
N_SSM_HEADS = 32
N_GROUPS = 4
HEADS_PER_GROUP = N_SSM_HEADS // N_GROUPS
D_STATE = 128
D_CONV = 5
CONV_PAD = D_CONV // 2
CHUNK = 128
N_EXPERTS = 8
EXPERT_TILE = 1024
LANES = 128
SUBLANES = 8
FFT_N2 = 128
VMEM_LIMIT = 56 * 1024 * 1024


def _cparams(*sem):
    return pltpu.CompilerParams(dimension_semantics=sem, vmem_limit_bytes=VMEM_LIMIT)


def _pick(n, candidates):
    for c in candidates:
        if n % c == 0:
            return c
    return n


def _sigmoid(x):
    return 1.0 / (1.0 + jnp.exp(-x))


def _silu(x):
    return x * _sigmoid(x)


def _rms(x, w):
    return x * lax.rsqrt(jnp.mean(x * x, axis=-1, keepdims=True) + EPS) * w


def _norm_mod(x, nw, shift, scale):
    return _rms(x, nw) * (1.0 + scale) + shift


def _dot(a, b):
    return jnp.dot(a, b, preferred_element_type=F32)


def _split_bf16(x, parts):
    out = []
    r = x
    for _ in range(parts):
        p = r.astype(BF16)
        out.append(p)
        r = r - p.astype(F32)
    return out


def _dot_split_lhs(x, m, parts):
    acc = None
    for p in _split_bf16(x, parts):
        t = _dot(p, m)
        acc = t if acc is None else acc + t
    return acc


def _dot_split_rhs(m, x, parts):
    acc = None
    for p in _split_bf16(x, parts):
        t = _dot(m, p)
        acc = t if acc is None else acc + t
    return acc


def _mod_kernel(c_ref, w_ref, b_ref, o_ref):
    s = _silu(c_ref[...])
    o_ref[...] = jnp.dot(s, w_ref[...], preferred_element_type=F32,
                         precision=lax.Precision.HIGHEST) + b_ref[...]


def _modulation(cvec, ada_w, ada_b):
    depth, d, n = ada_w.shape
    tn = _pick(n, (1536, 1024, 512))
    rows = cvec.shape[0]
    return pl.pallas_call(
        _mod_kernel,
        out_shape=jax.ShapeDtypeStruct((depth, rows, n), F32),
        grid=(depth, n // tn),
        in_specs=[pl.BlockSpec((rows, d), lambda i, j: (0, 0)),
                  pl.BlockSpec((None, d, tn), lambda i, j: (i, 0, j)),
                  pl.BlockSpec((None, 1, tn), lambda i, j: (i, 0, j))],
        out_specs=pl.BlockSpec((None, rows, tn), lambda i, j: (i, 0, j)),
        compiler_params=_cparams("parallel", "parallel"),
        name="adaln_modulation",
    )(cvec, ada_w, ada_b.reshape(depth, 1, n))


def _nml_kernel(n_w, x_ref, nw_ref, sh_ref, sc_ref, *refs):
    h = _norm_mod(x_ref[...], nw_ref[...], sh_ref[...], sc_ref[...]).astype(BF16)
    for w_ref, o_ref in zip(refs[:n_w], refs[n_w:]):
        o_ref[...] = _dot(h, w_ref[...]).astype(o_ref.dtype)


def _norm_mod_linear(x, nw, shift, scale, weights, out_dtypes, tm_max=512):
    b, s, d = x.shape
    tm = _pick(s, (tm_max, 256, 128))
    n_w = len(weights)
    vec = pl.BlockSpec((None, 1, d), lambda bi, i: (bi, 0, 0))
    in_specs = [pl.BlockSpec((None, tm, d), lambda bi, i: (bi, i, 0)),
                pl.BlockSpec((1, d), lambda bi, i: (0, 0)), vec, vec]
    in_specs += [pl.BlockSpec(w.shape, lambda bi, i: (0, 0)) for w in weights]
    out_shape = [jax.ShapeDtypeStruct((b, s, w.shape[1]), dt) for w, dt in zip(weights, out_dtypes)]
    out_specs = [pl.BlockSpec((None, tm, w.shape[1]), lambda bi, i: (bi, i, 0)) for w in weights]
    return pl.pallas_call(
        functools.partial(_nml_kernel, n_w),
        out_shape=out_shape,
        grid=(b, s // tm),
        in_specs=in_specs,
        out_specs=out_specs,
        compiler_params=_cparams("parallel", "parallel"),
        name="norm_mod_linear",
    )(x, nw, shift, scale, *weights)


def _qkprep_kernel(qkv_ref, cos_ref, sin_ref, qw_ref, kw_ref, bd_ref, q_ref, k_ref, v_ref):
    tm = qkv_ref.shape[0]
    lane = lax.broadcasted_iota(jnp.int32, (tm, LANES), 1)
    even = (lane % 2) == 0
    low = lane < HEAD_DIM
    cos = cos_ref[...]
    sin = sin_ref[...]
    bd = bd_ref[...]

    def prep(xb, w):
        ms = _dot_split_lhs(xb * xb, bd, 2)
        y = xb * lax.rsqrt(ms + EPS) * w
        swapped = jnp.where(even, pltpu.roll(y, LANES - 1, 1), pltpu.roll(y, 1, 1))
        return y * cos + swapped * sin

    qw = qw_ref[...]
    for cblk in range(Q_WIDTH // LANES):
        r = prep(qkv_ref[:, cblk * LANES:(cblk + 1) * LANES], qw) * (ATTN_SCALE * LOG2_E)
        r_sw = pltpu.roll(r, HEAD_DIM, 1)
        kvh = (2 * cblk) // GQ
        keep = low if kvh == 0 else jnp.logical_not(low)
        for j in range(2):
            src = r if j == kvh else r_sw
            q_ref[2 * cblk + j] = jnp.where(keep, src, 0.0).astype(q_ref.dtype)
    k_ref[...] = prep(qkv_ref[:, Q_WIDTH:Q_WIDTH + KV_WIDTH], kw_ref[...]).astype(k_ref.dtype)
    v_ref[...] = qkv_ref[:, Q_WIDTH + KV_WIDTH:Q_WIDTH + 2 * KV_WIDTH].astype(v_ref.dtype)


def _qk_prepare(qkv, cos_t, sin_t, qw, kw):
    b, s, _ = qkv.shape
    tm = _pick(s, (512, 256, 128))
    bd = np.kron(np.eye(2, dtype=np.float32), np.full((HEAD_DIM, HEAD_DIM), 1.0 / HEAD_DIM, np.float32))
    row = lambda bi, i: (i, 0)
    const = lambda bi, i: (0, 0)
    return pl.pallas_call(
        _qkprep_kernel,
        out_shape=[jax.ShapeDtypeStruct((b, N_HEADS, s, LANES), BF16),
                   jax.ShapeDtypeStruct((b, s, KV_WIDTH), BF16),
                   jax.ShapeDtypeStruct((b, s, KV_WIDTH), BF16)],
        grid=(b, s // tm),
        in_specs=[pl.BlockSpec((None, tm, qkv.shape[2]), lambda bi, i: (bi, i, 0)),
                  pl.BlockSpec((tm, LANES), row), pl.BlockSpec((tm, LANES), row),
                  pl.BlockSpec((1, LANES), const), pl.BlockSpec((1, LANES), const),
                  pl.BlockSpec((LANES, LANES), const)],
        out_specs=[pl.BlockSpec((None, N_HEADS, tm, LANES), lambda bi, i: (bi, 0, i, 0)),
                   pl.BlockSpec((None, tm, KV_WIDTH), lambda bi, i: (bi, i, 0)),
                   pl.BlockSpec((None, tm, KV_WIDTH), lambda bi, i: (bi, i, 0))],
        compiler_params=_cparams("parallel", "parallel"),
        name="qk_norm_rope",
    )(qkv, cos_t, sin_t, qw, kw, jnp.asarray(bd, BF16))


def _attn_kernel(nk, q_ref, k_ref, v_ref, o_ref, m_scr, l_scr, acc_scr):
    kv = pl.program_id(2)

    @pl.when(kv == 0)
    def _():
        m_scr[...] = jnp.full(m_scr.shape, -jnp.inf, F32)
        l_scr[...] = jnp.zeros(l_scr.shape, F32)
        acc_scr[...] = jnp.zeros(acc_scr.shape, F32)

    k = k_ref[...]
    v = v_ref[...]
    n_t = k.shape[0] // LANES

    def head(hd, carry):
        s = lax.dot_general(q_ref[hd], k, (((1,), (1,)), ((), ())), preferred_element_type=F32)
        tiles = [s[:, t * LANES:(t + 1) * LANES] for t in range(n_t)]
        smax = tiles[0]
        for t in range(1, n_t):
            smax = jnp.maximum(smax, tiles[t])
        m_prev = m_scr[hd]
        m_new = jnp.maximum(m_prev, jnp.max(smax, axis=-1, keepdims=True))
        alpha = jnp.exp2(m_prev - m_new)
        ps = []
        lsum = None
        for t in range(n_t):
            p_t = jnp.exp2(tiles[t] - m_new)
            lsum = p_t if lsum is None else lsum + p_t
            ps.append(p_t.astype(BF16))
        l_scr[hd] = alpha * l_scr[hd] + lsum
        acc_scr[hd] = alpha * acc_scr[hd] + _dot(jnp.concatenate(ps, axis=1), v)
        m_scr[hd] = m_new
        return carry

    lax.fori_loop(0, N_HEADS, head, 0, unroll=4)

    @pl.when(kv == nk - 1)
    def _():
        tq = o_ref.shape[0]
        low = lax.broadcasted_iota(jnp.int32, (tq, LANES), 1) < HEAD_DIM
        for g in range(GQ):
            o0 = acc_scr[g] / jnp.sum(l_scr[g], axis=-1, keepdims=True)
            o1 = acc_scr[GQ + g] / jnp.sum(l_scr[GQ + g], axis=-1, keepdims=True)
            o_ref[:, g * LANES:(g + 1) * LANES] = jnp.where(low, o0, o1).astype(o_ref.dtype)


def _attention(q, k, v):
    b, _, s, _ = q.shape
    skv = k.shape[1]
    tq = _pick(s, (512, 256, 128))
    tk = _pick(skv, (1280, 640, 256, 128))
    nk = skv // tk
    return pl.pallas_call(
        functools.partial(_attn_kernel, nk),
        out_shape=jax.ShapeDtypeStruct((b, s, Q_WIDTH), BF16),
        grid=(b, s // tq, nk),
        in_specs=[pl.BlockSpec((None, N_HEADS, tq, LANES), lambda bi, i, j: (bi, 0, i, 0)),
                  pl.BlockSpec((None, tk, KV_WIDTH), lambda bi, i, j: (bi, j, 0)),
                  pl.BlockSpec((None, tk, KV_WIDTH), lambda bi, i, j: (bi, j, 0))],
        out_specs=pl.BlockSpec((None, tq, Q_WIDTH), lambda bi, i, j: (bi, i, 0)),
        scratch_shapes=[pltpu.VMEM((N_HEADS, tq, LANES), F32), pltpu.VMEM((N_HEADS, tq, LANES), F32),
                        pltpu.VMEM((N_HEADS, tq, LANES), F32)],
        compiler_params=_cparams("parallel", "parallel", "arbitrary"),
        name="flash_attention",
    )(q, k, v)


def _dft_cos_sin(n):
    idx = np.arange(n, dtype=np.int64)
    ang = 2.0 * np.pi * ((idx[:, None] * idx[None, :]) % n).astype(np.float64) / n
    return np.cos(ang), np.sin(ang)


def _fft1_kernel(x_ref, w_ref, yr_ref, yi_ref):
    n1 = x_ref.shape[0]
    y = _dot(w_ref[...], x_ref[...].astype(BF16))
    yr_ref[...] = y[:n1]
    yi_ref[...] = y[n1:]


def _fft2_kernel(kb, scale, yr_ref, yi_ref, tc_ref, ts_ref, w2_ref, wc_ref, o_ref):
    n2 = FFT_N2
    w2 = w2_ref[...]
    wc = wc_ref[...]
    for j in range(kb):
        yr = yr_ref[j]
        yi = yi_ref[j]
        c = tc_ref[:, j:j + 1]
        s = ts_ref[:, j:j + 1]
        z = jnp.concatenate([yr * c + yi * s, yi * c - yr * s], axis=0).astype(BF16)
        gq = _dot(w2, z)
        for g in range(F_GROUPS):
            cols = slice(g * F_GROUP_DIM, (g + 1) * F_GROUP_DIM)
            gg = jnp.concatenate([gq[:n2, cols], gq[n2:, cols]], axis=1).astype(BF16)
            o_ref[:, j * F_WIDTH + g * F_GROUP_DIM:j * F_WIDTH + (g + 1) * F_GROUP_DIM] = _dot(gg, wc) * scale


def _fourier_mix_long(u):
    b, l, fw = u.shape
    n2 = FFT_N2
    n1 = l // n2
    cols = n2 * fw
    c1, s1 = _dft_cos_sin(n1)
    w1 = jnp.asarray(np.concatenate([c1, -s1], axis=0), BF16)
    c2, s2 = _dft_cos_sin(n2)
    w2 = jnp.asarray(np.block([[c2, s2], [-s2, c2]]), BF16)
    cc, sc = _dft_cos_sin(F_GROUP_DIM)
    wc = jnp.asarray(np.concatenate([cc, sc], axis=0), BF16)
    k1 = np.arange(n1, dtype=np.int64)
    nn = np.arange(n2, dtype=np.int64)
    ang = 2.0 * np.pi * ((nn[:, None] * k1[None, :]) % l).astype(np.float64) / l
    kb = SUBLANES
    tw_c = jnp.asarray(np.cos(ang).reshape(n2, n1 // kb, kb).transpose(1, 0, 2), F32)
    tw_s = jnp.asarray(np.sin(ang).reshape(n2, n1 // kb, kb).transpose(1, 0, 2), F32)
    tn = _pick(cols, (4096,))
    x2 = u.reshape(b, n1, cols)
    yr, yi = pl.pallas_call(
        _fft1_kernel,
        out_shape=[jax.ShapeDtypeStruct((b, n1, cols), F32)] * 2,
        grid=(b, cols // tn),
        in_specs=[pl.BlockSpec((None, n1, tn), lambda bi, j: (bi, 0, j)),
                  pl.BlockSpec((2 * n1, n1), lambda bi, j: (0, 0))],
        out_specs=[pl.BlockSpec((None, n1, tn), lambda bi, j: (bi, 0, j))] * 2,
        compiler_params=_cparams("parallel", "parallel"),
        name="fft_stage1",
    )(x2, w1)
    yr = yr.reshape(b, n1, n2, fw)
    yi = yi.reshape(b, n1, n2, fw)
    scale = float(1.0 / np.sqrt(float(l) * F_GROUP_DIM))
    yblk = pl.BlockSpec((None, kb, n2, fw), lambda bi, j: (bi, j, 0, 0))
    tblk = pl.BlockSpec((None, n2, kb), lambda bi, j: (j, 0, 0))
    out = pl.pallas_call(
        functools.partial(_fft2_kernel, kb, scale),
        out_shape=jax.ShapeDtypeStruct((b, n2, n1 * fw), F32),
        grid=(b, n1 // kb),
        in_specs=[yblk, yblk, tblk, tblk,
                  pl.BlockSpec((2 * n2, 2 * n2), lambda bi, j: (0, 0)),
                  pl.BlockSpec((2 * F_GROUP_DIM, F_GROUP_DIM), lambda bi, j: (0, 0))],
        out_specs=pl.BlockSpec((None, n2, kb * fw), lambda bi, j: (bi, 0, j)),
        compiler_params=_cparams("parallel", "parallel"),
        name="fft_stage2",
    )(yr, yi, tw_c, tw_s, w2, wc)
    return out.reshape(b, l, fw)


def _fftc_kernel(scale, u_ref, wl_ref, wc_ref, o_ref):
    fw = u_ref.shape[1]
    ab = _dot(u_ref[...].astype(BF16), wc_ref[...])
    st = jnp.concatenate([ab[:, :fw], ab[:, fw:]], axis=0).astype(BF16)
    o_ref[...] = _dot(wl_ref[...], st) * scale


def _fourier_mix_short(u):
    b, l, fw = u.shape
    cl, sl = _dft_cos_sin(l)
    wl = jnp.asarray(np.concatenate([cl, -sl], axis=1), BF16)
    cc, sc = _dft_cos_sin(F_GROUP_DIM)
    eye = np.eye(F_GROUPS)
    wc = jnp.asarray(np.concatenate([np.kron(eye, cc), np.kron(eye, sc)], axis=1), BF16)
    scale = float(1.0 / np.sqrt(float(l) * F_GROUP_DIM))
    return pl.pallas_call(
        functools.partial(_fftc_kernel, scale),
        out_shape=jax.ShapeDtypeStruct((b, l, fw), F32),
        grid=(b,),
        in_specs=[pl.BlockSpec((None, l, fw), lambda bi: (bi, 0, 0)),
                  pl.BlockSpec((l, 2 * l), lambda bi: (0, 0)),
                  pl.BlockSpec((fw, 2 * fw), lambda bi: (0, 0))],
        out_specs=pl.BlockSpec((None, l, fw), lambda bi: (bi, 0, 0)),
        compiler_params=_cparams("parallel"),
        name="fft_direct",
    )(u, wl, wc)


def _mixout_kernel(f_ref, a_ref, wf_ref, wa_ref, x_ref, gate_ref, nw_ref, o_ref):
    y = _dot(f_ref[...].astype(BF16), wf_ref[...]) + _dot(a_ref[...].astype(BF16), wa_ref[...])
    o_ref[...] = x_ref[...] + gate_ref[...] * _rms(y, nw_ref[...])


def _mixer_out(f, a, wf, wa, x, gate, nw):
    b, s, d = x.shape
    tm = _pick(s, (512, 256, 128))
    row = lambda w: pl.BlockSpec((None, tm, w), lambda bi, i: (bi, i, 0))
    const = lambda shp: pl.BlockSpec(shp, lambda bi, i: (0, 0))
    return pl.pallas_call(
        _mixout_kernel,
        out_shape=jax.ShapeDtypeStruct((b, s, d), F32),
        grid=(b, s // tm),
        in_specs=[row(f.shape[2]), row(a.shape[2]), const(wf.shape), const(wa.shape), row(d),
                  pl.BlockSpec((None, 1, d), lambda bi, i: (bi, 0, 0)), const((1, d))],
        out_specs=row(d),
        compiler_params=_cparams("parallel", "parallel"),
        name="mixer_out_residual",
    )(f, a, wf, wa, x, gate, nw)


def _ffn_kernel(n_e, n_f, use_comb, *refs):
    if use_comb:
        (x_ref, nw1_ref, sh_ref, sc_ref, wg_ref, wu_ref, wd_ref, gate_ref, nw2_ref, comb_ref,
         o_ref, h_scr, acc_scr) = refs
    else:
        (x_ref, nw1_ref, sh_ref, sc_ref, wg_ref, wu_ref, wd_ref, gate_ref, nw2_ref,
         o_ref, h_scr, acc_scr) = refs
    e = pl.program_id(2)
    f = pl.program_id(3)

    @pl.when((e == 0) & (f == 0))
    def _():
        h_scr[...] = _norm_mod(x_ref[...], nw1_ref[...], sh_ref[...], sc_ref[...]).astype(BF16)
        acc_scr[...] = jnp.zeros(acc_scr.shape, F32)

    h = h_scr[...]
    a = _silu(_dot(h, wg_ref[...])) * _dot(h, wu_ref[...])
    if use_comb:
        comb = comb_ref[...]
        lane = lax.broadcasted_iota(jnp.int32, comb.shape, 1)
        a = a * jnp.sum(jnp.where(lane == e, comb, 0.0), axis=-1, keepdims=True)
    acc_scr[...] += _dot(a.astype(BF16), wd_ref[...])

    @pl.when((e == n_e - 1) & (f == n_f - 1))
    def _():
        o_ref[...] = x_ref[...] + gate_ref[...] * _rms(acc_scr[...], nw2_ref[...])


def _ffn(x, nw1, shift, scale, w_gu, w_down, gate, nw2, comb=None):
    b, s, d = x.shape
    n_e, _, f2 = w_gu.shape
    ff = f2 // 2
    tm = _pick(s, (1024, 512, 256))
    tf = _pick(ff, (512, 256))
    n_f = ff // tf
    vec = pl.BlockSpec((None, 1, d), lambda bi, i, e, f: (bi, 0, 0))
    const = pl.BlockSpec((1, d), lambda bi, i, e, f: (0, 0))
    row = pl.BlockSpec((None, tm, d), lambda bi, i, e, f: (bi, i, 0))
    in_specs = [row, const, vec, vec,
                pl.BlockSpec((None, d, tf), lambda bi, i, e, f: (e, 0, f)),
                pl.BlockSpec((None, d, tf), lambda bi, i, e, f: (e, 0, f + n_f)),
                pl.BlockSpec((None, tf, d), lambda bi, i, e, f: (e, f, 0)),
                vec, const]
    args = [x, nw1, shift, scale, w_gu, w_gu, w_down, gate, nw2]
    if comb is not None:
        in_specs.append(pl.BlockSpec((None, tm, LANES), lambda bi, i, e, f: (bi, i, 0)))
        args.append(comb)
    return pl.pallas_call(
        functools.partial(_ffn_kernel, n_e, n_f, comb is not None),
        out_shape=jax.ShapeDtypeStruct((b, s, d), F32),
        grid=(b, s // tm, n_e, n_f),
        in_specs=in_specs,
        out_specs=row,
        scratch_shapes=[pltpu.VMEM((tm, d), BF16), pltpu.VMEM((tm, d), F32)],
        compiler_params=_cparams("parallel", "parallel", "arbitrary", "arbitrary"),
        name="swiglu_residual",
    )(*args)


def _router_kernel(x_ref, nw_ref, sh_ref, sc_ref, rw_ref, comb_ref):
    h = _norm_mod(x_ref[...], nw_ref[...], sh_ref[...], sc_ref[...])
    logits = _router_logits(h, rw_ref[...])
    lane = lax.broadcasted_iota(jnp.int32, logits.shape, 1)
    i1, i2, g1, g2 = _top2(logits, lane)
    comb_ref[...] = jnp.where(lane == i1, g1, 0.0) + jnp.where(lane == i2, g2, 0.0)


def _router(x, nw, shift, scale, rw_pad):
    b, s, d = x.shape
    tm = _pick(s, (512, 256, 128))
    vec = pl.BlockSpec((None, 1, d), lambda bi, i: (bi, 0, 0))
    return pl.pallas_call(
        _router_kernel,
        out_shape=jax.ShapeDtypeStruct((b, s, LANES), F32),
        grid=(b, s // tm),
        in_specs=[pl.BlockSpec((None, tm, d), lambda bi, i: (bi, i, 0)),
                  pl.BlockSpec((1, d), lambda bi, i: (0, 0)), vec, vec,
                  pl.BlockSpec((d, LANES), lambda bi, i: (0, 0))],
        out_specs=pl.BlockSpec((None, tm, LANES), lambda bi, i: (bi, i, 0)),
        compiler_params=_cparams("parallel", "parallel"),
        name="router_top2",
    )(x, nw, shift, scale, rw_pad)


def _top2(logits, lane):
    neg = -jnp.inf
    logits = jnp.where(lane < N_EXPERTS, logits, neg)
    m1 = jnp.max(logits, axis=-1, keepdims=True)
    i1 = jnp.min(jnp.where(logits == m1, lane, LANES), axis=-1, keepdims=True)
    rest = jnp.where(lane == i1, neg, logits)
    m2 = jnp.max(rest, axis=-1, keepdims=True)
    i2 = jnp.min(jnp.where(rest == m2, lane, LANES), axis=-1, keepdims=True)
    e2 = jnp.exp(m2 - m1)
    den = 1.0 + e2
    return i1, i2, 1.0 / den, e2 / den


def _router_logits(h, rw):
    h_hi, h_lo = _split_bf16(h, 2)
    w_hi, w_lo = _split_bf16(rw, 2)
    return _dot(h_hi, w_hi) + _dot(h_hi, w_lo) + _dot(h_lo, w_hi)


META_I1, META_I2, META_G1, META_G2, META_R1, META_R2 = range(6)


def _route_kernel(x_ref, nw_ref, sh_ref, sc_ref, rw_ref, meta_ref, cnt_ref, carry):
    @pl.when(pl.program_id(0) == 0)
    def _():
        carry[...] = jnp.zeros(carry.shape, F32)

    h = _norm_mod(x_ref[...], nw_ref[...], sh_ref[...], sc_ref[...])
    logits = _router_logits(h, rw_ref[...])
    tm = logits.shape[0]
    lane = lax.broadcasted_iota(jnp.int32, logits.shape, 1)
    i1, i2, g1, g2 = _top2(logits, lane)
    oh1 = lane == i1
    oh2 = lane == i2
    r = lax.broadcasted_iota(jnp.int32, (tm, tm), 0)
    c = lax.broadcasted_iota(jnp.int32, (tm, tm), 1)
    earlier = jnp.where(c < r, 1.0, 0.0).astype(BF16)
    f1 = jnp.where(oh1, 1.0, 0.0)
    f2 = jnp.where(oh2, 1.0, 0.0)
    before1 = _dot(earlier, f1.astype(BF16))
    before2 = _dot(earlier, f2.astype(BF16))
    c1 = jnp.sum(f1, axis=0, keepdims=True)
    c2 = jnp.sum(f2, axis=0, keepdims=True)
    base = carry[...]
    rank1 = jnp.sum(jnp.where(oh1, before1 + base, 0.0), axis=-1, keepdims=True)
    rank2 = jnp.sum(jnp.where(oh2, before2 + base + c1, 0.0), axis=-1, keepdims=True)
    carry[...] = base + c1 + c2
    cnt_ref[...] = carry[...]
    meta = jnp.zeros(logits.shape, F32)
    for slot, val in ((META_I1, i1.astype(F32)), (META_I2, i2.astype(F32)), (META_G1, g1), (META_G2, g2),
                      (META_R1, rank1), (META_R2, rank2)):
        meta = jnp.where(lane == slot, val, meta)
    meta_ref[...] = meta


def _dispatch_kernel(pos_ref, x_ref, nw_ref, sh_ref, sc_ref, zeros_ref, xs_ref, hbuf, sem):
    del zeros_ref
    tm = hbuf.shape[0]
    hbuf[...] = _norm_mod(x_ref[...], nw_ref[...], sh_ref[...], sc_ref[...])

    def row_copy(r, p):
        return pltpu.make_async_copy(hbuf.at[pl.ds(r, 1)], xs_ref.at[pl.ds(p, 1)], sem)

    def issue(r, carry):
        row_copy(r, pos_ref[0, r]).start()
        row_copy(r, pos_ref[0, tm + r]).start()
        return carry

    lax.fori_loop(0, tm, issue, 0)

    def drain(r, carry):
        row_copy(r, pos_ref[0, r]).wait()
        row_copy(r, pos_ref[0, tm + r]).wait()
        return carry

    lax.fori_loop(0, tm, drain, 0)


def _expert_kernel(n_f, te_ref, na_ref, xs_ref, wg_ref, wu_ref, wd_ref, ys_ref, h_scr):
    i = pl.program_id(0)
    f = pl.program_id(1)

    @pl.when(i < na_ref[0])
    def _():
        @pl.when(f == 0)
        def _():
            h_scr[...] = xs_ref[...].astype(BF16)

        h = h_scr[...]
        a = _silu(_dot(h, wg_ref[...])) * _dot(h, wu_ref[...])
        part = _dot(a.astype(BF16), wd_ref[...])

        @pl.when(f == 0)
        def _():
            ys_ref[...] = part

        @pl.when(f > 0)
        def _():
            ys_ref[...] += part

    @pl.when((i >= na_ref[0]) & (f == 0))
    def _():
        ys_ref[...] = jnp.zeros(ys_ref.shape, F32)


def _combine_kernel(pos_ref, ys_ref, meta_ref, x_ref, gate_ref, nw_ref, o_ref, buf, sem):
    tm = x_ref.shape[0]

    def row_copy(slot, r, p):
        return pltpu.make_async_copy(ys_ref.at[pl.ds(p, 1)], buf.at[slot, pl.ds(r, 1)], sem)

    def issue(r, carry):
        row_copy(0, r, pos_ref[0, r]).start()
        row_copy(1, r, pos_ref[0, tm + r]).start()
        return carry

    lax.fori_loop(0, tm, issue, 0)

    def drain(r, carry):
        row_copy(0, r, pos_ref[0, r]).wait()
        row_copy(1, r, pos_ref[0, tm + r]).wait()
        return carry

    lax.fori_loop(0, tm, drain, 0)
    meta = meta_ref[...]
    lane = lax.broadcasted_iota(jnp.int32, meta.shape, 1)
    g1 = jnp.sum(jnp.where(lane == META_G1, meta, 0.0), axis=-1, keepdims=True)
    g2 = jnp.sum(jnp.where(lane == META_G2, meta, 0.0), axis=-1, keepdims=True)
    y = g1 * buf[0] + g2 * buf[1]
    o_ref[...] = x_ref[...] + gate_ref[...] * _rms(y, nw_ref[...])


def _moe_sparse(x, nw1, shift, scale, rw_pad, w_gu, w_down, gate, nw2):
    b, s, d = x.shape
    n = b * s
    tm = _pick(s, (512, 256, 128))
    tpb = s // tm
    nt = n // tm
    n_e, _, f2 = w_gu.shape
    ff = f2 // 2
    tf = _pick(ff, (512, 256))
    n_f = ff // tf
    te_rows = EXPERT_TILE
    n_rows = 2 * n + n_e * te_rows
    nte = n_rows // te_rows

    xrow = pl.BlockSpec((None, tm, d), lambda i: (i // tpb, i % tpb, 0))
    vec = pl.BlockSpec((None, 1, d), lambda i: (i // tpb, 0, 0))
    const = lambda shp: pl.BlockSpec(shp, lambda i: (0, 0))
    mrow = pl.BlockSpec((None, tm, LANES), lambda i: (i // tpb, i % tpb, 0))
    meta, cnt = pl.pallas_call(
        _route_kernel,
        out_shape=[jax.ShapeDtypeStruct((b, s, LANES), F32), jax.ShapeDtypeStruct((1, LANES), F32)],
        grid=(nt,),
        in_specs=[xrow, const((1, d)), vec, vec, const((d, LANES))],
        out_specs=[mrow, const((1, LANES))],
        scratch_shapes=[pltpu.VMEM((1, LANES), F32)],
        compiler_params=_cparams("arbitrary"),
        name="route_top2_rank",
    )(x, nw1, shift, scale, rw_pad)

    m2 = meta.reshape(n, LANES)
    i1 = m2[:, META_I1].astype(jnp.int32)
    i2 = m2[:, META_I2].astype(jnp.int32)
    counts = cnt[0, :n_e].astype(jnp.int32)
    padded = (counts + te_rows - 1) // te_rows * te_rows
    ends = jnp.cumsum(padded)
    starts = ends - padded
    pos1 = jnp.take(starts, i1) + m2[:, META_R1].astype(jnp.int32)
    pos2 = jnp.take(starts, i2) + m2[:, META_R2].astype(jnp.int32)
    pos = jnp.concatenate([pos1.reshape(nt, 1, tm), pos2.reshape(nt, 1, tm)], axis=2)
    n_active = (ends[-1] // te_rows).astype(jnp.int32).reshape(1)
    tile_idx = jnp.minimum(jnp.arange(nte, dtype=jnp.int32), n_active[0] - 1)
    tile_expert = jnp.minimum(jnp.sum((tile_idx * te_rows)[:, None] >= ends[None, :], axis=1), n_e - 1).astype(jnp.int32)

    pos_spec = pl.BlockSpec((None, 1, 2 * tm), lambda i: (i, 0, 0), memory_space=pltpu.SMEM)
    any_spec = pl.BlockSpec(memory_space=pl.ANY)
    xs = pl.pallas_call(
        _dispatch_kernel,
        out_shape=jax.ShapeDtypeStruct((n_rows, d), F32),
        grid=(nt,),
        in_specs=[pos_spec, xrow, const((1, d)), vec, vec, any_spec],
        out_specs=any_spec,
        scratch_shapes=[pltpu.VMEM((tm, d), F32), pltpu.SemaphoreType.DMA],
        input_output_aliases={5: 0},
        compiler_params=_cparams("arbitrary"),
        name="moe_dispatch",
    )(pos, x, nw1, shift, scale, jnp.zeros((n_rows, d), F32))

    def live(i, na):
        return jnp.minimum(i, na[0] - 1)

    def fidx(i, f, na):
        return jnp.where(i < na[0], f, n_f - 1)

    ys = pl.pallas_call(
        functools.partial(_expert_kernel, n_f),
        out_shape=jax.ShapeDtypeStruct((n_rows, d), F32),
        grid_spec=pltpu.PrefetchScalarGridSpec(
            num_scalar_prefetch=2,
            grid=(nte, n_f),
            in_specs=[pl.BlockSpec((te_rows, d), lambda i, f, te, na: (live(i, na), 0)),
                      pl.BlockSpec((None, d, tf), lambda i, f, te, na: (te[i], 0, fidx(i, f, na))),
                      pl.BlockSpec((None, d, tf), lambda i, f, te, na: (te[i], 0, fidx(i, f, na) + n_f)),
                      pl.BlockSpec((None, tf, d), lambda i, f, te, na: (te[i], fidx(i, f, na), 0))],
            out_specs=pl.BlockSpec((te_rows, d), lambda i, f, te, na: (i, 0)),
            scratch_shapes=[pltpu.VMEM((te_rows, d), BF16)]),
        compiler_params=_cparams("arbitrary", "arbitrary"),
        name="moe_experts",
    )(tile_expert, n_active, xs, w_gu, w_gu, w_down)

    return pl.pallas_call(
        _combine_kernel,
        out_shape=jax.ShapeDtypeStruct((b, s, d), F32),
        grid=(nt,),
        in_specs=[pos_spec, any_spec, mrow, xrow, vec, const((1, d))],
        out_specs=xrow,
        scratch_shapes=[pltpu.VMEM((2, tm, d), F32), pltpu.SemaphoreType.DMA],
        compiler_params=_cparams("arbitrary"),
        name="moe_combine",
    )(pos, ys, meta, x, gate, nw2)


def _conv_kernel(nt, prev_ref, cur_ref, next_ref, w_ref, b_ref, xs_ref, bm_ref, cm_ref, buf):
    i = pl.program_id(1)
    tm = cur_ref.shape[0]
    halo = SUBLANES
    buf[halo:halo + tm, :] = cur_ref[...]
    buf[0:halo, :] = jnp.where(i > 0, prev_ref[...], 0.0)
    buf[halo + tm:2 * halo + tm, :] = jnp.where(i < nt - 1, next_ref[...], 0.0)
    acc = None
    for j in range(D_CONV):
        start = halo - CONV_PAD + j
        t = w_ref[j:j + 1, :] * buf[start:start + tm, :]
        acc = t if acc is None else acc + t
    y = _silu(acc + b_ref[...])
    d_inner = xs_ref.shape[1]
    gn = bm_ref.shape[1]
    xs_ref[...] = y[:, :d_inner]
    bm_ref[...] = y[:, d_inner:d_inner + gn]
    cm_ref[...] = y[:, d_inner + gn:]


def _conv_silu(xbc, conv_w, conv_b, d_inner):
    b, s, c = xbc.shape
    gn = (c - d_inner) // 2
    tm = _pick(s, (512, 256, 128))
    nt = s // tm
    hb = tm // SUBLANES
    last = s // SUBLANES - 1
    row = lambda w: pl.BlockSpec((None, tm, w), lambda bi, i: (bi, i, 0))
    return pl.pallas_call(
        functools.partial(_conv_kernel, nt),
        out_shape=[jax.ShapeDtypeStruct((b, s, d_inner), F32),
                   jax.ShapeDtypeStruct((b, s, gn), F32), jax.ShapeDtypeStruct((b, s, gn), F32)],
        grid=(b, nt),
        in_specs=[pl.BlockSpec((None, SUBLANES, c), lambda bi, i: (bi, jnp.maximum(i * hb - 1, 0), 0)),
                  row(c),
                  pl.BlockSpec((None, SUBLANES, c), lambda bi, i: (bi, jnp.minimum((i + 1) * hb, last), 0)),
                  pl.BlockSpec((D_CONV, c), lambda bi, i: (0, 0)),
                  pl.BlockSpec((1, c), lambda bi, i: (0, 0))],
        out_specs=[row(d_inner), row(gn), row(gn)],
        scratch_shapes=[pltpu.VMEM((tm + 2 * SUBLANES, c), F32)],
        compiler_params=_cparams("parallel", "parallel"),
        name="dwconv_silu",
    )(xbc, xbc, xbc, conv_w, conv_b)


def _softplus(x):
    return jnp.maximum(x, 0.0) + jnp.log(1.0 + jnp.exp(-jnp.abs(x)))


def _ssd_kernel(direction, nc, xs_ref, bm_ref, cm_ref, dt_ref, dtb_ref, a_ref, h0_ref, e_ref,
                y_ref, hT_ref, st_scr):
    c = pl.program_id(1)

    @pl.when(c == 0)
    def _():
        st_scr[...] = h0_ref[...]

    q = CHUNK
    fwd = direction == 0
    dt = _softplus(dt_ref[...] + dtb_ref[...])
    d_a = dt * a_ref[...]
    row = lax.broadcasted_iota(jnp.int32, (q, q), 0)
    col = lax.broadcasted_iota(jnp.int32, (q, q), 1)
    causal = (row >= col) if fwd else (row <= col)
    tri = jnp.where(causal, 1.0, 0.0).astype(BF16)
    a_cum = _dot_split_rhs(tri, d_a, 3)
    a_cum_t = a_cum.T
    expand = e_ref[...]
    dt_x = _dot_split_lhs(dt, expand, 2)
    a_x = _dot_split_lhs(a_cum, expand, 3)
    last = q - 1 if fwd else 0
    a_last = a_x[last:last + 1, :]
    xdt = xs_ref[...] * dt_x
    xdt_b = xdt.astype(BF16)
    w_state = (jnp.exp(a_last - a_x) * xdt).astype(BF16)
    exp_a = jnp.exp(a_x)
    exp_last = jnp.exp(a_last)
    low = lax.broadcasted_iota(jnp.int32, (q, LANES), 1) < SSM_HEAD_DIM
    gw = HEADS_PER_GROUP * SSM_HEAD_DIM
    for g in range(N_GROUPS):
        gs = slice(g * gw, (g + 1) * gw)
        b_g = bm_ref[:, g * D_STATE:(g + 1) * D_STATE]
        c_b = cm_ref[:, g * D_STATE:(g + 1) * D_STATE].astype(BF16)
        cb = lax.dot_general(c_b, b_g.astype(BF16), (((1,), (1,)), ((), ())), preferred_element_type=F32)
        st_g = st_scr[:, gs]
        y_off = _dot(c_b, st_g.astype(BF16)) * exp_a[:, gs]
        s_new = _dot(b_g.T.astype(BF16), w_state[:, gs])
        for pr in range(HEADS_PER_GROUP // 2):
            ys = []
            for j in range(2):
                hl = direction * N_SSM_HEADS + g * HEADS_PER_GROUP + 2 * pr + j
                seg = a_cum[:, hl:hl + 1] - a_cum_t[hl:hl + 1, :]
                dec = jnp.exp(jnp.where(causal, seg, -jnp.inf))
                ys.append(_dot((cb * dec).astype(BF16), xdt_b[:, g * gw + pr * LANES:g * gw + (pr + 1) * LANES]))
            y_ref[:, g * gw + pr * LANES:g * gw + (pr + 1) * LANES] = (
                jnp.where(low, ys[0], ys[1]) + y_off[:, pr * LANES:(pr + 1) * LANES])
        st_scr[:, gs] = exp_last[:, gs] * st_g + s_new

    @pl.when(c == nc - 1)
    def _():
        hT_ref[...] = st_scr[...]


def _ssd_scan(direction, xs, bm, cm, dt_raw, dt_bias, a_neg, h0):
    b, s, d_inner = xs.shape
    gn = bm.shape[2]
    nc = s // CHUNK
    heads = np.arange(d_inner) // SSM_HEAD_DIM + direction * N_SSM_HEADS
    expand = jnp.asarray(np.arange(LANES)[:, None] == heads[None, :], BF16)
    if direction == 0:
        cidx = lambda bi, c: (bi, c, 0)
    else:
        cidx = lambda bi, c: (bi, nc - 1 - c, 0)
    const = lambda bi, c: (0, 0)
    st_spec = pl.BlockSpec((None, D_STATE, d_inner), lambda bi, c: (bi, 0, 0))
    return pl.pallas_call(
        functools.partial(_ssd_kernel, direction, nc),
        out_shape=[jax.ShapeDtypeStruct((b, s, d_inner), F32),
                   jax.ShapeDtypeStruct((b, D_STATE, d_inner), F32)],
        grid=(b, nc),
        in_specs=[pl.BlockSpec((None, CHUNK, d_inner), cidx),
                  pl.BlockSpec((None, CHUNK, gn), cidx), pl.BlockSpec((None, CHUNK, gn), cidx),
                  pl.BlockSpec((None, CHUNK, LANES), cidx),
                  pl.BlockSpec((1, LANES), const), pl.BlockSpec((1, LANES), const),
                  st_spec, pl.BlockSpec((LANES, d_inner), const)],
        out_specs=[pl.BlockSpec((None, CHUNK, d_inner), cidx), st_spec],
        scratch_shapes=[pltpu.VMEM((D_STATE, d_inner), F32)],
        compiler_params=_cparams("parallel", "arbitrary"),
        name="ssd_scan",
    )(xs, bm, cm, dt_raw, dt_bias, a_neg, h0, expand)


def _ssdout_kernel(yf_ref, yb_ref, xs_ref, z_ref, dsk_ref, gnw_ref, w_ref, x_ref, gate_ref, nw_ref, o_ref):
    y = yf_ref[...] + yb_ref[...] + dsk_ref[...] * xs_ref[...]
    gy = y * _silu(z_ref[...])
    gnw = gnw_ref[...]
    gw = gy.shape[1] // N_GROUPS
    parts = []
    for g in range(N_GROUPS):
        parts.append(_rms(gy[:, g * gw:(g + 1) * gw], gnw[:, g * gw:(g + 1) * gw]).astype(BF16))
    out = _dot(jnp.concatenate(parts, axis=1), w_ref[...])
    o_ref[...] = x_ref[...] + gate_ref[...] * _rms(out, nw_ref[...])


def _ssd_out(yf, yb, xs, z, d_skip, gnw, w_out, x, gate, nw):
    b, s, d = x.shape
    d_inner = xs.shape[2]
    tm = _pick(s, (256, 128))
    row = lambda w: pl.BlockSpec((None, tm, w), lambda bi, i: (bi, i, 0))
    const = lambda shp: pl.BlockSpec(shp, lambda bi, i: (0, 0))
    return pl.pallas_call(
        _ssdout_kernel,
        out_shape=jax.ShapeDtypeStruct((b, s, d), F32),
        grid=(b, s // tm),
        in_specs=[row(d_inner), row(d_inner), row(d_inner), row(d_inner),
                  const((1, d_inner)), const((1, d_inner)), const(w_out.shape), row(d),
                  pl.BlockSpec((None, 1, d), lambda bi, i: (bi, 0, 0)), const((1, d))],
        out_specs=row(d),
        compiler_params=_cparams("parallel", "parallel"),
        name="ssd_gate_out_residual",
    )(yf, yb, xs, z, d_skip, gnw, w_out, x, gate, nw)


def _rope_tables(seq):
    rows = seq // GRID_W
    row = jnp.repeat(jnp.arange(rows, dtype=F32), GRID_W)
    col = jnp.tile(jnp.arange(GRID_W, dtype=F32), rows)
    freqs = ROPE_THETA ** (-jnp.arange(0, AXIS_ROT_DIM, 2, dtype=F32) / AXIS_ROT_DIM)
    ang = jnp.concatenate([row[:, None] * freqs, col[:, None] * freqs], axis=-1)
    cos = jnp.repeat(jnp.cos(ang), 2, axis=-1)
    sin = jnp.repeat(jnp.sin(ang), 2, axis=-1) * jnp.tile(jnp.asarray([-1.0, 1.0], F32), HEAD_DIM // 2)
    return jnp.tile(cos, (1, LANES // HEAD_DIM)), jnp.tile(sin, (1, LANES // HEAD_DIM))


def _mod_vectors(mods_i, batch):
    d = mods_i.shape[1] // 6
    mx = [mods_i[:batch, k * d:(k + 1) * d].reshape(batch, 1, d) for k in range(6)]
    mc = [jnp.broadcast_to(mods_i[batch, k * d:(k + 1) * d].reshape(1, 1, d), (batch, 1, d)) for k in range(6)]
    return mx, mc


def _even_mixer(x, ctx, nw0, nw1, mx, mc, w_in, qnw, knw, w_out, rope, need_ctx):
    w_u = w_in[:, :F_WIDTH].astype(BF16)
    w_qkv = w_in[:, F_WIDTH:].astype(BF16)
    qw = jnp.tile(qnw.reshape(1, HEAD_DIM), (1, LANES // HEAD_DIM))
    kw = jnp.tile(knw.reshape(1, HEAD_DIM), (1, LANES // HEAD_DIM))
    w_f = w_out[:F_WIDTH].astype(BF16)
    d = w_out.shape[1]
    w_a = (w_out[F_WIDTH:].reshape(N_KV_HEADS, GQ, HEAD_DIM, d).transpose(1, 0, 2, 3)
           .reshape(Q_WIDTH, d).astype(BF16))
    lc = ctx.shape[1]
    ones = jnp.ones((lc, LANES), F32)
    zeros = jnp.zeros((lc, LANES), F32)

    u_c, qkv_c = _norm_mod_linear(ctx, nw0, mc[0], mc[1], [w_u, w_qkv], [F32, F32])
    q_c, k_c, v_c = _qk_prepare(qkv_c, ones, zeros, qw, kw)
    u_x, qkv_x = _norm_mod_linear(x, nw0, mx[0], mx[1], [w_u, w_qkv], [F32, F32])
    q_x, k_x, v_x = _qk_prepare(qkv_x, rope[0], rope[1], qw, kw)
    k_all = jnp.concatenate([k_c, k_x], axis=1)
    v_all = jnp.concatenate([v_c, v_x], axis=1)
    a_x = _attention(q_x, k_all, v_all)
    f_x = _fourier_mix_long(u_x)
    x_new = _mixer_out(f_x, a_x, w_f, w_a, x, mx[2], nw1)
    if not need_ctx:
        return x_new, ctx
    a_c = _attention(q_c, k_c, v_c)
    f_c = _fourier_mix_short(u_c)
    return x_new, _mixer_out(f_c, a_c, w_f, w_a, ctx, mc[2], nw1)


def _ssd_branch(h_in, nw0, shift, scale, w_z, w_xbc, w_dt, conv_w, conv_b, dt_bias, a_neg, h0_f, h0_b):
    z, xbc, dt_raw = _norm_mod_linear(h_in, nw0, shift, scale, [w_z, w_xbc, w_dt], [F32, F32, F32], tm_max=256)
    xs, bm, cm = _conv_silu(xbc, conv_w, conv_b, w_z.shape[1])
    y_f, s_f = _ssd_scan(0, xs, bm, cm, dt_raw, dt_bias, a_neg, h0_f)
    y_b, s_b = _ssd_scan(1, xs, bm, cm, dt_raw, dt_bias, a_neg, h0_b)
    return z, xs, y_f, y_b, s_f, s_b


def _odd_mixer(x, ctx, nw0, nw1, mx, mc, w_in, conv_w, conv_b, dt_bias, a_log, d_skip, gnw, w_out, need_ctx):
    d_inner = w_out.shape[0]
    conv_dim = conv_w.shape[1]
    w_z = w_in[:, :d_inner].astype(BF16)
    w_xbc = w_in[:, d_inner:d_inner + conv_dim].astype(BF16)
    n_dt = w_in.shape[1] - d_inner - conv_dim
    w_dt = jnp.pad(w_in[:, d_inner + conv_dim:], ((0, 0), (0, LANES - n_dt))).astype(BF16)
    dtb = jnp.pad(dt_bias.reshape(1, n_dt), ((0, 0), (0, LANES - n_dt)))
    a_neg = jnp.pad(-jnp.exp(a_log.astype(F32)).reshape(1, n_dt), ((0, 0), (0, LANES - n_dt)))
    cb = conv_b.reshape(1, conv_dim)
    dsk = jnp.repeat(d_skip, SSM_HEAD_DIM).reshape(1, d_inner)
    gw = gnw.reshape(1, d_inner)
    w_o = w_out.astype(BF16)
    b = x.shape[0]
    zeros = jnp.zeros((b, D_STATE, d_inner), F32)

    z_c, xs_c, yf_c, yb_c, s_f, s_b = _ssd_branch(ctx, nw0, mc[0], mc[1], w_z, w_xbc, w_dt, conv_w, cb,
                                                  dtb, a_neg, zeros, zeros)
    z_x, xs_x, yf_x, yb_x, _, _ = _ssd_branch(x, nw0, mx[0], mx[1], w_z, w_xbc, w_dt, conv_w, cb,
                                              dtb, a_neg, s_f, s_b)
    x_new = _ssd_out(yf_x, yb_x, xs_x, z_x, dsk, gw, w_o, x, mx[2], nw1)
    if not need_ctx:
        return x_new, ctx
    return x_new, _ssd_out(yf_c, yb_c, xs_c, z_c, dsk, gw, w_o, ctx, mc[2], nw1)


def kernel(x, c, ctx, c_ctx, ada_w, ada_b, norm_w, mix_in_w, q_norm_w, k_norm_w, mix_out_w, ffn_gu_w, ffn_down_w, ssm_in_w, ssm_conv_w, ssm_conv_b, ssm_dt_bias, ssm_a_log, ssm_d, ssm_norm_w, ssm_out_w, router_w, moe_gu_w, moe_down_w):
    batch, seq, d = x.shape
    depth = ada_w.shape[0]
    rope = _rope_tables(seq)
    cvec = jnp.zeros((SUBLANES, d), F32).at[:batch].set(c).at[batch].set(c_ctx)
    mods = _modulation(cvec, ada_w, ada_b)
    for i in range(depth):
        last = i == depth - 1
        j = i // 2
        mx, mc = _mod_vectors(mods[i], batch)
        nw = [norm_w[i, k].reshape(1, d) for k in range(4)]
        if i % 2 == 0:
            x, ctx_mid = _even_mixer(x, ctx, nw[0], nw[1], mx, mc, mix_in_w[j], q_norm_w[j], k_norm_w[j],
                                     mix_out_w[j], rope, not last)
            w_gu = ffn_gu_w[j].astype(BF16)[None]
            w_dn = ffn_down_w[j].astype(BF16)[None]
            x = _ffn(x, nw[2], mx[3], mx[4], w_gu, w_dn, mx[5], nw[3])
            if not last:
                ctx = _ffn(ctx_mid, nw[2], mc[3], mc[4], w_gu, w_dn, mc[5], nw[3])
        else:
            x, ctx_mid = _odd_mixer(x, ctx, nw[0], nw[1], mx, mc, ssm_in_w[j], ssm_conv_w[j], ssm_conv_b[j],
                                    ssm_dt_bias[j], ssm_a_log[j], ssm_d[j], ssm_norm_w[j], ssm_out_w[j],
                                    not last)
            rw = jnp.pad(router_w[j], ((0, 0), (0, LANES - N_EXPERTS)))
            w_gu = moe_gu_w[j].astype(BF16)
            w_dn = moe_down_w[j].astype(BF16)
            x = _moe_sparse(x, nw[2], mx[3], mx[4], rw, w_gu, w_dn, mx[5], nw[3])
            if not last:
                comb_c = _router(ctx_mid, nw[2], mc[3], mc[4], rw)
                ctx = _ffn(ctx_mid, nw[2], mc[3], mc[4], w_gu, w_dn, mc[5], nw[3], comb_c)
    return x
```

```python
import functools

import numpy as np
import jax
import jax.numpy as jnp
from jax import lax
from jax.experimental import pallas as pl
from jax.experimental.pallas import tpu as pltpu

F32 = jnp.float32
BF16 = jnp.bfloat16

EPS = 1e-6
GRID_W = 64
HEAD_DIM = 64
N_HEADS = 8
N_KV_HEADS = 2
GQ = N_HEADS // N_KV_HEADS
F_GROUPS = 4
F_GROUP_DIM = 128
F_WIDTH = F_GROUPS * F_GROUP_DIM
Q_WIDTH = N_HEADS * HEAD_DIM
KV_WIDTH = N_KV_HEADS * HEAD_DIM
ROPE_THETA = 10000.0
AXIS_ROT_DIM = HEAD_DIM // 2
ATTN_SCALE = HEAD_DIM ** -0.5
LOG2_E = 1.4426950408889634
SSM_HEAD_DIM = 64
N_SSM_HEADS = 32
N_GROUPS = 4
HEADS_PER_GROUP = N_SSM_HEADS // N_GROUPS
D_STATE = 128
D_CONV = 5
CONV_PAD = D_CONV // 2
CHUNK = 128
N_EXPERTS = 8
EXPERT_TILE = 1024
LANES = 128
SUBLANES = 8
FFT_N2 = 128
VMEM_LIMIT = 56 * 1024 * 1024


def _cparams(*sem):
    return pltpu.CompilerParams(dimension_semantics=sem, vmem_limit_bytes=VMEM_LIMIT)


def _pick(n, candidates):
    for c in candidates:
        if n % c == 0:
            return c
    return n


def _sigmoid(x):
    return 1.0 / (1.0 + jnp.exp(-x))


def _silu(x):
    return x * _sigmoid(x)


def _rms(x, w):
    return x * lax.rsqrt(jnp.mean(x * x, axis=-1, keepdims=True) + EPS) * w


def _norm_mod(x, nw, shift, scale):
    return _rms(x, nw) * (1.0 + scale) + shift


def _dot(a, b):
    return jnp.dot(a, b, preferred_element_type=F32)


def _split_bf16(x, parts):
    out = []
    r = x
    for _ in range(parts):
        p = r.astype(BF16)
        out.append(p)
        r = r - p.astype(F32)
    return out


def _dot_split_lhs(x, m_stacked, parts):
    return _dot(jnp.concatenate(_split_bf16(x, parts), axis=1), m_stacked)


def _dot_split_rhs(m, x, parts):
    acc = None
    for p in _split_bf16(x, parts):
        t = _dot(m, p)
        acc = t if acc is None else acc + t
    return acc


def _mod_kernel(c_ref, w_ref, b_ref, o_ref):
    s = _silu(c_ref[...])
    o_ref[...] = jnp.dot(s, w_ref[...], preferred_element_type=F32,
                         precision=lax.Precision.HIGHEST) + b_ref[...]


def _modulation(cvec, ada_w, ada_b):
    depth, d, n = ada_w.shape
    tn = _pick(n, (1536, 1024, 512))
    rows = cvec.shape[0]
    return pl.pallas_call(
        _mod_kernel,
        out_shape=jax.ShapeDtypeStruct((depth, rows, n), F32),
        grid=(depth, n // tn),
        in_specs=[pl.BlockSpec((rows, d), lambda i, j: (0, 0)),
                  pl.BlockSpec((None, d, tn), lambda i, j: (i, 0, j)),
                  pl.BlockSpec((None, 1, tn), lambda i, j: (i, 0, j))],
        out_specs=pl.BlockSpec((None, rows, tn), lambda i, j: (i, 0, j)),
        compiler_params=_cparams("parallel", "parallel"),
        name="adaln_modulation",
    )(cvec, ada_w, ada_b.reshape(depth, 1, n))


def _nml_kernel(n_w, x_ref, nw_ref, sh_ref, sc_ref, *refs):
    h = _norm_mod(x_ref[...], nw_ref[...], sh_ref[...], sc_ref[...]).astype(BF16)
    for w_ref, o_ref in zip(refs[:n_w], refs[n_w:]):
        o_ref[...] = _dot(h, w_ref[...]).astype(o_ref.dtype)


def _norm_mod_linear(x, nw, shift, scale, weights, out_dtypes, tm_max=512):
    b, s, d = x.shape
    tm = _pick(s, (tm_max, 256, 128))
    n_w = len(weights)
    vec = pl.BlockSpec((None, 1, d), lambda bi, i: (bi, 0, 0))
    in_specs = [pl.BlockSpec((None, tm, d), lambda bi, i: (bi, i, 0)),
                pl.BlockSpec((1, d), lambda bi, i: (0, 0)), vec, vec]
    in_specs += [pl.BlockSpec(w.shape, lambda bi, i: (0, 0)) for w in weights]
    out_shape = [jax.ShapeDtypeStruct((b, s, w.shape[1]), dt) for w, dt in zip(weights, out_dtypes)]
    out_specs = [pl.BlockSpec((None, tm, w.shape[1]), lambda bi, i: (bi, i, 0)) for w in weights]
    return pl.pallas_call(
        functools.partial(_nml_kernel, n_w),
        out_shape=out_shape,
        grid=(b, s // tm),
        in_specs=in_specs,
        out_specs=out_specs,
        compiler_params=_cparams("parallel", "parallel"),
        name="norm_mod_linear",
    )(x, nw, shift, scale, *weights)


def _qkprep_kernel(qkv_ref, cos_ref, sin_ref, qw_ref, kw_ref, bd_ref, q_ref, k_ref, v_ref):
    tm = qkv_ref.shape[0]
    lane = lax.broadcasted_iota(jnp.int32, (tm, LANES), 1)
    even = (lane % 2) == 0
    low = lane < HEAD_DIM
    cos = cos_ref[...]
    sin = sin_ref[...]
    bd = bd_ref[...]

    def prep(xb, w):
        ms = _dot_split_lhs(xb * xb, bd, 2)
        y = xb * lax.rsqrt(ms + EPS) * w
        swapped = jnp.where(even, pltpu.roll(y, LANES - 1, 1), pltpu.roll(y, 1, 1))
        return y * cos + swapped * sin

    qw = qw_ref[...]
    for cblk in range(Q_WIDTH // LANES):
        r = prep(qkv_ref[:, cblk * LANES:(cblk + 1) * LANES], qw) * (ATTN_SCALE * LOG2_E)
        r_sw = pltpu.roll(r, HEAD_DIM, 1)
        kvh = (2 * cblk) // GQ
        keep = low if kvh == 0 else jnp.logical_not(low)
        for j in range(2):
            src = r if j == kvh else r_sw
            q_ref[2 * cblk + j] = jnp.where(keep, src, 0.0).astype(q_ref.dtype)
    k_ref[...] = prep(qkv_ref[:, Q_WIDTH:Q_WIDTH + KV_WIDTH], kw_ref[...]).astype(k_ref.dtype)
    v_ref[...] = qkv_ref[:, Q_WIDTH + KV_WIDTH:Q_WIDTH + 2 * KV_WIDTH].astype(v_ref.dtype)


def _qk_prepare(qkv, cos_t, sin_t, qw, kw):
    b, s, _ = qkv.shape
    tm = _pick(s, (512, 256, 128))
    bd = np.kron(np.eye(2, dtype=np.float32), np.full((HEAD_DIM, HEAD_DIM), 1.0 / HEAD_DIM, np.float32))
    bd = np.tile(bd, (2, 1))
    row = lambda bi, i: (i, 0)
    const = lambda bi, i: (0, 0)
    return pl.pallas_call(
        _qkprep_kernel,
        out_shape=[jax.ShapeDtypeStruct((b, N_HEADS, s, LANES), BF16),
                   jax.ShapeDtypeStruct((b, s, KV_WIDTH), BF16),
                   jax.ShapeDtypeStruct((b, s, KV_WIDTH), BF16)],
        grid=(b, s // tm),
        in_specs=[pl.BlockSpec((None, tm, qkv.shape[2]), lambda bi, i: (bi, i, 0)),
                  pl.BlockSpec((tm, LANES), row), pl.BlockSpec((tm, LANES), row),
                  pl.BlockSpec((1, LANES), const), pl.BlockSpec((1, LANES), const),
                  pl.BlockSpec((2 * LANES, LANES), const)],
        out_specs=[pl.BlockSpec((None, N_HEADS, tm, LANES), lambda bi, i: (bi, 0, i, 0)),
                   pl.BlockSpec((None, tm, KV_WIDTH), lambda bi, i: (bi, i, 0)),
                   pl.BlockSpec((None, tm, KV_WIDTH), lambda bi, i: (bi, i, 0))],
        compiler_params=_cparams("parallel", "parallel"),
        name="qk_norm_rope",
    )(qkv, cos_t, sin_t, qw, kw, jnp.asarray(bd, BF16))


def _attn_kernel(nk, q_ref, k_ref, v_ref, o_ref, m_scr, l_scr, acc_scr):
    kv = pl.program_id(2)

    @pl.when(kv == 0)
    def _():
        m_scr[...] = jnp.full(m_scr.shape, -jnp.inf, F32)
        l_scr[...] = jnp.zeros(l_scr.shape, F32)
        acc_scr[...] = jnp.zeros(acc_scr.shape, F32)

    k = k_ref[...]
    v = v_ref[...]
    n_t = k.shape[0] // LANES

    def head(hd, carry):
        s = lax.dot_general(q_ref[hd], k, (((1,), (1,)), ((), ())), preferred_element_type=F32)
        tiles = [s[:, t * LANES:(t + 1) * LANES] for t in range(n_t)]
        smax = tiles[0]
        for t in range(1, n_t):
            smax = jnp.maximum(smax, tiles[t])
        m_prev = m_scr[hd]
        m_new = jnp.maximum(m_prev, jnp.max(smax, axis=-1, keepdims=True))
        alpha = jnp.exp2(m_prev - m_new)
        ps = []
        lsum = None
        for t in range(n_t):
            p_t = jnp.exp2(tiles[t] - m_new)
            lsum = p_t if lsum is None else lsum + p_t
            ps.append(p_t.astype(BF16))
        l_scr[hd] = alpha * l_scr[hd] + lsum
        acc_scr[hd] = alpha * acc_scr[hd] + _dot(jnp.concatenate(ps, axis=1), v)
        m_scr[hd] = m_new
        return carry

    lax.fori_loop(0, N_HEADS, head, 0, unroll=4)

    @pl.when(kv == nk - 1)
    def _():
        tq = o_ref.shape[0]
        low = lax.broadcasted_iota(jnp.int32, (tq, LANES), 1) < HEAD_DIM
        for g in range(GQ):
            o0 = acc_scr[g] / jnp.sum(l_scr[g], axis=-1, keepdims=True)
            o1 = acc_scr[GQ + g] / jnp.sum(l_scr[GQ + g], axis=-1, keepdims=True)
            o_ref[:, g * LANES:(g + 1) * LANES] = jnp.where(low, o0, o1).astype(o_ref.dtype)


def _attention(q, k, v):
    b, _, s, _ = q.shape
    skv = k.shape[1]
    tq = _pick(s, (1024, 512, 256, 128))
    tk = _pick(skv, (1280, 640, 256, 128))
    nk = skv // tk
    return pl.pallas_call(
        functools.partial(_attn_kernel, nk),
        out_shape=jax.ShapeDtypeStruct((b, s, Q_WIDTH), BF16),
        grid=(b, s // tq, nk),
        in_specs=[pl.BlockSpec((None, N_HEADS, tq, LANES), lambda bi, i, j: (bi, 0, i, 0)),
                  pl.BlockSpec((None, tk, KV_WIDTH), lambda bi, i, j: (bi, j, 0)),
                  pl.BlockSpec((None, tk, KV_WIDTH), lambda bi, i, j: (bi, j, 0))],
        out_specs=pl.BlockSpec((None, tq, Q_WIDTH), lambda bi, i, j: (bi, i, 0)),
        scratch_shapes=[pltpu.VMEM((N_HEADS, tq, LANES), F32), pltpu.VMEM((N_HEADS, tq, LANES), F32),
                        pltpu.VMEM((N_HEADS, tq, LANES), F32)],
        compiler_params=_cparams("parallel", "parallel", "arbitrary"),
        name="flash_attention",
    )(q, k, v)


def _dft_cos_sin(n):
    idx = np.arange(n, dtype=np.int64)
    ang = 2.0 * np.pi * ((idx[:, None] * idx[None, :]) % n).astype(np.float64) / n
    return np.cos(ang), np.sin(ang)


def _fft1_kernel(x_ref, w_ref, yr_ref, yi_ref):
    n1 = x_ref.shape[0]
    y = _dot(w_ref[...], x_ref[...].astype(BF16))
    yr_ref[...] = y[:n1]
    yi_ref[...] = y[n1:]


def _fft2_kernel(kb, scale, yr_ref, yi_ref, tc_ref, ts_ref, w2_ref, wc_ref, o_ref):
    n2 = FFT_N2
    w2 = w2_ref[...]
    wc = wc_ref[...]
    for j in range(kb):
        yr = yr_ref[j]
        yi = yi_ref[j]
        c = tc_ref[:, j:j + 1]
        s = ts_ref[:, j:j + 1]
        z = jnp.concatenate([yr * c + yi * s, yi * c - yr * s], axis=0).astype(BF16)
        gq = _dot(w2, z)
        for g in range(F_GROUPS):
            cols = slice(g * F_GROUP_DIM, (g + 1) * F_GROUP_DIM)
            gg = jnp.concatenate([gq[:n2, cols], gq[n2:, cols]], axis=1).astype(BF16)
            o_ref[:, j * F_WIDTH + g * F_GROUP_DIM:j * F_WIDTH + (g + 1) * F_GROUP_DIM] = _dot(gg, wc) * scale


def _fourier_mix_long(u):
    b, l, fw = u.shape
    n2 = FFT_N2
    n1 = l // n2
    cols = n2 * fw
    c1, s1 = _dft_cos_sin(n1)
    w1 = jnp.asarray(np.concatenate([c1, -s1], axis=0), BF16)
    c2, s2 = _dft_cos_sin(n2)
    w2 = jnp.asarray(np.block([[c2, s2], [-s2, c2]]), BF16)
    cc, sc = _dft_cos_sin(F_GROUP_DIM)
    wc = jnp.asarray(np.concatenate([cc, sc], axis=0), BF16)
    k1 = np.arange(n1, dtype=np.int64)
    nn = np.arange(n2, dtype=np.int64)
    ang = 2.0 * np.pi * ((nn[:, None] * k1[None, :]) % l).astype(np.float64) / l
    kb = SUBLANES
    tw_c = jnp.asarray(np.cos(ang).reshape(n2, n1 // kb, kb).transpose(1, 0, 2), F32)
    tw_s = jnp.asarray(np.sin(ang).reshape(n2, n1 // kb, kb).transpose(1, 0, 2), F32)
    tn = _pick(cols, (4096,))
    x2 = u.reshape(b, n1, cols)
    yr, yi = pl.pallas_call(
        _fft1_kernel,
        out_shape=[jax.ShapeDtypeStruct((b, n1, cols), F32)] * 2,
        grid=(b, cols // tn),
        in_specs=[pl.BlockSpec((None, n1, tn), lambda bi, j: (bi, 0, j)),
                  pl.BlockSpec((2 * n1, n1), lambda bi, j: (0, 0))],
        out_specs=[pl.BlockSpec((None, n1, tn), lambda bi, j: (bi, 0, j))] * 2,
        compiler_params=_cparams("parallel", "parallel"),
        name="fft_stage1",
    )(x2, w1)
    yr = yr.reshape(b, n1, n2, fw)
    yi = yi.reshape(b, n1, n2, fw)
    scale = float(1.0 / np.sqrt(float(l) * F_GROUP_DIM))
    yblk = pl.BlockSpec((None, kb, n2, fw), lambda bi, j: (bi, j, 0, 0))
    tblk = pl.BlockSpec((None, n2, kb), lambda bi, j: (j, 0, 0))
    out = pl.pallas_call(
        functools.partial(_fft2_kernel, kb, scale),
        out_shape=jax.ShapeDtypeStruct((b, n2, n1 * fw), F32),
        grid=(b, n1 // kb),
        in_specs=[yblk, yblk, tblk, tblk,
                  pl.BlockSpec((2 * n2, 2 * n2), lambda bi, j: (0, 0)),
                  pl.BlockSpec((2 * F_GROUP_DIM, F_GROUP_DIM), lambda bi, j: (0, 0))],
        out_specs=pl.BlockSpec((None, n2, kb * fw), lambda bi, j: (bi, 0, j)),
        compiler_params=_cparams("parallel", "parallel"),
        name="fft_stage2",
    )(yr, yi, tw_c, tw_s, w2, wc)
    return out.reshape(b, l, fw)


def _fftc_kernel(scale, u_ref, wl_ref, wc_ref, o_ref):
    fw = u_ref.shape[1]
    ab = _dot(u_ref[...].astype(BF16), wc_ref[...])
    st = jnp.concatenate([ab[:, :fw], ab[:, fw:]], axis=0).astype(BF16)
    o_ref[...] = _dot(wl_ref[...], st) * scale


def _fourier_mix_short(u):
    b, l, fw = u.shape
    cl, sl = _dft_cos_sin(l)
    wl = jnp.asarray(np.concatenate([cl, -sl], axis=1), BF16)
    cc, sc = _dft_cos_sin(F_GROUP_DIM)
    eye = np.eye(F_GROUPS)
    wc = jnp.asarray(np.concatenate([np.kron(eye, cc), np.kron(eye, sc)], axis=1), BF16)
    scale = float(1.0 / np.sqrt(float(l) * F_GROUP_DIM))
    return pl.pallas_call(
        functools.partial(_fftc_kernel, scale),
        out_shape=jax.ShapeDtypeStruct((b, l, fw), F32),
        grid=(b,),
        in_specs=[pl.BlockSpec((None, l, fw), lambda bi: (bi, 0, 0)),
                  pl.BlockSpec((l, 2 * l), lambda bi: (0, 0)),
                  pl.BlockSpec((fw, 2 * fw), lambda bi: (0, 0))],
        out_specs=pl.BlockSpec((None, l, fw), lambda bi: (bi, 0, 0)),
        compiler_params=_cparams("parallel"),
        name="fft_direct",
    )(u, wl, wc)


def _mixout_kernel(f_ref, a_ref, wf_ref, wa_ref, x_ref, gate_ref, nw_ref, o_ref):
    y = _dot(f_ref[...].astype(BF16), wf_ref[...]) + _dot(a_ref[...].astype(BF16), wa_ref[...])
    o_ref[...] = x_ref[...] + gate_ref[...] * _rms(y, nw_ref[...])


def _mixer_out(f, a, wf, wa, x, gate, nw):
    b, s, d = x.shape
    tm = _pick(s, (512, 256, 128))
    row = lambda w: pl.BlockSpec((None, tm, w), lambda bi, i: (bi, i, 0))
    const = lambda shp: pl.BlockSpec(shp, lambda bi, i: (0, 0))
    return pl.pallas_call(
        _mixout_kernel,
        out_shape=jax.ShapeDtypeStruct((b, s, d), F32),
        grid=(b, s // tm),
        in_specs=[row(f.shape[2]), row(a.shape[2]), const(wf.shape), const(wa.shape), row(d),
                  pl.BlockSpec((None, 1, d), lambda bi, i: (bi, 0, 0)), const((1, d))],
        out_specs=row(d),
        compiler_params=_cparams("parallel", "parallel"),
        name="mixer_out_residual",
    )(f, a, wf, wa, x, gate, nw)


def _ffn_kernel(n_e, n_f, use_comb, *refs):
    if use_comb:
        (x_ref, nw1_ref, sh_ref, sc_ref, wg_ref, wu_ref, wd_ref, gate_ref, nw2_ref, comb_ref,
         o_ref, h_scr, acc_scr) = refs
    else:
        (x_ref, nw1_ref, sh_ref, sc_ref, wg_ref, wu_ref, wd_ref, gate_ref, nw2_ref,
         o_ref, h_scr, acc_scr) = refs
    e = pl.program_id(2)
    f = pl.program_id(3)

    @pl.when((e == 0) & (f == 0))
    def _():
        h_scr[...] = _norm_mod(x_ref[...], nw1_ref[...], sh_ref[...], sc_ref[...]).astype(BF16)
        acc_scr[...] = jnp.zeros(acc_scr.shape, F32)

    h = h_scr[...]
    a = _silu(_dot(h, wg_ref[...])) * _dot(h, wu_ref[...])
    if use_comb:
        comb = comb_ref[...]
        lane = lax.broadcasted_iota(jnp.int32, comb.shape, 1)
        a = a * jnp.sum(jnp.where(lane == e, comb, 0.0), axis=-1, keepdims=True)
    acc_scr[...] += _dot(a.astype(BF16), wd_ref[...])

    @pl.when((e == n_e - 1) & (f == n_f - 1))
    def _():
        o_ref[...] = x_ref[...] + gate_ref[...] * _rms(acc_scr[...], nw2_ref[...])


def _ffn(x, nw1, shift, scale, w_gu, w_down, gate, nw2, comb=None):
    b, s, d = x.shape
    n_e, _, f2 = w_gu.shape
    ff = f2 // 2
    tm = _pick(s, (1024, 512, 256))
    tf = _pick(ff, (512, 256))
    n_f = ff // tf
    vec = pl.BlockSpec((None, 1, d), lambda bi, i, e, f: (bi, 0, 0))
    const = pl.BlockSpec((1, d), lambda bi, i, e, f: (0, 0))
    row = pl.BlockSpec((None, tm, d), lambda bi, i, e, f: (bi, i, 0))
    in_specs = [row, const, vec, vec,
                pl.BlockSpec((None, d, tf), lambda bi, i, e, f: (e, 0, f)),
                pl.BlockSpec((None, d, tf), lambda bi, i, e, f: (e, 0, f + n_f)),
                pl.BlockSpec((None, tf, d), lambda bi, i, e, f: (e, f, 0)),
                vec, const]
    args = [x, nw1, shift, scale, w_gu, w_gu, w_down, gate, nw2]
    if comb is not None:
        in_specs.append(pl.BlockSpec((None, tm, LANES), lambda bi, i, e, f: (bi, i, 0)))
        args.append(comb)
    return pl.pallas_call(
        functools.partial(_ffn_kernel, n_e, n_f, comb is not None),
        out_shape=jax.ShapeDtypeStruct((b, s, d), F32),
        grid=(b, s // tm, n_e, n_f),
        in_specs=in_specs,
        out_specs=row,
        scratch_shapes=[pltpu.VMEM((tm, d), BF16), pltpu.VMEM((tm, d), F32)],
        compiler_params=_cparams("parallel", "parallel", "arbitrary", "arbitrary"),
        name="swiglu_residual",
    )(*args)


def _router_kernel(x_ref, nw_ref, sh_ref, sc_ref, rw_ref, comb_ref):
    h = _norm_mod(x_ref[...], nw_ref[...], sh_ref[...], sc_ref[...])
    logits = _router_logits(h, rw_ref[...])
    lane = lax.broadcasted_iota(jnp.int32, logits.shape, 1)
    i1, i2, g1, g2 = _top2(logits, lane)
    comb_ref[...] = jnp.where(lane == i1, g1, 0.0) + jnp.where(lane == i2, g2, 0.0)


def _router(x, nw, shift, scale, rw_pad):
    b, s, d = x.shape
    tm = _pick(s, (512, 256, 128))
    vec = pl.BlockSpec((None, 1, d), lambda bi, i: (bi, 0, 0))
    return pl.pallas_call(
        _router_kernel,
        out_shape=jax.ShapeDtypeStruct((b, s, LANES), F32),
        grid=(b, s // tm),
        in_specs=[pl.BlockSpec((None, tm, d), lambda bi, i: (bi, i, 0)),
                  pl.BlockSpec((1, d), lambda bi, i: (0, 0)), vec, vec,
                  pl.BlockSpec((d, LANES), lambda bi, i: (0, 0))],
        out_specs=pl.BlockSpec((None, tm, LANES), lambda bi, i: (bi, i, 0)),
        compiler_params=_cparams("parallel", "parallel"),
        name="router_top2",
    )(x, nw, shift, scale, rw_pad)


def _top2(logits, lane):
    neg = -jnp.inf
    logits = jnp.where(lane < N_EXPERTS, logits, neg)
    m1 = jnp.max(logits, axis=-1, keepdims=True)
    i1 = jnp.min(jnp.where(logits == m1, lane, LANES), axis=-1, keepdims=True)
    rest = jnp.where(lane == i1, neg, logits)
    m2 = jnp.max(rest, axis=-1, keepdims=True)
    i2 = jnp.min(jnp.where(rest == m2, lane, LANES), axis=-1, keepdims=True)
    e2 = jnp.exp(m2 - m1)
    den = 1.0 + e2
    return i1, i2, 1.0 / den, e2 / den


def _router_logits(h, rw):
    h_hi, h_lo = _split_bf16(h, 2)
    w_hi, w_lo = _split_bf16(rw, 2)
    return _dot(h_hi, w_hi) + _dot(h_hi, w_lo) + _dot(h_lo, w_hi)


META_I1, META_I2, META_G1, META_G2, META_R1, META_R2 = range(6)


def _route_kernel(x_ref, nw_ref, sh_ref, sc_ref, rw_ref, meta_ref, cnt_ref, carry):
    @pl.when(pl.program_id(0) == 0)
    def _():
        carry[...] = jnp.zeros(carry.shape, F32)

    h = _norm_mod(x_ref[...], nw_ref[...], sh_ref[...], sc_ref[...])
    logits = _router_logits(h, rw_ref[...])
    tm = logits.shape[0]
    lane = lax.broadcasted_iota(jnp.int32, logits.shape, 1)
    i1, i2, g1, g2 = _top2(logits, lane)
    oh1 = lane == i1
    oh2 = lane == i2
    r = lax.broadcasted_iota(jnp.int32, (tm, tm), 0)
    c = lax.broadcasted_iota(jnp.int32, (tm, tm), 1)
    earlier = jnp.where(c < r, 1.0, 0.0).astype(BF16)
    f1 = jnp.where(oh1, 1.0, 0.0)
    f2 = jnp.where(oh2, 1.0, 0.0)
    before1 = _dot(earlier, f1.astype(BF16))
    before2 = _dot(earlier, f2.astype(BF16))
    c1 = jnp.sum(f1, axis=0, keepdims=True)
    c2 = jnp.sum(f2, axis=0, keepdims=True)
    base = carry[...]
    rank1 = jnp.sum(jnp.where(oh1, before1 + base, 0.0), axis=-1, keepdims=True)
    rank2 = jnp.sum(jnp.where(oh2, before2 + base + c1, 0.0), axis=-1, keepdims=True)
    carry[...] = base + c1 + c2
    cnt_ref[...] = carry[...]
    meta = jnp.zeros(logits.shape, F32)
    for slot, val in ((META_I1, i1.astype(F32)), (META_I2, i2.astype(F32)), (META_G1, g1), (META_G2, g2),
                      (META_R1, rank1), (META_R2, rank2)):
        meta = jnp.where(lane == slot, val, meta)
    meta_ref[...] = meta


def _dispatch_kernel(nt, pos_ref, prev_ref, x_ref, nw_ref, sh_ref, sc_ref, zeros_ref, xs_ref, hbuf, sem):
    del zeros_ref
    i = pl.program_id(0)
    tm = hbuf.shape[1]
    slot = i % 2

    def row_copy(sl, r, p):
        return pltpu.make_async_copy(hbuf.at[sl, pl.ds(r, 1)], xs_ref.at[pl.ds(p, 1)], sem.at[sl])

    def drain(sl, p_ref):
        def body(r, carry):
            row_copy(sl, r, p_ref[0, r]).wait()
            row_copy(sl, r, p_ref[0, tm + r]).wait()
            return carry
        lax.fori_loop(0, tm, body, 0)

    hbuf[slot] = _norm_mod(x_ref[...], nw_ref[...], sh_ref[...], sc_ref[...])

    def issue(r, carry):
        row_copy(slot, r, pos_ref[0, r]).start()
        row_copy(slot, r, pos_ref[0, tm + r]).start()
        return carry

    lax.fori_loop(0, tm, issue, 0)

    @pl.when(i > 0)
    def _():
        drain(1 - slot, prev_ref)

    @pl.when(i == nt - 1)
    def _():
        drain(slot, pos_ref)


def _expert_kernel(n_f, te_ref, na_ref, xs_ref, wg_ref, wu_ref, wd_ref, ys_ref, h_scr):
    i = pl.program_id(0)
    f = pl.program_id(1)

    @pl.when(i < na_ref[0])
    def _():
        @pl.when(f == 0)
        def _():
            h_scr[...] = xs_ref[...].astype(BF16)

        h = h_scr[...]
        a = _silu(_dot(h, wg_ref[...])) * _dot(h, wu_ref[...])
        part = _dot(a.astype(BF16), wd_ref[...])

        @pl.when(f == 0)
        def _():
            ys_ref[...] = part

        @pl.when(f > 0)
        def _():
            ys_ref[...] += part

    @pl.when((i >= na_ref[0]) & (f == 0))
    def _():
        ys_ref[...] = jnp.zeros(ys_ref.shape, F32)


def _combine_kernel(nt, pos_ref, next_ref, ys_ref, meta_ref, x_ref, gate_ref, nw_ref, o_ref, buf, sem):
    i = pl.program_id(0)
    tm = x_ref.shape[0]
    slot = i % 2

    def row_copy(sl, k, r, p):
        return pltpu.make_async_copy(ys_ref.at[pl.ds(p, 1)], buf.at[sl, k, pl.ds(r, 1)], sem.at[sl])

    def issue(sl, p_ref):
        def body(r, carry):
            row_copy(sl, 0, r, p_ref[0, r]).start()
            row_copy(sl, 1, r, p_ref[0, tm + r]).start()
            return carry
        lax.fori_loop(0, tm, body, 0)

    @pl.when(i == 0)
    def _():
        issue(slot, pos_ref)

    @pl.when(i + 1 < nt)
    def _():
        issue(1 - slot, next_ref)

    def drain(r, carry):
        row_copy(slot, 0, r, pos_ref[0, r]).wait()
        row_copy(slot, 1, r, pos_ref[0, tm + r]).wait()
        return carry

    lax.fori_loop(0, tm, drain, 0)
    meta = meta_ref[...]
    lane = lax.broadcasted_iota(jnp.int32, meta.shape, 1)
    g1 = jnp.sum(jnp.where(lane == META_G1, meta, 0.0), axis=-1, keepdims=True)
    g2 = jnp.sum(jnp.where(lane == META_G2, meta, 0.0), axis=-1, keepdims=True)
    y = g1 * buf[slot, 0] + g2 * buf[slot, 1]
    o_ref[...] = x_ref[...] + gate_ref[...] * _rms(y, nw_ref[...])


def _moe_sparse(x, nw1, shift, scale, rw_pad, w_gu, w_down, gate, nw2):
    b, s, d = x.shape
    n = b * s
    tm = _pick(s, (512, 256, 128))
    tpb = s // tm
    nt = n // tm
    n_e, _, f2 = w_gu.shape
    ff = f2 // 2
    tf = _pick(ff, (512, 256))
    n_f = ff // tf
    te_rows = EXPERT_TILE
    n_rows = 2 * n + n_e * te_rows
    nte = n_rows // te_rows

    xrow = pl.BlockSpec((None, tm, d), lambda i: (i // tpb, i % tpb, 0))
    vec = pl.BlockSpec((None, 1, d), lambda i: (i // tpb, 0, 0))
    const = lambda shp: pl.BlockSpec(shp, lambda i: (0, 0))
    mrow = pl.BlockSpec((None, tm, LANES), lambda i: (i // tpb, i % tpb, 0))
    meta, cnt = pl.pallas_call(
        _route_kernel,
        out_shape=[jax.ShapeDtypeStruct((b, s, LANES), F32), jax.ShapeDtypeStruct((1, LANES), F32)],
        grid=(nt,),
        in_specs=[xrow, const((1, d)), vec, vec, const((d, LANES))],
        out_specs=[mrow, const((1, LANES))],
        scratch_shapes=[pltpu.VMEM((1, LANES), F32)],
        compiler_params=_cparams("arbitrary"),
        name="route_top2_rank",
    )(x, nw1, shift, scale, rw_pad)

    m2 = meta.reshape(n, LANES)
    i1 = m2[:, META_I1].astype(jnp.int32)
    i2 = m2[:, META_I2].astype(jnp.int32)
    counts = cnt[0, :n_e].astype(jnp.int32)
    padded = (counts + te_rows - 1) // te_rows * te_rows
    ends = jnp.cumsum(padded)
    starts = ends - padded
    pos1 = jnp.take(starts, i1) + m2[:, META_R1].astype(jnp.int32)
    pos2 = jnp.take(starts, i2) + m2[:, META_R2].astype(jnp.int32)
    pos = jnp.concatenate([pos1.reshape(nt, 1, tm), pos2.reshape(nt, 1, tm)], axis=2)
    n_active = (ends[-1] // te_rows).astype(jnp.int32).reshape(1)
    tile_idx = jnp.minimum(jnp.arange(nte, dtype=jnp.int32), n_active[0] - 1)
    tile_expert = jnp.minimum(jnp.sum((tile_idx * te_rows)[:, None] >= ends[None, :], axis=1), n_e - 1).astype(jnp.int32)

    def pos_block(index):
        return pl.BlockSpec((None, 1, 2 * tm), lambda i: (index(i), 0, 0), memory_space=pltpu.SMEM)

    pos_spec = pos_block(lambda i: i)
    any_spec = pl.BlockSpec(memory_space=pl.ANY)
    xs = pl.pallas_call(
        functools.partial(_dispatch_kernel, nt),
        out_shape=jax.ShapeDtypeStruct((n_rows, d), F32),
        grid=(nt,),
        in_specs=[pos_spec, pos_block(lambda i: jnp.maximum(i - 1, 0)), xrow, const((1, d)), vec, vec, any_spec],
        out_specs=any_spec,
        scratch_shapes=[pltpu.VMEM((2, tm, d), F32), pltpu.SemaphoreType.DMA((2,))],
        input_output_aliases={6: 0},
        compiler_params=_cparams("arbitrary"),
        name="moe_dispatch",
    )(pos, pos, x, nw1, shift, scale, jnp.zeros((n_rows, d), F32))

    def live(i, na):
        return jnp.minimum(i, na[0] - 1)

    def fidx(i, f, na):
        return jnp.where(i < na[0], f, n_f - 1)

    ys = pl.pallas_call(
        functools.partial(_expert_kernel, n_f),
        out_shape=jax.ShapeDtypeStruct((n_rows, d), F32),
        grid_spec=pltpu.PrefetchScalarGridSpec(
            num_scalar_prefetch=2,
            grid=(nte, n_f),
            in_specs=[pl.BlockSpec((te_rows, d), lambda i, f, te, na: (live(i, na), 0)),
                      pl.BlockSpec((None, d, tf), lambda i, f, te, na: (te[i], 0, fidx(i, f, na))),
                      pl.BlockSpec((None, d, tf), lambda i, f, te, na: (te[i], 0, fidx(i, f, na) + n_f)),
                      pl.BlockSpec((None, tf, d), lambda i, f, te, na: (te[i], fidx(i, f, na), 0))],
            out_specs=pl.BlockSpec((te_rows, d), lambda i, f, te, na: (i, 0)),
            scratch_shapes=[pltpu.VMEM((te_rows, d), BF16)]),
        compiler_params=_cparams("arbitrary", "arbitrary"),
        name="moe_experts",
    )(tile_expert, n_active, xs, w_gu, w_gu, w_down)

    return pl.pallas_call(
        functools.partial(_combine_kernel, nt),
        out_shape=jax.ShapeDtypeStruct((b, s, d), F32),
        grid=(nt,),
        in_specs=[pos_spec, pos_block(lambda i: jnp.minimum(i + 1, nt - 1)), any_spec, mrow, xrow, vec,
                  const((1, d))],
        out_specs=xrow,
        scratch_shapes=[pltpu.VMEM((2, 2, tm, d), F32), pltpu.SemaphoreType.DMA((2,))],
        compiler_params=_cparams("arbitrary"),
        name="moe_combine",
    )(pos, pos, ys, meta, x, gate, nw2)


def _conv_kernel(nt, prev_ref, cur_ref, next_ref, w_ref, b_ref, xs_ref, bm_ref, cm_ref, buf):
    i = pl.program_id(1)
    tm = cur_ref.shape[0]
    halo = SUBLANES
    buf[halo:halo + tm, :] = cur_ref[...]
    buf[0:halo, :] = jnp.where(i > 0, prev_ref[...], 0.0)
    buf[halo + tm:2 * halo + tm, :] = jnp.where(i < nt - 1, next_ref[...], 0.0)
    ext = buf[...]
    n_ext = ext.shape[0]
    down = lambda v: pltpu.roll(v, 1, 0)
    up = lambda v: pltpu.roll(v, n_ext - 1, 0)
    w = [w_ref[j:j + 1, :] for j in range(D_CONV)]
    acc = (w[2] * ext + down(w[1] * ext + down(w[0] * ext)) + up(w[3] * ext + up(w[4] * ext)))
    y = _silu(acc[halo:halo + tm, :] + b_ref[...])
    d_inner = xs_ref.shape[1]
    gn = bm_ref.shape[1]
    xs_ref[...] = y[:, :d_inner]
    bm_ref[...] = y[:, d_inner:d_inner + gn]
    cm_ref[...] = y[:, d_inner + gn:]


def _conv_silu(xbc, conv_w, conv_b, d_inner):
    b, s, c = xbc.shape
    gn = (c - d_inner) // 2
    tm = _pick(s, (512, 256, 128))
    nt = s // tm
    hb = tm // SUBLANES
    last = s // SUBLANES - 1
    row = lambda w: pl.BlockSpec((None, tm, w), lambda bi, i: (bi, i, 0))
    return pl.pallas_call(
        functools.partial(_conv_kernel, nt),
        out_shape=[jax.ShapeDtypeStruct((b, s, d_inner), F32),
                   jax.ShapeDtypeStruct((b, s, gn), F32), jax.ShapeDtypeStruct((b, s, gn), F32)],
        grid=(b, nt),
        in_specs=[pl.BlockSpec((None, SUBLANES, c), lambda bi, i: (bi, jnp.maximum(i * hb - 1, 0), 0)),
                  row(c),
                  pl.BlockSpec((None, SUBLANES, c), lambda bi, i: (bi, jnp.minimum((i + 1) * hb, last), 0)),
                  pl.BlockSpec((D_CONV, c), lambda bi, i: (0, 0)),
                  pl.BlockSpec((1, c), lambda bi, i: (0, 0))],
        out_specs=[row(d_inner), row(gn), row(gn)],
        scratch_shapes=[pltpu.VMEM((tm + 2 * SUBLANES, c), F32)],
        compiler_params=_cparams("parallel", "parallel"),
        name="dwconv_silu",
    )(xbc, xbc, xbc, conv_w, conv_b)


def _softplus(x):
    return jnp.maximum(x, 0.0) + jnp.log(1.0 + jnp.exp(-jnp.abs(x)))


def _ssd_kernel(direction, nc, xs_ref, bm_ref, cm_ref, dt_ref, dtb_ref, a_ref, h0_ref, e_ref,
                y_ref, hT_ref, st_scr):
    c = pl.program_id(1)

    @pl.when(c == 0)
    def _():
        st_scr[...] = h0_ref[...]

    q = CHUNK
    fwd = direction == 0
    dt = _softplus(dt_ref[...] + dtb_ref[...])
    d_a = dt * a_ref[...]
    row = lax.broadcasted_iota(jnp.int32, (q, q), 0)
    col = lax.broadcasted_iota(jnp.int32, (q, q), 1)
    causal = (row >= col) if fwd else (row <= col)
    tri = jnp.where(causal, 1.0, 0.0).astype(BF16)
    a_cum = _dot_split_rhs(tri, d_a, 3)
    a_cum_t = a_cum.T
    expand = e_ref[...]
    dt_x = _dot_split_lhs(dt, expand, 2)
    a_x = _dot_split_lhs(a_cum, expand, 2)
    last = q - 1 if fwd else 0
    a_last = a_x[last:last + 1, :]
    xdt = xs_ref[...] * dt_x
    xdt_b = xdt.astype(BF16)
    w_state = (jnp.exp(a_last - a_x) * xdt).astype(BF16)
    exp_a = jnp.exp(a_x)
    exp_last = jnp.exp(a_last)
    low = lax.broadcasted_iota(jnp.int32, (q, LANES), 1) < SSM_HEAD_DIM
    gw = HEADS_PER_GROUP * SSM_HEAD_DIM
    for g in range(N_GROUPS):
        gs = slice(g * gw, (g + 1) * gw)
        b_g = bm_ref[:, g * D_STATE:(g + 1) * D_STATE]
        c_b = cm_ref[:, g * D_STATE:(g + 1) * D_STATE].astype(BF16)
        cb = lax.dot_general(c_b, b_g.astype(BF16), (((1,), (1,)), ((), ())), preferred_element_type=F32)
        st_g = st_scr[:, gs]
        y_off = _dot(c_b, st_g.astype(BF16)) * exp_a[:, gs]
        s_new = _dot(b_g.T.astype(BF16), w_state[:, gs])
        for pr in range(HEADS_PER_GROUP // 2):
            ys = []
            for j in range(2):
                hl = direction * N_SSM_HEADS + g * HEADS_PER_GROUP + 2 * pr + j
                seg = a_cum[:, hl:hl + 1] - a_cum_t[hl:hl + 1, :]
                dec = jnp.exp(jnp.where(causal, seg, -jnp.inf))
                ys.append(_dot((cb * dec).astype(BF16), xdt_b[:, g * gw + pr * LANES:g * gw + (pr + 1) * LANES]))
            y_ref[:, g * gw + pr * LANES:g * gw + (pr + 1) * LANES] = (
                jnp.where(low, ys[0], ys[1]) + y_off[:, pr * LANES:(pr + 1) * LANES]).astype(y_ref.dtype)
        st_scr[:, gs] = exp_last[:, gs] * st_g + s_new

    @pl.when(c == nc - 1)
    def _():
        hT_ref[...] = st_scr[...]


def _ssd_scan(direction, xs, bm, cm, dt_raw, dt_bias, a_neg, h0):
    b, s, d_inner = xs.shape
    gn = bm.shape[2]
    nc = s // CHUNK
    heads = np.arange(d_inner) // SSM_HEAD_DIM + direction * N_SSM_HEADS
    expand = jnp.asarray(np.tile(np.arange(LANES)[:, None] == heads[None, :], (2, 1)), BF16)
    if direction == 0:
        cidx = lambda bi, c: (bi, c, 0)
    else:
        cidx = lambda bi, c: (bi, nc - 1 - c, 0)
    const = lambda bi, c: (0, 0)
    st_spec = pl.BlockSpec((None, D_STATE, d_inner), lambda bi, c: (bi, 0, 0))
    return pl.pallas_call(
        functools.partial(_ssd_kernel, direction, nc),
        out_shape=[jax.ShapeDtypeStruct((b, s, d_inner), BF16),
                   jax.ShapeDtypeStruct((b, D_STATE, d_inner), F32)],
        grid=(b, nc),
        in_specs=[pl.BlockSpec((None, CHUNK, d_inner), cidx),
                  pl.BlockSpec((None, CHUNK, gn), cidx), pl.BlockSpec((None, CHUNK, gn), cidx),
                  pl.BlockSpec((None, CHUNK, LANES), cidx),
                  pl.BlockSpec((1, LANES), const), pl.BlockSpec((1, LANES), const),
                  st_spec, pl.BlockSpec((2 * LANES, d_inner), const)],
        out_specs=[pl.BlockSpec((None, CHUNK, d_inner), cidx), st_spec],
        scratch_shapes=[pltpu.VMEM((D_STATE, d_inner), F32)],
        compiler_params=_cparams("parallel", "arbitrary"),
        name="ssd_scan",
    )(xs, bm, cm, dt_raw, dt_bias, a_neg, h0, expand)


def _ssdout_kernel(yf_ref, yb_ref, xs_ref, z_ref, dsk_ref, gnw_ref, w_ref, x_ref, gate_ref, nw_ref, o_ref):
    y = yf_ref[...].astype(F32) + yb_ref[...].astype(F32) + dsk_ref[...] * xs_ref[...]
    gy = y * _silu(z_ref[...].astype(F32))
    gnw = gnw_ref[...]
    gw = gy.shape[1] // N_GROUPS
    parts = []
    for g in range(N_GROUPS):
        parts.append(_rms(gy[:, g * gw:(g + 1) * gw], gnw[:, g * gw:(g + 1) * gw]).astype(BF16))
    out = _dot(jnp.concatenate(parts, axis=1), w_ref[...])
    o_ref[...] = x_ref[...] + gate_ref[...] * _rms(out, nw_ref[...])


def _ssd_out(yf, yb, xs, z, d_skip, gnw, w_out, x, gate, nw):
    b, s, d = x.shape
    d_inner = xs.shape[2]
    tm = _pick(s, (256, 128))
    row = lambda w: pl.BlockSpec((None, tm, w), lambda bi, i: (bi, i, 0))
    const = lambda shp: pl.BlockSpec(shp, lambda bi, i: (0, 0))
    return pl.pallas_call(
        _ssdout_kernel,
        out_shape=jax.ShapeDtypeStruct((b, s, d), F32),
        grid=(b, s // tm),
        in_specs=[row(d_inner), row(d_inner), row(d_inner), row(d_inner),
                  const((1, d_inner)), const((1, d_inner)), const(w_out.shape), row(d),
                  pl.BlockSpec((None, 1, d), lambda bi, i: (bi, 0, 0)), const((1, d))],
        out_specs=row(d),
        compiler_params=_cparams("parallel", "parallel"),
        name="ssd_gate_out_residual",
    )(yf, yb, xs, z, d_skip, gnw, w_out, x, gate, nw)


def _rope_tables(seq):
    rows = seq // GRID_W
    row = jnp.repeat(jnp.arange(rows, dtype=F32), GRID_W)
    col = jnp.tile(jnp.arange(GRID_W, dtype=F32), rows)
    freqs = ROPE_THETA ** (-jnp.arange(0, AXIS_ROT_DIM, 2, dtype=F32) / AXIS_ROT_DIM)
    ang = jnp.concatenate([row[:, None] * freqs, col[:, None] * freqs], axis=-1)
    cos = jnp.repeat(jnp.cos(ang), 2, axis=-1)
    sin = jnp.repeat(jnp.sin(ang), 2, axis=-1) * jnp.tile(jnp.asarray([-1.0, 1.0], F32), HEAD_DIM // 2)
    return jnp.tile(cos, (1, LANES // HEAD_DIM)), jnp.tile(sin, (1, LANES // HEAD_DIM))


def _mod_vectors(mods_i, batch):
    d = mods_i.shape[1] // 6
    mx = [mods_i[:batch, k * d:(k + 1) * d].reshape(batch, 1, d) for k in range(6)]
    mc = [jnp.broadcast_to(mods_i[batch, k * d:(k + 1) * d].reshape(1, 1, d), (batch, 1, d)) for k in range(6)]
    return mx, mc


def _even_mixer(x, ctx, nw0, nw1, mx, mc, w_in, qnw, knw, w_out, rope, need_ctx):
    w_u = w_in[:, :F_WIDTH].astype(BF16)
    w_qkv = w_in[:, F_WIDTH:].astype(BF16)
    qw = jnp.tile(qnw.reshape(1, HEAD_DIM), (1, LANES // HEAD_DIM))
    kw = jnp.tile(knw.reshape(1, HEAD_DIM), (1, LANES // HEAD_DIM))
    w_f = w_out[:F_WIDTH].astype(BF16)
    d = w_out.shape[1]
    w_a = (w_out[F_WIDTH:].reshape(N_KV_HEADS, GQ, HEAD_DIM, d).transpose(1, 0, 2, 3)
           .reshape(Q_WIDTH, d).astype(BF16))
    lc = ctx.shape[1]
    ones = jnp.ones((lc, LANES), F32)
    zeros = jnp.zeros((lc, LANES), F32)

    u_c, qkv_c = _norm_mod_linear(ctx, nw0, mc[0], mc[1], [w_u, w_qkv], [F32, F32])
    q_c, k_c, v_c = _qk_prepare(qkv_c, ones, zeros, qw, kw)
    u_x, qkv_x = _norm_mod_linear(x, nw0, mx[0], mx[1], [w_u, w_qkv], [F32, F32])
    q_x, k_x, v_x = _qk_prepare(qkv_x, rope[0], rope[1], qw, kw)
    k_all = jnp.concatenate([k_c, k_x], axis=1)
    v_all = jnp.concatenate([v_c, v_x], axis=1)
    a_x = _attention(q_x, k_all, v_all)
    f_x = _fourier_mix_long(u_x)
    x_new = _mixer_out(f_x, a_x, w_f, w_a, x, mx[2], nw1)
    if not need_ctx:
        return x_new, ctx
    a_c = _attention(q_c, k_c, v_c)
    f_c = _fourier_mix_short(u_c)
    return x_new, _mixer_out(f_c, a_c, w_f, w_a, ctx, mc[2], nw1)


def _ssd_branch(h_in, nw0, shift, scale, w_z, w_xbc, w_dt, conv_w, conv_b, dt_bias, a_neg, h0_f, h0_b):
    z, xbc, dt_raw = _norm_mod_linear(h_in, nw0, shift, scale, [w_z, w_xbc, w_dt], [BF16, F32, F32], tm_max=256)
    xs, bm, cm = _conv_silu(xbc, conv_w, conv_b, w_z.shape[1])
    y_f, s_f = _ssd_scan(0, xs, bm, cm, dt_raw, dt_bias, a_neg, h0_f)
    y_b, s_b = _ssd_scan(1, xs, bm, cm, dt_raw, dt_bias, a_neg, h0_b)
    return z, xs, y_f, y_b, s_f, s_b


def _odd_mixer(x, ctx, nw0, nw1, mx, mc, w_in, conv_w, conv_b, dt_bias, a_log, d_skip, gnw, w_out, need_ctx):
    d_inner = w_out.shape[0]
    conv_dim = conv_w.shape[1]
    w_z = w_in[:, :d_inner].astype(BF16)
    w_xbc = w_in[:, d_inner:d_inner + conv_dim].astype(BF16)
    n_dt = w_in.shape[1] - d_inner - conv_dim
    w_dt = jnp.pad(w_in[:, d_inner + conv_dim:], ((0, 0), (0, LANES - n_dt))).astype(BF16)
    dtb = jnp.pad(dt_bias.reshape(1, n_dt), ((0, 0), (0, LANES - n_dt)))
    a_neg = jnp.pad(-jnp.exp(a_log.astype(F32)).reshape(1, n_dt), ((0, 0), (0, LANES - n_dt)))
    cb = conv_b.reshape(1, conv_dim)
    dsk = jnp.repeat(d_skip, SSM_HEAD_DIM).reshape(1, d_inner)
    gw = gnw.reshape(1, d_inner)
    w_o = w_out.astype(BF16)
    b = x.shape[0]
    zeros = jnp.zeros((b, D_STATE, d_inner), F32)

    z_c, xs_c, yf_c, yb_c, s_f, s_b = _ssd_branch(ctx, nw0, mc[0], mc[1], w_z, w_xbc, w_dt, conv_w, cb,
                                                  dtb, a_neg, zeros, zeros)
    z_x, xs_x, yf_x, yb_x, _, _ = _ssd_branch(x, nw0, mx[0], mx[1], w_z, w_xbc, w_dt, conv_w, cb,
                                              dtb, a_neg, s_f, s_b)
    x_new = _ssd_out(yf_x, yb_x, xs_x, z_x, dsk, gw, w_o, x, mx[2], nw1)
    if not need_ctx:
        return x_new, ctx
    return x_new, _ssd_out(yf_c, yb_c, xs_c, z_c, dsk, gw, w_o, ctx, mc[2], nw1)


def kernel(x, c, ctx, c_ctx, ada_w, ada_b, norm_w, mix_in_w, q_norm_w, k_norm_w, mix_out_w, ffn_gu_w, ffn_down_w, ssm_in_w, ssm_conv_w, ssm_conv_b, ssm_dt_bias, ssm_a_log, ssm_d, ssm_norm_w, ssm_out_w, router_w, moe_gu_w, moe_down_w):
    batch, seq, d = x.shape
    depth = ada_w.shape[0]
    rope = _rope_tables(seq)
    cvec = jnp.zeros((SUBLANES, d), F32).at[:batch].set(c).at[batch].set(c_ctx)
    mods = _modulation(cvec, ada_w, ada_b)
    for i in range(depth):
        last = i == depth - 1
        j = i // 2
        mx, mc = _mod_vectors(mods[i], batch)
        nw = [norm_w[i, k].reshape(1, d) for k in range(4)]
        if i % 2 == 0:
            x, ctx_mid = _even_mixer(x, ctx, nw[0], nw[1], mx, mc, mix_in_w[j], q_norm_w[j], k_norm_w[j],
                                     mix_out_w[j], rope, not last)
            w_gu = ffn_gu_w[j].astype(BF16)[None]
            w_dn = ffn_down_w[j].astype(BF16)[None]
            x = _ffn(x, nw[2], mx[3], mx[4], w_gu, w_dn, mx[5], nw[3])
            if not last:
                ctx = _ffn(ctx_mid, nw[2], mc[3], mc[4], w_gu, w_dn, mc[5], nw[3])
        else:
            x, ctx_mid = _odd_mixer(x, ctx, nw[0], nw[1], mx, mc, ssm_in_w[j], ssm_conv_w[j], ssm_conv_b[j],
                                    ssm_dt_bias[j], ssm_a_log[j], ssm_d[j], ssm_norm_w[j], ssm_out_w[j],
                                    not last)
            rw = jnp.pad(router_w[j], ((0, 0), (0, LANES - N_EXPERTS)))
            w_gu = moe_gu_w[j].astype(BF16)
            w_dn = moe_down_w[j].astype(BF16)
            x = _moe_sparse(x, nw[2], mx[3], mx[4], rw, w_gu, w_dn, mx[5], nw[3])
            if not last:
                comb_c = _router(ctx_mid, nw[2], mc[3], mc[4], rw)
                ctx = _ffn(ctx_mid, nw[2], mc[3], mc[4], w_gu, w_dn, mc[5], nw[3], comb_c)
    return x
```

```python
import functools

import numpy as np
import jax
import jax.numpy as jnp
from jax import lax
from jax.experimental import pallas as pl
from jax.experimental.pallas import tpu as pltpu

F32 = jnp.float32
BF16 = jnp.bfloat16

EPS = 1e-6
GRID_W = 64
HEAD_DIM = 64
N_HEADS = 8
N_KV_HEADS = 2
GQ = N_HEADS // N_KV_HEADS
F_GROUPS = 4
F_GROUP_DIM = 128
F_WIDTH = F_GROUPS * F_GROUP_DIM
Q_WIDTH = N_HEADS * HEAD_DIM
KV_WIDTH = N_KV_HEADS * HEAD_DIM
ROPE_THETA = 10000.0
AXIS_ROT_DIM = HEAD_DIM // 2
ATTN_SCALE = HEAD_DIM ** -0.5
LOG2_E = 1.4426950408889634
SSM_HEAD_DIM = 64
N_SSM_HEADS = 32
N_GROUPS = 4
HEADS_PER_GROUP = N_SSM_HEADS // N_GROUPS
D_STATE = 128
D_CONV = 5
CONV_PAD = D_CONV // 2
CHUNK = 128
N_EXPERTS = 8
EXPERT_TILE = 1024
LANES = 128
SUBLANES = 8
FFT_N2 = 128
VMEM_LIMIT = 56 * 1024 * 1024


def _cparams(*sem):
    return pltpu.CompilerParams(dimension_semantics=sem, vmem_limit_bytes=VMEM_LIMIT)


def _pick(n, candidates):
    for c in candidates:
        if n % c == 0:
            return c
    return n


def _sigmoid(x):
    return 1.0 / (1.0 + jnp.exp(-x))


def _silu(x):
    return x * _sigmoid(x)


def _rms(x, w):
    return x * lax.rsqrt(jnp.mean(x * x, axis=-1, keepdims=True) + EPS) * w


def _norm_mod(x, nw, shift, scale):
    return _rms(x, nw) * (1.0 + scale) + shift


def _dot(a, b):
    return jnp.dot(a, b, preferred_element_type=F32)


def _split_bf16(x, parts):
    out = []
    r = x
    for _ in range(parts):
        p = r.astype(BF16)
        out.append(p)
        r = r - p.astype(F32)
    return out


def _dot_split_lhs(x, m_stacked, parts):
    return _dot(jnp.concatenate(_split_bf16(x, parts), axis=1), m_stacked)


def _dot_split_rhs(m, x, parts):
    acc = None
    for p in _split_bf16(x, parts):
        t = _dot(m, p)
        acc = t if acc is None else acc + t
    return acc


def _mod_kernel(c_ref, w_ref, b_ref, o_ref):
    s = _silu(c_ref[...])
    o_ref[...] = jnp.dot(s, w_ref[...], preferred_element_type=F32,
                         precision=lax.Precision.HIGHEST) + b_ref[...]


def _modulation(cvec, ada_w, ada_b):
    depth, d, n = ada_w.shape
    tn = _pick(n, (1536, 1024, 512))
    rows = cvec.shape[0]
    return pl.pallas_call(
        _mod_kernel,
        out_shape=jax.ShapeDtypeStruct((depth, rows, n), F32),
        grid=(depth, n // tn),
        in_specs=[pl.BlockSpec((rows, d), lambda i, j: (0, 0)),
                  pl.BlockSpec((None, d, tn), lambda i, j: (i, 0, j)),
                  pl.BlockSpec((None, 1, tn), lambda i, j: (i, 0, j))],
        out_specs=pl.BlockSpec((None, rows, tn), lambda i, j: (i, 0, j)),
        compiler_params=_cparams("parallel", "parallel"),
        name="adaln_modulation",
    )(cvec, ada_w, ada_b.reshape(depth, 1, n))


def _nml_kernel(n_w, x_ref, nw_ref, sh_ref, sc_ref, *refs):
    h = _norm_mod(x_ref[...], nw_ref[...], sh_ref[...], sc_ref[...]).astype(BF16)
    for w_ref, o_ref in zip(refs[:n_w], refs[n_w:]):
        o_ref[...] = _dot(h, w_ref[...]).astype(o_ref.dtype)


def _norm_mod_linear(x, nw, shift, scale, weights, out_dtypes, tm_max=512):
    b, s, d = x.shape
    tm = _pick(s, (tm_max, 256, 128))
    n_w = len(weights)
    vec = pl.BlockSpec((None, 1, d), lambda bi, i: (bi, 0, 0))
    in_specs = [pl.BlockSpec((None, tm, d), lambda bi, i: (bi, i, 0)),
                pl.BlockSpec((1, d), lambda bi, i: (0, 0)), vec, vec]
    in_specs += [pl.BlockSpec(w.shape, lambda bi, i: (0, 0)) for w in weights]
    out_shape = [jax.ShapeDtypeStruct((b, s, w.shape[1]), dt) for w, dt in zip(weights, out_dtypes)]
    out_specs = [pl.BlockSpec((None, tm, w.shape[1]), lambda bi, i: (bi, i, 0)) for w in weights]
    return pl.pallas_call(
        functools.partial(_nml_kernel, n_w),
        out_shape=out_shape,
        grid=(b, s // tm),
        in_specs=in_specs,
        out_specs=out_specs,
        compiler_params=_cparams("parallel", "parallel"),
        name="norm_mod_linear",
    )(x, nw, shift, scale, *weights)


def _qkprep_kernel(qkv_ref, cos_ref, sin_ref, qw_ref, kw_ref, bd_ref, q_ref, k_ref, v_ref):
    tm = qkv_ref.shape[0]
    lane = lax.broadcasted_iota(jnp.int32, (tm, LANES), 1)
    even = (lane % 2) == 0
    low = lane < HEAD_DIM
    cos = cos_ref[...]
    sin = sin_ref[...]
    bd = bd_ref[...]

    def prep(xb, w):
        ms = _dot_split_lhs(xb * xb, bd, 2)
        y = xb * lax.rsqrt(ms + EPS) * w
        swapped = jnp.where(even, pltpu.roll(y, LANES - 1, 1), pltpu.roll(y, 1, 1))
        return y * cos + swapped * sin

    qw = qw_ref[...]
    for cblk in range(Q_WIDTH // LANES):
        r = prep(qkv_ref[:, cblk * LANES:(cblk + 1) * LANES], qw) * (ATTN_SCALE * LOG2_E)
        r_sw = pltpu.roll(r, HEAD_DIM, 1)
        kvh = (2 * cblk) // GQ
        keep = low if kvh == 0 else jnp.logical_not(low)
        for j in range(2):
            src = r if j == kvh else r_sw
            q_ref[2 * cblk + j] = jnp.where(keep, src, 0.0).astype(q_ref.dtype)
    k_ref[...] = prep(qkv_ref[:, Q_WIDTH:Q_WIDTH + KV_WIDTH], kw_ref[...]).astype(k_ref.dtype)
    v_ref[...] = qkv_ref[:, Q_WIDTH + KV_WIDTH:Q_WIDTH + 2 * KV_WIDTH].astype(v_ref.dtype)


def _qk_prepare(qkv, cos_t, sin_t, qw, kw):
    b, s, _ = qkv.shape
    tm = _pick(s, (512, 256, 128))
    bd = np.kron(np.eye(2, dtype=np.float32), np.full((HEAD_DIM, HEAD_DIM), 1.0 / HEAD_DIM, np.float32))
    bd = np.tile(bd, (2, 1))
    row = lambda bi, i: (i, 0)
    const = lambda bi, i: (0, 0)
    return pl.pallas_call(
        _qkprep_kernel,
        out_shape=[jax.ShapeDtypeStruct((b, N_HEADS, s, LANES), BF16),
                   jax.ShapeDtypeStruct((b, s, KV_WIDTH), BF16),
                   jax.ShapeDtypeStruct((b, s, KV_WIDTH), BF16)],
        grid=(b, s // tm),
        in_specs=[pl.BlockSpec((None, tm, qkv.shape[2]), lambda bi, i: (bi, i, 0)),
                  pl.BlockSpec((tm, LANES), row), pl.BlockSpec((tm, LANES), row),
                  pl.BlockSpec((1, LANES), const), pl.BlockSpec((1, LANES), const),
                  pl.BlockSpec((2 * LANES, LANES), const)],
        out_specs=[pl.BlockSpec((None, N_HEADS, tm, LANES), lambda bi, i: (bi, 0, i, 0)),
                   pl.BlockSpec((None, tm, KV_WIDTH), lambda bi, i: (bi, i, 0)),
                   pl.BlockSpec((None, tm, KV_WIDTH), lambda bi, i: (bi, i, 0))],
        compiler_params=_cparams("parallel", "parallel"),
        name="qk_norm_rope",
    )(qkv, cos_t, sin_t, qw, kw, jnp.asarray(bd, BF16))


def _attn_kernel(nk, q_ref, k_ref, v_ref, o_ref, m_scr, l_scr, acc_scr):
    kv = pl.program_id(2)

    @pl.when(kv == 0)
    def _():
        m_scr[...] = jnp.full(m_scr.shape, -jnp.inf, F32)
        l_scr[...] = jnp.zeros(l_scr.shape, F32)
        acc_scr[...] = jnp.zeros(acc_scr.shape, F32)

    k = k_ref[...]
    v = v_ref[...]
    n_t = k.shape[0] // LANES

    def head(hd, carry):
        s = lax.dot_general(q_ref[hd], k, (((1,), (1,)), ((), ())), preferred_element_type=F32)
        tiles = [s[:, t * LANES:(t + 1) * LANES] for t in range(n_t)]
        smax = tiles[0]
        for t in range(1, n_t):
            smax = jnp.maximum(smax, tiles[t])
        m_prev = m_scr[hd]
        m_new = jnp.maximum(m_prev, jnp.max(smax, axis=-1, keepdims=True))
        alpha = jnp.exp2(m_prev - m_new)
        ps = []
        lsum = None
        for t in range(n_t):
            p_t = jnp.exp2(tiles[t] - m_new)
            lsum = p_t if lsum is None else lsum + p_t
            ps.append(p_t.astype(BF16))
        l_scr[hd] = alpha * l_scr[hd] + lsum
        acc_scr[hd] = alpha * acc_scr[hd] + _dot(jnp.concatenate(ps, axis=1), v)
        m_scr[hd] = m_new
        return carry

    lax.fori_loop(0, N_HEADS, head, 0, unroll=8)

    @pl.when(kv == nk - 1)
    def _():
        tq = o_ref.shape[0]
        low = lax.broadcasted_iota(jnp.int32, (tq, LANES), 1) < HEAD_DIM
        for g in range(GQ):
            o0 = acc_scr[g] / jnp.sum(l_scr[g], axis=-1, keepdims=True)
            o1 = acc_scr[GQ + g] / jnp.sum(l_scr[GQ + g], axis=-1, keepdims=True)
            o_ref[:, g * LANES:(g + 1) * LANES] = jnp.where(low, o0, o1).astype(o_ref.dtype)


def _attention(q, k, v):
    b, _, s, _ = q.shape
    skv = k.shape[1]
    tq = _pick(s, (1024, 512, 256, 128))
    tk = _pick(skv, (1280, 640, 256, 128))
    nk = skv // tk
    return pl.pallas_call(
        functools.partial(_attn_kernel, nk),
        out_shape=jax.ShapeDtypeStruct((b, s, Q_WIDTH), BF16),
        grid=(b, s // tq, nk),
        in_specs=[pl.BlockSpec((None, N_HEADS, tq, LANES), lambda bi, i, j: (bi, 0, i, 0)),
                  pl.BlockSpec((None, tk, KV_WIDTH), lambda bi, i, j: (bi, j, 0)),
                  pl.BlockSpec((None, tk, KV_WIDTH), lambda bi, i, j: (bi, j, 0))],
        out_specs=pl.BlockSpec((None, tq, Q_WIDTH), lambda bi, i, j: (bi, i, 0)),
        scratch_shapes=[pltpu.VMEM((N_HEADS, tq, LANES), F32), pltpu.VMEM((N_HEADS, tq, LANES), F32),
                        pltpu.VMEM((N_HEADS, tq, LANES), F32)],
        compiler_params=_cparams("parallel", "parallel", "arbitrary"),
        name="flash_attention",
    )(q, k, v)


def _dft_cos_sin(n):
    idx = np.arange(n, dtype=np.int64)
    ang = 2.0 * np.pi * ((idx[:, None] * idx[None, :]) % n).astype(np.float64) / n
    return np.cos(ang), np.sin(ang)


def _fft1_kernel(x_ref, w_ref, yr_ref, yi_ref):
    n1 = x_ref.shape[0]
    fw = yr_ref.shape[2]
    x = jnp.concatenate([x_ref[:, jj, :] for jj in range(x_ref.shape[1])], axis=1)
    y = _dot(w_ref[...], x.astype(BF16))
    for jj in range(yr_ref.shape[1]):
        yr_ref[:, jj, :] = y[:n1, jj * fw:(jj + 1) * fw]
        yi_ref[:, jj, :] = y[n1:, jj * fw:(jj + 1) * fw]


def _fft2_kernel(kb, scale, yr_ref, yi_ref, tc_ref, ts_ref, w2_ref, wc_ref, o_ref):
    n2 = FFT_N2
    w2 = w2_ref[...]
    wc = wc_ref[...]
    for j in range(kb):
        yr = yr_ref[j]
        yi = yi_ref[j]
        c = tc_ref[:, j:j + 1]
        s = ts_ref[:, j:j + 1]
        z = jnp.concatenate([yr * c + yi * s, yi * c - yr * s], axis=0).astype(BF16)
        gq = _dot(w2, z)
        for g in range(F_GROUPS):
            cols = slice(g * F_GROUP_DIM, (g + 1) * F_GROUP_DIM)
            gg = jnp.concatenate([gq[:n2, cols], gq[n2:, cols]], axis=1).astype(BF16)
            o_ref[:, j, g * F_GROUP_DIM:(g + 1) * F_GROUP_DIM] = _dot(gg, wc) * scale


def _fourier_mix_long(u):
    b, l, fw = u.shape
    n2 = FFT_N2
    n1 = l // n2
    cols = n2 * fw
    c1, s1 = _dft_cos_sin(n1)
    w1 = jnp.asarray(np.concatenate([c1, -s1], axis=0), BF16)
    c2, s2 = _dft_cos_sin(n2)
    w2 = jnp.asarray(np.block([[c2, s2], [-s2, c2]]), BF16)
    cc, sc = _dft_cos_sin(F_GROUP_DIM)
    wc = jnp.asarray(np.concatenate([cc, sc], axis=0), BF16)
    k1 = np.arange(n1, dtype=np.int64)
    nn = np.arange(n2, dtype=np.int64)
    ang = 2.0 * np.pi * ((nn[:, None] * k1[None, :]) % l).astype(np.float64) / l
    kb = SUBLANES
    tw_c = jnp.asarray(np.cos(ang).reshape(n2, n1 // kb, kb).transpose(1, 0, 2), F32)
    tw_s = jnp.asarray(np.sin(ang).reshape(n2, n1 // kb, kb).transpose(1, 0, 2), F32)
    tn = _pick(cols, (4096,))
    x2 = u.reshape(b, n1, n2, fw)
    yr, yi = pl.pallas_call(
        _fft1_kernel,
        out_shape=[jax.ShapeDtypeStruct((b, n1, n2, fw), F32)] * 2,
        grid=(b, cols // tn),
        in_specs=[pl.BlockSpec((None, n1, tn // fw, fw), lambda bi, j: (bi, 0, j, 0)),
                  pl.BlockSpec((2 * n1, n1), lambda bi, j: (0, 0))],
        out_specs=[pl.BlockSpec((None, n1, tn // fw, fw), lambda bi, j: (bi, 0, j, 0))] * 2,
        compiler_params=_cparams("parallel", "parallel"),
        name="fft_stage1",
    )(x2, w1)
    scale = float(1.0 / np.sqrt(float(l) * F_GROUP_DIM))
    yblk = pl.BlockSpec((None, kb, n2, fw), lambda bi, j: (bi, j, 0, 0))
    tblk = pl.BlockSpec((None, n2, kb), lambda bi, j: (j, 0, 0))
    out = pl.pallas_call(
        functools.partial(_fft2_kernel, kb, scale),
        out_shape=jax.ShapeDtypeStruct((b, n2, n1, fw), F32),
        grid=(b, n1 // kb),
        in_specs=[yblk, yblk, tblk, tblk,
                  pl.BlockSpec((2 * n2, 2 * n2), lambda bi, j: (0, 0)),
                  pl.BlockSpec((2 * F_GROUP_DIM, F_GROUP_DIM), lambda bi, j: (0, 0))],
        out_specs=pl.BlockSpec((None, n2, kb, fw), lambda bi, j: (bi, 0, j, 0)),
        compiler_params=_cparams("parallel", "parallel"),
        name="fft_stage2",
    )(yr, yi, tw_c, tw_s, w2, wc)
    return out.reshape(b, l, fw)


def _fftc_kernel(scale, u_ref, wl_ref, wc_ref, o_ref):
    fw = u_ref.shape[1]
    ab = _dot(u_ref[...].astype(BF16), wc_ref[...])
    st = jnp.concatenate([ab[:, :fw], ab[:, fw:]], axis=0).astype(BF16)
    o_ref[...] = _dot(wl_ref[...], st) * scale


def _fourier_mix_short(u):
    b, l, fw = u.shape
    cl, sl = _dft_cos_sin(l)
    wl = jnp.asarray(np.concatenate([cl, -sl], axis=1), BF16)
    cc, sc = _dft_cos_sin(F_GROUP_DIM)
    eye = np.eye(F_GROUPS)
    wc = jnp.asarray(np.concatenate([np.kron(eye, cc), np.kron(eye, sc)], axis=1), BF16)
    scale = float(1.0 / np.sqrt(float(l) * F_GROUP_DIM))
    return pl.pallas_call(
        functools.partial(_fftc_kernel, scale),
        out_shape=jax.ShapeDtypeStruct((b, l, fw), F32),
        grid=(b,),
        in_specs=[pl.BlockSpec((None, l, fw), lambda bi: (bi, 0, 0)),
                  pl.BlockSpec((l, 2 * l), lambda bi: (0, 0)),
                  pl.BlockSpec((fw, 2 * fw), lambda bi: (0, 0))],
        out_specs=pl.BlockSpec((None, l, fw), lambda bi: (bi, 0, 0)),
        compiler_params=_cparams("parallel"),
        name="fft_direct",
    )(u, wl, wc)


def _mixout_kernel(f_ref, a_ref, wf_ref, wa_ref, x_ref, gate_ref, nw_ref, o_ref):
    y = _dot(f_ref[...].astype(BF16), wf_ref[...]) + _dot(a_ref[...].astype(BF16), wa_ref[...])
    o_ref[...] = x_ref[...] + gate_ref[...] * _rms(y, nw_ref[...])


def _mixer_out(f, a, wf, wa, x, gate, nw):
    b, s, d = x.shape
    tm = _pick(s, (512, 256, 128))
    row = lambda w: pl.BlockSpec((None, tm, w), lambda bi, i: (bi, i, 0))
    const = lambda shp: pl.BlockSpec(shp, lambda bi, i: (0, 0))
    return pl.pallas_call(
        _mixout_kernel,
        out_shape=jax.ShapeDtypeStruct((b, s, d), F32),
        grid=(b, s // tm),
        in_specs=[row(f.shape[2]), row(a.shape[2]), const(wf.shape), const(wa.shape), row(d),
                  pl.BlockSpec((None, 1, d), lambda bi, i: (bi, 0, 0)), const((1, d))],
        out_specs=row(d),
        compiler_params=_cparams("parallel", "parallel"),
        name="mixer_out_residual",
    )(f, a, wf, wa, x, gate, nw)


def _ffn_kernel(n_e, n_f, use_comb, *refs):
    if use_comb:
        (x_ref, nw1_ref, sh_ref, sc_ref, wg_ref, wu_ref, wd_ref, gate_ref, nw2_ref, comb_ref,
         o_ref, h_scr, acc_scr) = refs
    else:
        (x_ref, nw1_ref, sh_ref, sc_ref, wg_ref, wu_ref, wd_ref, gate_ref, nw2_ref,
         o_ref, h_scr, acc_scr) = refs
    e = pl.program_id(2)
    f = pl.program_id(3)

    @pl.when((e == 0) & (f == 0))
    def _():
        h_scr[...] = _norm_mod(x_ref[...], nw1_ref[...], sh_ref[...], sc_ref[...]).astype(BF16)
        acc_scr[...] = jnp.zeros(acc_scr.shape, F32)

    h = h_scr[...]
    a = _silu(_dot(h, wg_ref[...])) * _dot(h, wu_ref[...])
    if use_comb:
        comb = comb_ref[...]
        lane = lax.broadcasted_iota(jnp.int32, comb.shape, 1)
        a = a * jnp.sum(jnp.where(lane == e, comb, 0.0), axis=-1, keepdims=True)
    acc_scr[...] += _dot(a.astype(BF16), wd_ref[...])

    @pl.when((e == n_e - 1) & (f == n_f - 1))
    def _():
        o_ref[...] = x_ref[...] + gate_ref[...] * _rms(acc_scr[...], nw2_ref[...])


def _ffn(x, nw1, shift, scale, w_gu, w_down, gate, nw2, comb=None):
    b, s, d = x.shape
    n_e, _, f2 = w_gu.shape
    ff = f2 // 2
    tm = _pick(s, (1024, 512, 256))
    tf = _pick(ff, (512, 256))
    n_f = ff // tf
    vec = pl.BlockSpec((None, 1, d), lambda bi, i, e, f: (bi, 0, 0))
    const = pl.BlockSpec((1, d), lambda bi, i, e, f: (0, 0))
    row = pl.BlockSpec((None, tm, d), lambda bi, i, e, f: (bi, i, 0))
    in_specs = [row, const, vec, vec,
                pl.BlockSpec((None, d, tf), lambda bi, i, e, f: (e, 0, f)),
                pl.BlockSpec((None, d, tf), lambda bi, i, e, f: (e, 0, f + n_f)),
                pl.BlockSpec((None, tf, d), lambda bi, i, e, f: (e, f, 0)),
                vec, const]
    args = [x, nw1, shift, scale, w_gu, w_gu, w_down, gate, nw2]
    if comb is not None:
        in_specs.append(pl.BlockSpec((None, tm, LANES), lambda bi, i, e, f: (bi, i, 0)))
        args.append(comb)
    return pl.pallas_call(
        functools.partial(_ffn_kernel, n_e, n_f, comb is not None),
        out_shape=jax.ShapeDtypeStruct((b, s, d), F32),
        grid=(b, s // tm, n_e, n_f),
        in_specs=in_specs,
        out_specs=row,
        scratch_shapes=[pltpu.VMEM((tm, d), BF16), pltpu.VMEM((tm, d), F32)],
        compiler_params=_cparams("parallel", "parallel", "arbitrary", "arbitrary"),
        name="swiglu_residual",
    )(*args)


def _router_kernel(x_ref, nw_ref, sh_ref, sc_ref, rw_ref, comb_ref):
    h = _norm_mod(x_ref[...], nw_ref[...], sh_ref[...], sc_ref[...])
    logits = _router_logits(h, rw_ref[...])
    lane = lax.broadcasted_iota(jnp.int32, logits.shape, 1)
    i1, i2, g1, g2 = _top2(logits, lane)
    comb_ref[...] = jnp.where(lane == i1, g1, 0.0) + jnp.where(lane == i2, g2, 0.0)


def _router(x, nw, shift, scale, rw_pad):
    b, s, d = x.shape
    tm = _pick(s, (512, 256, 128))
    vec = pl.BlockSpec((None, 1, d), lambda bi, i: (bi, 0, 0))
    return pl.pallas_call(
        _router_kernel,
        out_shape=jax.ShapeDtypeStruct((b, s, LANES), F32),
        grid=(b, s // tm),
        in_specs=[pl.BlockSpec((None, tm, d), lambda bi, i: (bi, i, 0)),
                  pl.BlockSpec((1, d), lambda bi, i: (0, 0)), vec, vec,
                  pl.BlockSpec((d, LANES), lambda bi, i: (0, 0))],
        out_specs=pl.BlockSpec((None, tm, LANES), lambda bi, i: (bi, i, 0)),
        compiler_params=_cparams("parallel", "parallel"),
        name="router_top2",
    )(x, nw, shift, scale, rw_pad)


def _top2(logits, lane):
    neg = -jnp.inf
    logits = jnp.where(lane < N_EXPERTS, logits, neg)
    m1 = jnp.max(logits, axis=-1, keepdims=True)
    i1 = jnp.min(jnp.where(logits == m1, lane, LANES), axis=-1, keepdims=True)
    rest = jnp.where(lane == i1, neg, logits)
    m2 = jnp.max(rest, axis=-1, keepdims=True)
    i2 = jnp.min(jnp.where(rest == m2, lane, LANES), axis=-1, keepdims=True)
    e2 = jnp.exp(m2 - m1)
    den = 1.0 + e2
    return i1, i2, 1.0 / den, e2 / den


def _router_logits(h, rw):
    h_hi, h_lo = _split_bf16(h, 2)
    w_hi, w_lo = _split_bf16(rw, 2)
    return _dot(h_hi, w_hi) + _dot(h_hi, w_lo) + _dot(h_lo, w_hi)


META_I1, META_I2, META_G1, META_G2, META_R1, META_R2 = range(6)


def _route_kernel(x_ref, nw_ref, sh_ref, sc_ref, rw_ref, meta_ref, cnt_ref, carry):
    @pl.when(pl.program_id(0) == 0)
    def _():
        carry[...] = jnp.zeros(carry.shape, F32)

    h = _norm_mod(x_ref[...], nw_ref[...], sh_ref[...], sc_ref[...])
    logits = _router_logits(h, rw_ref[...])
    tm = logits.shape[0]
    lane = lax.broadcasted_iota(jnp.int32, logits.shape, 1)
    i1, i2, g1, g2 = _top2(logits, lane)
    oh1 = lane == i1
    oh2 = lane == i2
    r = lax.broadcasted_iota(jnp.int32, (tm, tm), 0)
    c = lax.broadcasted_iota(jnp.int32, (tm, tm), 1)
    earlier = jnp.where(c < r, 1.0, 0.0).astype(BF16)
    f1 = jnp.where(oh1, 1.0, 0.0)
    f2 = jnp.where(oh2, 1.0, 0.0)
    before1 = _dot(earlier, f1.astype(BF16))
    before2 = _dot(earlier, f2.astype(BF16))
    c1 = jnp.sum(f1, axis=0, keepdims=True)
    c2 = jnp.sum(f2, axis=0, keepdims=True)
    base = carry[...]
    rank1 = jnp.sum(jnp.where(oh1, before1 + base, 0.0), axis=-1, keepdims=True)
    rank2 = jnp.sum(jnp.where(oh2, before2 + base + c1, 0.0), axis=-1, keepdims=True)
    carry[...] = base + c1 + c2
    cnt_ref[...] = carry[...]
    meta = jnp.zeros(logits.shape, F32)
    for slot, val in ((META_I1, i1.astype(F32)), (META_I2, i2.astype(F32)), (META_G1, g1), (META_G2, g2),
                      (META_R1, rank1), (META_R2, rank2)):
        meta = jnp.where(lane == slot, val, meta)
    meta_ref[...] = meta


def _dispatch_kernel(nt, pos_ref, prev_ref, x_ref, nw_ref, sh_ref, sc_ref, zeros_ref, xs_ref, hbuf, sem):
    del zeros_ref
    i = pl.program_id(0)
    tm = hbuf.shape[1]
    slot = i % 2

    def row_copy(sl, r, p):
        return pltpu.make_async_copy(hbuf.at[sl, pl.ds(r, 1)], xs_ref.at[pl.ds(p, 1)], sem.at[sl])

    def drain(sl, p_ref):
        def body(r, carry):
            row_copy(sl, r, p_ref[0, r]).wait()
            row_copy(sl, r, p_ref[0, tm + r]).wait()
            return carry
        lax.fori_loop(0, tm, body, 0)

    hbuf[slot] = _norm_mod(x_ref[...], nw_ref[...], sh_ref[...], sc_ref[...])

    def issue(r, carry):
        row_copy(slot, r, pos_ref[0, r]).start()
        row_copy(slot, r, pos_ref[0, tm + r]).start()
        return carry

    lax.fori_loop(0, tm, issue, 0)

    @pl.when(i > 0)
    def _():
        drain(1 - slot, prev_ref)

    @pl.when(i == nt - 1)
    def _():
        drain(slot, pos_ref)


def _expert_kernel(n_f, te_ref, na_ref, xs_ref, wg_ref, wu_ref, wd_ref, ys_ref, h_scr):
    i = pl.program_id(0)
    f = pl.program_id(1)

    @pl.when(i < na_ref[0])
    def _():
        @pl.when(f == 0)
        def _():
            h_scr[...] = xs_ref[...].astype(BF16)

        h = h_scr[...]
        a = _silu(_dot(h, wg_ref[...])) * _dot(h, wu_ref[...])
        part = _dot(a.astype(BF16), wd_ref[...])

        @pl.when(f == 0)
        def _():
            ys_ref[...] = part

        @pl.when(f > 0)
        def _():
            ys_ref[...] += part

    @pl.when((i >= na_ref[0]) & (f == 0))
    def _():
        ys_ref[...] = jnp.zeros(ys_ref.shape, F32)


def _combine_kernel(nt, pos_ref, next_ref, ys_ref, meta_ref, x_ref, gate_ref, nw_ref, o_ref, buf, sem):
    i = pl.program_id(0)
    tm = x_ref.shape[0]
    slot = i % 2

    def row_copy(sl, k, r, p):
        return pltpu.make_async_copy(ys_ref.at[pl.ds(p, 1)], buf.at[sl, k, pl.ds(r, 1)], sem.at[sl])

    def issue(sl, p_ref):
        def body(r, carry):
            row_copy(sl, 0, r, p_ref[0, r]).start()
            row_copy(sl, 1, r, p_ref[0, tm + r]).start()
            return carry
        lax.fori_loop(0, tm, body, 0)

    @pl.when(i == 0)
    def _():
        issue(slot, pos_ref)

    @pl.when(i + 1 < nt)
    def _():
        issue(1 - slot, next_ref)

    def drain(r, carry):
        row_copy(slot, 0, r, pos_ref[0, r]).wait()
        row_copy(slot, 1, r, pos_ref[0, tm + r]).wait()
        return carry

    lax.fori_loop(0, tm, drain, 0)
    meta = meta_ref[...]
    lane = lax.broadcasted_iota(jnp.int32, meta.shape, 1)
    g1 = jnp.sum(jnp.where(lane == META_G1, meta, 0.0), axis=-1, keepdims=True)
    g2 = jnp.sum(jnp.where(lane == META_G2, meta, 0.0), axis=-1, keepdims=True)
    y = g1 * buf[slot, 0] + g2 * buf[slot, 1]
    o_ref[...] = x_ref[...] + gate_ref[...] * _rms(y, nw_ref[...])


def _moe_sparse(x, nw1, shift, scale, rw_pad, w_gu, w_down, gate, nw2):
    b, s, d = x.shape
    n = b * s
    tm = _pick(s, (512, 256, 128))
    tpb = s // tm
    nt = n // tm
    n_e, _, f2 = w_gu.shape
    ff = f2 // 2
    tf = _pick(ff, (512, 256))
    n_f = ff // tf
    te_rows = EXPERT_TILE
    n_rows = 2 * n + n_e * te_rows
    nte = n_rows // te_rows

    xrow = pl.BlockSpec((None, tm, d), lambda i: (i // tpb, i % tpb, 0))
    vec = pl.BlockSpec((None, 1, d), lambda i: (i // tpb, 0, 0))
    const = lambda shp: pl.BlockSpec(shp, lambda i: (0, 0))
    mrow = pl.BlockSpec((None, tm, LANES), lambda i: (i // tpb, i % tpb, 0))
    meta, cnt = pl.pallas_call(
        _route_kernel,
        out_shape=[jax.ShapeDtypeStruct((b, s, LANES), F32), jax.ShapeDtypeStruct((1, LANES), F32)],
        grid=(nt,),
        in_specs=[xrow, const((1, d)), vec, vec, const((d, LANES))],
        out_specs=[mrow, const((1, LANES))],
        scratch_shapes=[pltpu.VMEM((1, LANES), F32)],
        compiler_params=_cparams("arbitrary"),
        name="route_top2_rank",
    )(x, nw1, shift, scale, rw_pad)

    m2 = meta.reshape(n, LANES)
    i1 = m2[:, META_I1].astype(jnp.int32)
    i2 = m2[:, META_I2].astype(jnp.int32)
    counts = cnt[0, :n_e].astype(jnp.int32)
    padded = (counts + te_rows - 1) // te_rows * te_rows
    ends = jnp.cumsum(padded)
    starts = ends - padded
    pos1 = jnp.take(starts, i1) + m2[:, META_R1].astype(jnp.int32)
    pos2 = jnp.take(starts, i2) + m2[:, META_R2].astype(jnp.int32)
    pos = jnp.concatenate([pos1.reshape(nt, 1, tm), pos2.reshape(nt, 1, tm)], axis=2)
    n_active = (ends[-1] // te_rows).astype(jnp.int32).reshape(1)
    tile_idx = jnp.minimum(jnp.arange(nte, dtype=jnp.int32), n_active[0] - 1)
    tile_expert = jnp.minimum(jnp.sum((tile_idx * te_rows)[:, None] >= ends[None, :], axis=1), n_e - 1).astype(jnp.int32)

    def pos_block(index):
        return pl.BlockSpec((None, 1, 2 * tm), lambda i: (index(i), 0, 0), memory_space=pltpu.SMEM)

    pos_spec = pos_block(lambda i: i)
    any_spec = pl.BlockSpec(memory_space=pl.ANY)
    xs = pl.pallas_call(
        functools.partial(_dispatch_kernel, nt),
        out_shape=jax.ShapeDtypeStruct((n_rows, d), F32),
        grid=(nt,),
        in_specs=[pos_spec, pos_block(lambda i: jnp.maximum(i - 1, 0)), xrow, const((1, d)), vec, vec, any_spec],
        out_specs=any_spec,
        scratch_shapes=[pltpu.VMEM((2, tm, d), F32), pltpu.SemaphoreType.DMA((2,))],
        input_output_aliases={6: 0},
        compiler_params=_cparams("arbitrary"),
        name="moe_dispatch",
    )(pos, pos, x, nw1, shift, scale, jnp.zeros((n_rows, d), F32))

    def live(i, na):
        return jnp.minimum(i, na[0] - 1)

    def fidx(i, f, na):
        return jnp.where(i < na[0], f, n_f - 1)

    ys = pl.pallas_call(
        functools.partial(_expert_kernel, n_f),
        out_shape=jax.ShapeDtypeStruct((n_rows, d), F32),
        grid_spec=pltpu.PrefetchScalarGridSpec(
            num_scalar_prefetch=2,
            grid=(nte, n_f),
            in_specs=[pl.BlockSpec((te_rows, d), lambda i, f, te, na: (live(i, na), 0)),
                      pl.BlockSpec((None, d, tf), lambda i, f, te, na: (te[i], 0, fidx(i, f, na))),
                      pl.BlockSpec((None, d, tf), lambda i, f, te, na: (te[i], 0, fidx(i, f, na) + n_f)),
                      pl.BlockSpec((None, tf, d), lambda i, f, te, na: (te[i], fidx(i, f, na), 0))],
            out_specs=pl.BlockSpec((te_rows, d), lambda i, f, te, na: (i, 0)),
            scratch_shapes=[pltpu.VMEM((te_rows, d), BF16)]),
        compiler_params=_cparams("arbitrary", "arbitrary"),
        name="moe_experts",
    )(tile_expert, n_active, xs, w_gu, w_gu, w_down)

    return pl.pallas_call(
        functools.partial(_combine_kernel, nt),
        out_shape=jax.ShapeDtypeStruct((b, s, d), F32),
        grid=(nt,),
        in_specs=[pos_spec, pos_block(lambda i: jnp.minimum(i + 1, nt - 1)), any_spec, mrow, xrow, vec,
                  const((1, d))],
        out_specs=xrow,
        scratch_shapes=[pltpu.VMEM((2, 2, tm, d), F32), pltpu.SemaphoreType.DMA((2,))],
        compiler_params=_cparams("arbitrary"),
        name="moe_combine",
    )(pos, pos, ys, meta, x, gate, nw2)


def _conv_kernel(nt, prev_ref, cur_ref, next_ref, w_ref, b_ref, xs_ref, bm_ref, cm_ref, buf):
    i = pl.program_id(1)
    tm = cur_ref.shape[0]
    halo = SUBLANES
    buf[halo:halo + tm, :] = cur_ref[...]
    buf[0:halo, :] = jnp.where(i > 0, prev_ref[...], 0.0)
    buf[halo + tm:2 * halo + tm, :] = jnp.where(i < nt - 1, next_ref[...], 0.0)
    ext = buf[...]
    n_ext = ext.shape[0]
    down = lambda v: pltpu.roll(v, 1, 0)
    up = lambda v: pltpu.roll(v, n_ext - 1, 0)
    w = [w_ref[j:j + 1, :] for j in range(D_CONV)]
    acc = (w[2] * ext + down(w[1] * ext + down(w[0] * ext)) + up(w[3] * ext + up(w[4] * ext)))
    y = _silu(acc[halo:halo + tm, :] + b_ref[...])
    d_inner = xs_ref.shape[1]
    gn = bm_ref.shape[1]
    xs_ref[...] = y[:, :d_inner]
    bm_ref[...] = y[:, d_inner:d_inner + gn]
    cm_ref[...] = y[:, d_inner + gn:]


def _conv_silu(xbc, conv_w, conv_b, d_inner):
    b, s, c = xbc.shape
    gn = (c - d_inner) // 2
    tm = _pick(s, (512, 256, 128))
    nt = s // tm
    hb = tm // SUBLANES
    last = s // SUBLANES - 1
    row = lambda w: pl.BlockSpec((None, tm, w), lambda bi, i: (bi, i, 0))
    return pl.pallas_call(
        functools.partial(_conv_kernel, nt),
        out_shape=[jax.ShapeDtypeStruct((b, s, d_inner), F32),
                   jax.ShapeDtypeStruct((b, s, gn), F32), jax.ShapeDtypeStruct((b, s, gn), F32)],
        grid=(b, nt),
        in_specs=[pl.BlockSpec((None, SUBLANES, c), lambda bi, i: (bi, jnp.maximum(i * hb - 1, 0), 0)),
                  row(c),
                  pl.BlockSpec((None, SUBLANES, c), lambda bi, i: (bi, jnp.minimum((i + 1) * hb, last), 0)),
                  pl.BlockSpec((D_CONV, c), lambda bi, i: (0, 0)),
                  pl.BlockSpec((1, c), lambda bi, i: (0, 0))],
        out_specs=[row(d_inner), row(gn), row(gn)],
        scratch_shapes=[pltpu.VMEM((tm + 2 * SUBLANES, c), F32)],
        compiler_params=_cparams("parallel", "parallel"),
        name="dwconv_silu",
    )(xbc, xbc, xbc, conv_w, conv_b)


def _softplus(x):
    return jnp.maximum(x, 0.0) + jnp.log(1.0 + jnp.exp(-jnp.abs(x)))


def _ssd_kernel(direction, nc, xs_ref, bm_ref, cm_ref, dt_ref, dtb_ref, a_ref, h0_ref, e_ref,
                y_ref, hT_ref, st_scr):
    c = pl.program_id(1)

    @pl.when(c == 0)
    def _():
        st_scr[...] = h0_ref[...]

    q = CHUNK
    fwd = direction == 0
    dt = _softplus(dt_ref[...] + dtb_ref[...])
    d_a = dt * a_ref[...]
    row = lax.broadcasted_iota(jnp.int32, (q, q), 0)
    col = lax.broadcasted_iota(jnp.int32, (q, q), 1)
    causal = (row >= col) if fwd else (row <= col)
    tri = jnp.where(causal, 1.0, 0.0).astype(BF16)
    a_cum = _dot_split_rhs(tri, d_a, 3)
    a_cum_t = a_cum.T
    expand = e_ref[...]
    dt_x = _dot_split_lhs(dt, expand, 2)
    a_x = _dot_split_lhs(a_cum, expand, 2)
    last = q - 1 if fwd else 0
    a_last = a_x[last:last + 1, :]
    xdt = xs_ref[...] * dt_x
    xdt_b = xdt.astype(BF16)
    w_state = (jnp.exp(a_last - a_x) * xdt).astype(BF16)
    exp_a = jnp.exp(a_x)
    exp_last = jnp.exp(a_last)
    low = lax.broadcasted_iota(jnp.int32, (q, LANES), 1) < SSM_HEAD_DIM
    gw = HEADS_PER_GROUP * SSM_HEAD_DIM
    for g in range(N_GROUPS):
        gs = slice(g * gw, (g + 1) * gw)
        b_g = bm_ref[:, g * D_STATE:(g + 1) * D_STATE]
        c_b = cm_ref[:, g * D_STATE:(g + 1) * D_STATE].astype(BF16)
        cb = lax.dot_general(c_b, b_g.astype(BF16), (((1,), (1,)), ((), ())), preferred_element_type=F32)
        st_g = st_scr[:, gs]
        y_off = _dot(c_b, st_g.astype(BF16)) * exp_a[:, gs]
        s_new = _dot(b_g.T.astype(BF16), w_state[:, gs])
        for pr in range(HEADS_PER_GROUP // 2):
            ms = []
            for j in range(2):
                hl = direction * N_SSM_HEADS + g * HEADS_PER_GROUP + 2 * pr + j
                seg = a_cum[:, hl:hl + 1] - a_cum_t[hl:hl + 1, :]
                dec = jnp.exp(jnp.where(causal, seg, -jnp.inf))
                ms.append((cb * dec).astype(BF16))
            xp = xdt_b[:, g * gw + pr * LANES:g * gw + (pr + 1) * LANES]
            zero = jnp.zeros_like(xp)
            rhs = jnp.concatenate([jnp.where(low, xp, zero), jnp.where(low, zero, xp)], axis=0)
            y_ref[:, g * gw + pr * LANES:g * gw + (pr + 1) * LANES] = (
                _dot(jnp.concatenate(ms, axis=1), rhs) + y_off[:, pr * LANES:(pr + 1) * LANES]).astype(y_ref.dtype)
        st_scr[:, gs] = exp_last[:, gs] * st_g + s_new

    @pl.when(c == nc - 1)
    def _():
        hT_ref[...] = st_scr[...]


def _ssd_scan(direction, xs, bm, cm, dt_raw, dt_bias, a_neg, h0):
    b, s, d_inner = xs.shape
    gn = bm.shape[2]
    nc = s // CHUNK
    heads = np.arange(d_inner) // SSM_HEAD_DIM + direction * N_SSM_HEADS
    expand = jnp.asarray(np.tile(np.arange(LANES)[:, None] == heads[None, :], (2, 1)), BF16)
    if direction == 0:
        cidx = lambda bi, c: (bi, c, 0)
    else:
        cidx = lambda bi, c: (bi, nc - 1 - c, 0)
    const = lambda bi, c: (0, 0)
    st_spec = pl.BlockSpec((None, D_STATE, d_inner), lambda bi, c: (bi, 0, 0))
    return pl.pallas_call(
        functools.partial(_ssd_kernel, direction, nc),
        out_shape=[jax.ShapeDtypeStruct((b, s, d_inner), BF16),
                   jax.ShapeDtypeStruct((b, D_STATE, d_inner), F32)],
        grid=(b, nc),
        in_specs=[pl.BlockSpec((None, CHUNK, d_inner), cidx),
                  pl.BlockSpec((None, CHUNK, gn), cidx), pl.BlockSpec((None, CHUNK, gn), cidx),
                  pl.BlockSpec((None, CHUNK, LANES), cidx),
                  pl.BlockSpec((1, LANES), const), pl.BlockSpec((1, LANES), const),
                  st_spec, pl.BlockSpec((2 * LANES, d_inner), const)],
        out_specs=[pl.BlockSpec((None, CHUNK, d_inner), cidx), st_spec],
        scratch_shapes=[pltpu.VMEM((D_STATE, d_inner), F32)],
        compiler_params=_cparams("parallel", "arbitrary"),
        name="ssd_scan",
    )(xs, bm, cm, dt_raw, dt_bias, a_neg, h0, expand)


def _ssdout_kernel(yf_ref, yb_ref, xs_ref, z_ref, dsk_ref, gnw_ref, w_ref, x_ref, gate_ref, nw_ref, o_ref):
    y = yf_ref[...].astype(F32) + yb_ref[...].astype(F32) + dsk_ref[...] * xs_ref[...]
    gy = y * _silu(z_ref[...].astype(F32))
    gnw = gnw_ref[...]
    gw = gy.shape[1] // N_GROUPS
    parts = []
    for g in range(N_GROUPS):
        parts.append(_rms(gy[:, g * gw:(g + 1) * gw], gnw[:, g * gw:(g + 1) * gw]).astype(BF16))
    out = _dot(jnp.concatenate(parts, axis=1), w_ref[...])
    o_ref[...] = x_ref[...] + gate_ref[...] * _rms(out, nw_ref[...])


def _ssd_out(yf, yb, xs, z, d_skip, gnw, w_out, x, gate, nw):
    b, s, d = x.shape
    d_inner = xs.shape[2]
    tm = _pick(s, (256, 128))
    row = lambda w: pl.BlockSpec((None, tm, w), lambda bi, i: (bi, i, 0))
    const = lambda shp: pl.BlockSpec(shp, lambda bi, i: (0, 0))
    return pl.pallas_call(
        _ssdout_kernel,
        out_shape=jax.ShapeDtypeStruct((b, s, d), F32),
        grid=(b, s // tm),
        in_specs=[row(d_inner), row(d_inner), row(d_inner), row(d_inner),
                  const((1, d_inner)), const((1, d_inner)), const(w_out.shape), row(d),
                  pl.BlockSpec((None, 1, d), lambda bi, i: (bi, 0, 0)), const((1, d))],
        out_specs=row(d),
        compiler_params=_cparams("parallel", "parallel"),
        name="ssd_gate_out_residual",
    )(yf, yb, xs, z, d_skip, gnw, w_out, x, gate, nw)


def _rope_tables(seq):
    rows = seq // GRID_W
    row = jnp.repeat(jnp.arange(rows, dtype=F32), GRID_W)
    col = jnp.tile(jnp.arange(GRID_W, dtype=F32), rows)
    freqs = ROPE_THETA ** (-jnp.arange(0, AXIS_ROT_DIM, 2, dtype=F32) / AXIS_ROT_DIM)
    ang = jnp.concatenate([row[:, None] * freqs, col[:, None] * freqs], axis=-1)
    cos = jnp.repeat(jnp.cos(ang), 2, axis=-1)
    sin = jnp.repeat(jnp.sin(ang), 2, axis=-1) * jnp.tile(jnp.asarray([-1.0, 1.0], F32), HEAD_DIM // 2)
    return jnp.tile(cos, (1, LANES // HEAD_DIM)), jnp.tile(sin, (1, LANES // HEAD_DIM))


def _mod_vectors(mods_i, batch):
    d = mods_i.shape[1] // 6
    mx = [mods_i[:batch, k * d:(k + 1) * d].reshape(batch, 1, d) for k in range(6)]
    mc = [jnp.broadcast_to(mods_i[batch, k * d:(k + 1) * d].reshape(1, 1, d), (batch, 1, d)) for k in range(6)]
    return mx, mc


def _even_mixer(x, ctx, nw0, nw1, mx, mc, w_in, qnw, knw, w_out, rope, need_ctx):
    w_u = w_in[:, :F_WIDTH].astype(BF16)
    w_qkv = w_in[:, F_WIDTH:].astype(BF16)
    qw = jnp.tile(qnw.reshape(1, HEAD_DIM), (1, LANES // HEAD_DIM))
    kw = jnp.tile(knw.reshape(1, HEAD_DIM), (1, LANES // HEAD_DIM))
    w_f = w_out[:F_WIDTH].astype(BF16)
    d = w_out.shape[1]
    w_a = (w_out[F_WIDTH:].reshape(N_KV_HEADS, GQ, HEAD_DIM, d).transpose(1, 0, 2, 3)
           .reshape(Q_WIDTH, d).astype(BF16))
    lc = ctx.shape[1]
    ones = jnp.ones((lc, LANES), F32)
    zeros = jnp.zeros((lc, LANES), F32)

    u_c, qkv_c = _norm_mod_linear(ctx, nw0, mc[0], mc[1], [w_u, w_qkv], [F32, F32])
    q_c, k_c, v_c = _qk_prepare(qkv_c, ones, zeros, qw, kw)
    u_x, qkv_x = _norm_mod_linear(x, nw0, mx[0], mx[1], [w_u, w_qkv], [F32, F32])
    q_x, k_x, v_x = _qk_prepare(qkv_x, rope[0], rope[1], qw, kw)
    k_all = jnp.concatenate([k_c, k_x], axis=1)
    v_all = jnp.concatenate([v_c, v_x], axis=1)
    a_x = _attention(q_x, k_all, v_all)
    f_x = _fourier_mix_long(u_x)
    x_new = _mixer_out(f_x, a_x, w_f, w_a, x, mx[2], nw1)
    if not need_ctx:
        return x_new, ctx
    a_c = _attention(q_c, k_c, v_c)
    f_c = _fourier_mix_short(u_c)
    return x_new, _mixer_out(f_c, a_c, w_f, w_a, ctx, mc[2], nw1)


def _ssd_branch(h_in, nw0, shift, scale, w_z, w_xbc, w_dt, conv_w, conv_b, dt_bias, a_neg, h0_f, h0_b):
    z, xbc, dt_raw = _norm_mod_linear(h_in, nw0, shift, scale, [w_z, w_xbc, w_dt], [BF16, F32, F32], tm_max=256)
    xs, bm, cm = _conv_silu(xbc, conv_w, conv_b, w_z.shape[1])
    y_f, s_f = _ssd_scan(0, xs, bm, cm, dt_raw, dt_bias, a_neg, h0_f)
    y_b, s_b = _ssd_scan(1, xs, bm, cm, dt_raw, dt_bias, a_neg, h0_b)
    return z, xs, y_f, y_b, s_f, s_b


def _odd_mixer(x, ctx, nw0, nw1, mx, mc, w_in, conv_w, conv_b, dt_bias, a_log, d_skip, gnw, w_out, need_ctx):
    d_inner = w_out.shape[0]
    conv_dim = conv_w.shape[1]
    w_z = w_in[:, :d_inner].astype(BF16)
    w_xbc = w_in[:, d_inner:d_inner + conv_dim].astype(BF16)
    n_dt = w_in.shape[1] - d_inner - conv_dim
    w_dt = jnp.pad(w_in[:, d_inner + conv_dim:], ((0, 0), (0, LANES - n_dt))).astype(BF16)
    dtb = jnp.pad(dt_bias.reshape(1, n_dt), ((0, 0), (0, LANES - n_dt)))
    a_neg = jnp.pad(-jnp.exp(a_log.astype(F32)).reshape(1, n_dt), ((0, 0), (0, LANES - n_dt)))
    cb = conv_b.reshape(1, conv_dim)
    dsk = jnp.repeat(d_skip, SSM_HEAD_DIM).reshape(1, d_inner)
    gw = gnw.reshape(1, d_inner)
    w_o = w_out.astype(BF16)
    b = x.shape[0]
    zeros = jnp.zeros((b, D_STATE, d_inner), F32)

    z_c, xs_c, yf_c, yb_c, s_f, s_b = _ssd_branch(ctx, nw0, mc[0], mc[1], w_z, w_xbc, w_dt, conv_w, cb,
                                                  dtb, a_neg, zeros, zeros)
    z_x, xs_x, yf_x, yb_x, _, _ = _ssd_branch(x, nw0, mx[0], mx[1], w_z, w_xbc, w_dt, conv_w, cb,
                                              dtb, a_neg, s_f, s_b)
    x_new = _ssd_out(yf_x, yb_x, xs_x, z_x, dsk, gw, w_o, x, mx[2], nw1)
    if not need_ctx:
        return x_new, ctx
    return x_new, _ssd_out(yf_c, yb_c, xs_c, z_c, dsk, gw, w_o, ctx, mc[2], nw1)


def kernel(x, c, ctx, c_ctx, ada_w, ada_b, norm_w, mix_in_w, q_norm_w, k_norm_w, mix_out_w, ffn_gu_w, ffn_down_w, ssm_in_w, ssm_conv_w, ssm_conv_b, ssm_dt_bias, ssm_a_log, ssm_d, ssm_norm_w, ssm_out_w, router_w, moe_gu_w, moe_down_w):
    batch, seq, d = x.shape
    depth = ada_w.shape[0]
    rope = _rope_tables(seq)
    cvec = jnp.zeros((SUBLANES, d), F32).at[:batch].set(c).at[batch].set(c_ctx)
    mods = _modulation(cvec, ada_w, ada_b)
    for i in range(depth):
        last = i == depth - 1
        j = i // 2
        mx, mc = _mod_vectors(mods[i], batch)
        nw = [norm_w[i, k].reshape(1, d) for k in range(4)]
        if i % 2 == 0:
            x, ctx_mid = _even_mixer(x, ctx, nw[0], nw[1], mx, mc, mix_in_w[j], q_norm_w[j], k_norm_w[j],
                                     mix_out_w[j], rope, not last)
            w_gu = ffn_gu_w[j].astype(BF16)[None]
            w_dn = ffn_down_w[j].astype(BF16)[None]
            x = _ffn(x, nw[2], mx[3], mx[4], w_gu, w_dn, mx[5], nw[3])
            if not last:
                ctx = _ffn(ctx_mid, nw[2], mc[3], mc[4], w_gu, w_dn, mc[5], nw[3])
        else:
            x, ctx_mid = _odd_mixer(x, ctx, nw[0], nw[1], mx, mc, ssm_in_w[j], ssm_conv_w[j], ssm_conv_b[j],
                                    ssm_dt_bias[j], ssm_a_log[j], ssm_d[j], ssm_norm_w[j], ssm_out_w[j],
                                    not last)
            rw = jnp.pad(router_w[j], ((0, 0), (0, LANES - N_EXPERTS)))
            w_gu = moe_gu_w[j].astype(BF16)
            w_dn = moe_down_w[j].astype(BF16)
            x = _moe_sparse(x, nw[2], mx[3], mx[4], rw, w_gu, w_dn, mx[5], nw[3])
            if not last:
                comb_c = _router(ctx_mid, nw[2], mc[3], mc[4], rw)
                ctx = _ffn(ctx_mid, nw[2], mc[3], mc[4], w_gu, w_dn, mc[5], nw[3], comb_c)
    return x
```

```python
import functools

import numpy as np
import jax
import jax.numpy as jnp
from jax import lax
from jax.experimental import pallas as pl
from jax.experimental.pallas import tpu as pltpu

F32 = jnp.float32
BF16 = jnp.bfloat16

EPS = 1e-6
GRID_W = 64
HEAD_DIM = 64
N_HEADS = 8
N_KV_HEADS = 2
GQ = N_HEADS // N_KV_HEADS
F_GROUPS = 4
F_GROUP_DIM = 128
F_WIDTH = F_GROUPS * F_GROUP_DIM
Q_WIDTH = N_HEADS * HEAD_DIM
KV_WIDTH = N_KV_HEADS * HEAD_DIM
ROPE_THETA = 10000.0
AXIS_ROT_DIM = HEAD_DIM // 2
ATTN_SCALE = HEAD_DIM ** -0.5
LOG2_E = 1.4426950408889634
SSM_HEAD_DIM = 64
N_SSM_HEADS = 32
N_GROUPS = 4
HEADS_PER_GROUP = N_SSM_HEADS // N_GROUPS
D_STATE = 128
D_CONV = 5
CONV_PAD = D_CONV // 2
CHUNK = 128
N_EXPERTS = 8
EXPERT_TILE = 1024
LANES = 128
SUBLANES = 8
FFT_N2 = 128
VMEM_LIMIT = 56 * 1024 * 1024


def _cparams(*sem):
    return pltpu.CompilerParams(dimension_semantics=sem, vmem_limit_bytes=VMEM_LIMIT)


def _pick(n, candidates):
    for c in candidates:
        if n % c == 0:
            return c
    return n


def _sigmoid(x):
    return 1.0 / (1.0 + jnp.exp(-x))


def _silu(x):
    return x * _sigmoid(x)


def _rms(x, w):
    return x * lax.rsqrt(jnp.mean(x * x, axis=-1, keepdims=True) + EPS) * w


def _norm_mod(x, nw, shift, scale):
    return _rms(x, nw) * (1.0 + scale) + shift


def _dot(a, b):
    return jnp.dot(a, b, preferred_element_type=F32)


def _split_bf16(x, parts):
    out = []
    r = x
    for _ in range(parts):
        p = r.astype(BF16)
        out.append(p)
        r = r - p.astype(F32)
    return out


def _dot_split_lhs(x, m_stacked, parts):
    return _dot(jnp.concatenate(_split_bf16(x, parts), axis=1), m_stacked)


def _dot_split_rhs(m, x, parts):
    acc = None
    for p in _split_bf16(x, parts):
        t = _dot(m, p)
        acc = t if acc is None else acc + t
    return acc


def _mod_kernel(c_ref, w_ref, b_ref, o_ref):
    s = _silu(c_ref[...])
    o_ref[...] = jnp.dot(s, w_ref[...], preferred_element_type=F32,
                         precision=lax.Precision.HIGHEST) + b_ref[...]


def _modulation(cvec, ada_w, ada_b):
    depth, d, n = ada_w.shape
    tn = _pick(n, (1536, 1024, 512))
    rows = cvec.shape[0]
    return pl.pallas_call(
        _mod_kernel,
        out_shape=jax.ShapeDtypeStruct((depth, rows, n), F32),
        grid=(depth, n // tn),
        in_specs=[pl.BlockSpec((rows, d), lambda i, j: (0, 0)),
                  pl.BlockSpec((None, d, tn), lambda i, j: (i, 0, j)),
                  pl.BlockSpec((None, 1, tn), lambda i, j: (i, 0, j))],
        out_specs=pl.BlockSpec((None, rows, tn), lambda i, j: (i, 0, j)),
        compiler_params=_cparams("parallel", "parallel"),
        name="adaln_modulation",
    )(cvec, ada_w, ada_b.reshape(depth, 1, n))


def _nml_kernel(n_w, x_ref, nw_ref, sh_ref, sc_ref, *refs):
    h = _norm_mod(x_ref[...], nw_ref[...], sh_ref[...], sc_ref[...]).astype(BF16)
    for w_ref, o_ref in zip(refs[:n_w], refs[n_w:]):
        o_ref[...] = _dot(h, w_ref[...]).astype(o_ref.dtype)


def _norm_mod_linear(x, nw, shift, scale, weights, out_dtypes, tm_max=512):
    b, s, d = x.shape
    tm = _pick(s, (tm_max, 256, 128))
    n_w = len(weights)
    vec = pl.BlockSpec((None, 1, d), lambda bi, i: (bi, 0, 0))
    in_specs = [pl.BlockSpec((None, tm, d), lambda bi, i: (bi, i, 0)),
                pl.BlockSpec((1, d), lambda bi, i: (0, 0)), vec, vec]
    in_specs += [pl.BlockSpec(w.shape, lambda bi, i: (0, 0)) for w in weights]
    out_shape = [jax.ShapeDtypeStruct((b, s, w.shape[1]), dt) for w, dt in zip(weights, out_dtypes)]
    out_specs = [pl.BlockSpec((None, tm, w.shape[1]), lambda bi, i: (bi, i, 0)) for w in weights]
    return pl.pallas_call(
        functools.partial(_nml_kernel, n_w),
        out_shape=out_shape,
        grid=(b, s // tm),
        in_specs=in_specs,
        out_specs=out_specs,
        compiler_params=_cparams("parallel", "parallel"),
        name="norm_mod_linear",
    )(x, nw, shift, scale, *weights)


def _qkprep_kernel(qkv_ref, cos_ref, sin_ref, qw_ref, kw_ref, bd_ref, q_ref, k_ref, v_ref):
    tm = qkv_ref.shape[0]
    lane = lax.broadcasted_iota(jnp.int32, (tm, LANES), 1)
    even = (lane % 2) == 0
    low = lane < HEAD_DIM
    cos = cos_ref[...]
    sin = sin_ref[...]
    bd = bd_ref[...]

    def prep(xb, w):
        ms = _dot_split_lhs(xb * xb, bd, 2)
        y = xb * lax.rsqrt(ms + EPS) * w
        swapped = jnp.where(even, pltpu.roll(y, LANES - 1, 1), pltpu.roll(y, 1, 1))
        return y * cos + swapped * sin

    qw = qw_ref[...]
    for cblk in range(Q_WIDTH // LANES):
        r = prep(qkv_ref[:, cblk * LANES:(cblk + 1) * LANES], qw) * (ATTN_SCALE * LOG2_E)
        r_sw = pltpu.roll(r, HEAD_DIM, 1)
        kvh = (2 * cblk) // GQ
        keep = low if kvh == 0 else jnp.logical_not(low)
        for j in range(2):
            src = r if j == kvh else r_sw
            q_ref[2 * cblk + j] = jnp.where(keep, src, 0.0).astype(q_ref.dtype)
    k_ref[...] = prep(qkv_ref[:, Q_WIDTH:Q_WIDTH + KV_WIDTH], kw_ref[...]).astype(k_ref.dtype)
    v_ref[...] = qkv_ref[:, Q_WIDTH + KV_WIDTH:Q_WIDTH + 2 * KV_WIDTH].astype(v_ref.dtype)


def _qk_prepare(qkv, cos_t, sin_t, qw, kw):
    b, s, _ = qkv.shape
    tm = _pick(s, (512, 256, 128))
    bd = np.kron(np.eye(2, dtype=np.float32), np.full((HEAD_DIM, HEAD_DIM), 1.0 / HEAD_DIM, np.float32))
    bd = np.tile(bd, (2, 1))
    row = lambda bi, i: (i, 0)
    const = lambda bi, i: (0, 0)
    return pl.pallas_call(
        _qkprep_kernel,
        out_shape=[jax.ShapeDtypeStruct((b, N_HEADS, s, LANES), BF16),
                   jax.ShapeDtypeStruct((b, s, KV_WIDTH), BF16),
                   jax.ShapeDtypeStruct((b, s, KV_WIDTH), BF16)],
        grid=(b, s // tm),
        in_specs=[pl.BlockSpec((None, tm, qkv.shape[2]), lambda bi, i: (bi, i, 0)),
                  pl.BlockSpec((tm, LANES), row), pl.BlockSpec((tm, LANES), row),
                  pl.BlockSpec((1, LANES), const), pl.BlockSpec((1, LANES), const),
                  pl.BlockSpec((2 * LANES, LANES), const)],
        out_specs=[pl.BlockSpec((None, N_HEADS, tm, LANES), lambda bi, i: (bi, 0, i, 0)),
                   pl.BlockSpec((None, tm, KV_WIDTH), lambda bi, i: (bi, i, 0)),
                   pl.BlockSpec((None, tm, KV_WIDTH), lambda bi, i: (bi, i, 0))],
        compiler_params=_cparams("parallel", "parallel"),
        name="qk_norm_rope",
    )(qkv, cos_t, sin_t, qw, kw, jnp.asarray(bd, BF16))


def _attn_kernel(nk, q_ref, k_ref, v_ref, o_ref, m_scr, l_scr, acc_scr):
    kv = pl.program_id(2)

    @pl.when(kv == 0)
    def _():
        m_scr[...] = jnp.full(m_scr.shape, -jnp.inf, F32)
        l_scr[...] = jnp.zeros(l_scr.shape, F32)
        acc_scr[...] = jnp.zeros(acc_scr.shape, F32)

    k = k_ref[...]
    v = v_ref[...]
    n_t = k.shape[0] // LANES

    def head(hd, carry):
        s = lax.dot_general(q_ref[hd], k, (((1,), (1,)), ((), ())), preferred_element_type=F32)
        tiles = [s[:, t * LANES:(t + 1) * LANES] for t in range(n_t)]
        smax = tiles[0]
        for t in range(1, n_t):
            smax = jnp.maximum(smax, tiles[t])
        m_prev = m_scr[hd]
        m_new = jnp.maximum(m_prev, jnp.max(smax, axis=-1, keepdims=True))
        alpha = jnp.exp2(m_prev - m_new)
        ps = []
        lsum = None
        for t in range(n_t):
            p_t = jnp.exp2(tiles[t] - m_new)
            lsum = p_t if lsum is None else lsum + p_t
            ps.append(p_t.astype(BF16))
        l_scr[hd] = alpha * l_scr[hd] + lsum
        acc_scr[hd] = alpha * acc_scr[hd] + _dot(jnp.concatenate(ps, axis=1), v)
        m_scr[hd] = m_new
        return carry

    lax.fori_loop(0, N_HEADS, head, 0, unroll=8)

    @pl.when(kv == nk - 1)
    def _():
        tq = o_ref.shape[0]
        low = lax.broadcasted_iota(jnp.int32, (tq, LANES), 1) < HEAD_DIM
        for g in range(GQ):
            o0 = acc_scr[g] / jnp.sum(l_scr[g], axis=-1, keepdims=True)
            o1 = acc_scr[GQ + g] / jnp.sum(l_scr[GQ + g], axis=-1, keepdims=True)
            o_ref[:, g * LANES:(g + 1) * LANES] = jnp.where(low, o0, o1).astype(o_ref.dtype)


def _attention(q, k, v):
    b, _, s, _ = q.shape
    skv = k.shape[1]
    tq = _pick(s, (1024, 512, 256, 128))
    tk = _pick(skv, (1280, 640, 256, 128))
    nk = skv // tk
    return pl.pallas_call(
        functools.partial(_attn_kernel, nk),
        out_shape=jax.ShapeDtypeStruct((b, s, Q_WIDTH), BF16),
        grid=(b, s // tq, nk),
        in_specs=[pl.BlockSpec((None, N_HEADS, tq, LANES), lambda bi, i, j: (bi, 0, i, 0)),
                  pl.BlockSpec((None, tk, KV_WIDTH), lambda bi, i, j: (bi, j, 0)),
                  pl.BlockSpec((None, tk, KV_WIDTH), lambda bi, i, j: (bi, j, 0))],
        out_specs=pl.BlockSpec((None, tq, Q_WIDTH), lambda bi, i, j: (bi, i, 0)),
        scratch_shapes=[pltpu.VMEM((N_HEADS, tq, LANES), F32), pltpu.VMEM((N_HEADS, tq, LANES), F32),
                        pltpu.VMEM((N_HEADS, tq, LANES), F32)],
        compiler_params=_cparams("parallel", "parallel", "arbitrary"),
        name="flash_attention",
    )(q, k, v)


def _dft_cos_sin(n):
    idx = np.arange(n, dtype=np.int64)
    ang = 2.0 * np.pi * ((idx[:, None] * idx[None, :]) % n).astype(np.float64) / n
    return np.cos(ang), np.sin(ang)


def _fft1_kernel(x_ref, w_ref, yr_ref, yi_ref):
    n1 = x_ref.shape[0]
    fw = yr_ref.shape[2]
    x = jnp.concatenate([x_ref[:, jj, :] for jj in range(x_ref.shape[1])], axis=1)
    y = _dot(w_ref[...], x.astype(BF16))
    for jj in range(yr_ref.shape[1]):
        yr_ref[:, jj, :] = y[:n1, jj * fw:(jj + 1) * fw]
        yi_ref[:, jj, :] = y[n1:, jj * fw:(jj + 1) * fw]


def _fft2_kernel(kb, scale, yr_ref, yi_ref, tc_ref, ts_ref, w2_ref, wc_ref, o_ref):
    n2 = FFT_N2
    w2 = w2_ref[...]
    wc = wc_ref[...]
    for j in range(kb):
        yr = yr_ref[j]
        yi = yi_ref[j]
        c = tc_ref[:, j:j + 1]
        s = ts_ref[:, j:j + 1]
        z = jnp.concatenate([yr * c + yi * s, yi * c - yr * s], axis=0).astype(BF16)
        gq = _dot(w2, z)
        for g in range(F_GROUPS):
            cols = slice(g * F_GROUP_DIM, (g + 1) * F_GROUP_DIM)
            gg = jnp.concatenate([gq[:n2, cols], gq[n2:, cols]], axis=1).astype(BF16)
            o_ref[:, j, g * F_GROUP_DIM:(g + 1) * F_GROUP_DIM] = _dot(gg, wc) * scale


def _fourier_mix_long(u):
    b, l, fw = u.shape
    n2 = FFT_N2
    n1 = l // n2
    cols = n2 * fw
    c1, s1 = _dft_cos_sin(n1)
    w1 = jnp.asarray(np.concatenate([c1, -s1], axis=0), BF16)
    c2, s2 = _dft_cos_sin(n2)
    w2 = jnp.asarray(np.block([[c2, s2], [-s2, c2]]), BF16)
    cc, sc = _dft_cos_sin(F_GROUP_DIM)
    wc = jnp.asarray(np.concatenate([cc, sc], axis=0), BF16)
    k1 = np.arange(n1, dtype=np.int64)
    nn = np.arange(n2, dtype=np.int64)
    ang = 2.0 * np.pi * ((nn[:, None] * k1[None, :]) % l).astype(np.float64) / l
    kb = SUBLANES
    tw_c = jnp.asarray(np.cos(ang).reshape(n2, n1 // kb, kb).transpose(1, 0, 2), F32)
    tw_s = jnp.asarray(np.sin(ang).reshape(n2, n1 // kb, kb).transpose(1, 0, 2), F32)
    tn = _pick(cols, (4096,))
    x2 = u.reshape(b, n1, n2, fw)
    yr, yi = pl.pallas_call(
        _fft1_kernel,
        out_shape=[jax.ShapeDtypeStruct((b, n1, n2, fw), F32)] * 2,
        grid=(b, cols // tn),
        in_specs=[pl.BlockSpec((None, n1, tn // fw, fw), lambda bi, j: (bi, 0, j, 0)),
                  pl.BlockSpec((2 * n1, n1), lambda bi, j: (0, 0))],
        out_specs=[pl.BlockSpec((None, n1, tn // fw, fw), lambda bi, j: (bi, 0, j, 0))] * 2,
        compiler_params=_cparams("parallel", "parallel"),
        name="fft_stage1",
    )(x2, w1)
    scale = float(1.0 / np.sqrt(float(l) * F_GROUP_DIM))
    yblk = pl.BlockSpec((None, kb, n2, fw), lambda bi, j: (bi, j, 0, 0))
    tblk = pl.BlockSpec((None, n2, kb), lambda bi, j: (j, 0, 0))
    out = pl.pallas_call(
        functools.partial(_fft2_kernel, kb, scale),
        out_shape=jax.ShapeDtypeStruct((b, n2, n1, fw), F32),
        grid=(b, n1 // kb),
        in_specs=[yblk, yblk, tblk, tblk,
                  pl.BlockSpec((2 * n2, 2 * n2), lambda bi, j: (0, 0)),
                  pl.BlockSpec((2 * F_GROUP_DIM, F_GROUP_DIM), lambda bi, j: (0, 0))],
        out_specs=pl.BlockSpec((None, n2, kb, fw), lambda bi, j: (bi, 0, j, 0)),
        compiler_params=_cparams("parallel", "parallel"),
        name="fft_stage2",
    )(yr, yi, tw_c, tw_s, w2, wc)
    return out.reshape(b, l, fw)


def _fftc_kernel(scale, u_ref, wl_ref, wc_ref, o_ref):
    fw = u_ref.shape[1]
    ab = _dot(u_ref[...].astype(BF16), wc_ref[...])
    st = jnp.concatenate([ab[:, :fw], ab[:, fw:]], axis=0).astype(BF16)
    o_ref[...] = _dot(wl_ref[...], st) * scale


def _fourier_mix_short(u):
    b, l, fw = u.shape
    cl, sl = _dft_cos_sin(l)
    wl = jnp.asarray(np.concatenate([cl, -sl], axis=1), BF16)
    cc, sc = _dft_cos_sin(F_GROUP_DIM)
    eye = np.eye(F_GROUPS)
    wc = jnp.asarray(np.concatenate([np.kron(eye, cc), np.kron(eye, sc)], axis=1), BF16)
    scale = float(1.0 / np.sqrt(float(l) * F_GROUP_DIM))
    return pl.pallas_call(
        functools.partial(_fftc_kernel, scale),
        out_shape=jax.ShapeDtypeStruct((b, l, fw), F32),
        grid=(b,),
        in_specs=[pl.BlockSpec((None, l, fw), lambda bi: (bi, 0, 0)),
                  pl.BlockSpec((l, 2 * l), lambda bi: (0, 0)),
                  pl.BlockSpec((fw, 2 * fw), lambda bi: (0, 0))],
        out_specs=pl.BlockSpec((None, l, fw), lambda bi: (bi, 0, 0)),
        compiler_params=_cparams("parallel"),
        name="fft_direct",
    )(u, wl, wc)


def _mixout_kernel(f_ref, a_ref, wf_ref, wa_ref, x_ref, gate_ref, nw_ref, o_ref):
    y = _dot(f_ref[...].astype(BF16), wf_ref[...]) + _dot(a_ref[...].astype(BF16), wa_ref[...])
    o_ref[...] = x_ref[...] + gate_ref[...] * _rms(y, nw_ref[...])


def _mixer_out(f, a, wf, wa, x, gate, nw):
    b, s, d = x.shape
    tm = _pick(s, (512, 256, 128))
    row = lambda w: pl.BlockSpec((None, tm, w), lambda bi, i: (bi, i, 0))
    const = lambda shp: pl.BlockSpec(shp, lambda bi, i: (0, 0))
    return pl.pallas_call(
        _mixout_kernel,
        out_shape=jax.ShapeDtypeStruct((b, s, d), F32),
        grid=(b, s // tm),
        in_specs=[row(f.shape[2]), row(a.shape[2]), const(wf.shape), const(wa.shape), row(d),
                  pl.BlockSpec((None, 1, d), lambda bi, i: (bi, 0, 0)), const((1, d))],
        out_specs=row(d),
        compiler_params=_cparams("parallel", "parallel"),
        name="mixer_out_residual",
    )(f, a, wf, wa, x, gate, nw)


def _ffn_kernel(n_e, n_f, use_comb, *refs):
    if use_comb:
        (x_ref, nw1_ref, sh_ref, sc_ref, wg_ref, wu_ref, wd_ref, gate_ref, nw2_ref, comb_ref,
         o_ref, h_scr, acc_scr) = refs
    else:
        (x_ref, nw1_ref, sh_ref, sc_ref, wg_ref, wu_ref, wd_ref, gate_ref, nw2_ref,
         o_ref, h_scr, acc_scr) = refs
    e = pl.program_id(2)
    f = pl.program_id(3)

    @pl.when((e == 0) & (f == 0))
    def _():
        h_scr[...] = _norm_mod(x_ref[...], nw1_ref[...], sh_ref[...], sc_ref[...]).astype(BF16)
        acc_scr[...] = jnp.zeros(acc_scr.shape, F32)

    h = h_scr[...]
    a = _silu(_dot(h, wg_ref[...])) * _dot(h, wu_ref[...])
    if use_comb:
        comb = comb_ref[...]
        lane = lax.broadcasted_iota(jnp.int32, comb.shape, 1)
        a = a * jnp.sum(jnp.where(lane == e, comb, 0.0), axis=-1, keepdims=True)
    acc_scr[...] += _dot(a.astype(BF16), wd_ref[...])

    @pl.when((e == n_e - 1) & (f == n_f - 1))
    def _():
        o_ref[...] = x_ref[...] + gate_ref[...] * _rms(acc_scr[...], nw2_ref[...])


def _ffn(x, nw1, shift, scale, w_gu, w_down, gate, nw2, comb=None):
    b, s, d = x.shape
    n_e, _, f2 = w_gu.shape
    ff = f2 // 2
    tm = _pick(s, (1024, 512, 256))
    tf = _pick(ff, (512, 256))
    n_f = ff // tf
    vec = pl.BlockSpec((None, 1, d), lambda bi, i, e, f: (bi, 0, 0))
    const = pl.BlockSpec((1, d), lambda bi, i, e, f: (0, 0))
    row = pl.BlockSpec((None, tm, d), lambda bi, i, e, f: (bi, i, 0))
    in_specs = [row, const, vec, vec,
                pl.BlockSpec((None, d, tf), lambda bi, i, e, f: (e, 0, f)),
                pl.BlockSpec((None, d, tf), lambda bi, i, e, f: (e, 0, f + n_f)),
                pl.BlockSpec((None, tf, d), lambda bi, i, e, f: (e, f, 0)),
                vec, const]
    args = [x, nw1, shift, scale, w_gu, w_gu, w_down, gate, nw2]
    if comb is not None:
        in_specs.append(pl.BlockSpec((None, tm, LANES), lambda bi, i, e, f: (bi, i, 0)))
        args.append(comb)
    return pl.pallas_call(
        functools.partial(_ffn_kernel, n_e, n_f, comb is not None),
        out_shape=jax.ShapeDtypeStruct((b, s, d), F32),
        grid=(b, s // tm, n_e, n_f),
        in_specs=in_specs,
        out_specs=row,
        scratch_shapes=[pltpu.VMEM((tm, d), BF16), pltpu.VMEM((tm, d), F32)],
        compiler_params=_cparams("parallel", "parallel", "arbitrary", "arbitrary"),
        name="swiglu_residual",
    )(*args)


def _router_kernel(x_ref, nw_ref, sh_ref, sc_ref, rw_ref, comb_ref):
    h = _norm_mod(x_ref[...], nw_ref[...], sh_ref[...], sc_ref[...])
    logits = _router_logits(h, rw_ref[...])
    lane = lax.broadcasted_iota(jnp.int32, logits.shape, 1)
    i1, i2, g1, g2 = _top2(logits, lane)
    comb_ref[...] = jnp.where(lane == i1, g1, 0.0) + jnp.where(lane == i2, g2, 0.0)


def _router(x, nw, shift, scale, rw_pad):
    b, s, d = x.shape
    tm = _pick(s, (512, 256, 128))
    vec = pl.BlockSpec((None, 1, d), lambda bi, i: (bi, 0, 0))
    return pl.pallas_call(
        _router_kernel,
        out_shape=jax.ShapeDtypeStruct((b, s, LANES), F32),
        grid=(b, s // tm),
        in_specs=[pl.BlockSpec((None, tm, d), lambda bi, i: (bi, i, 0)),
                  pl.BlockSpec((1, d), lambda bi, i: (0, 0)), vec, vec,
                  pl.BlockSpec((d, LANES), lambda bi, i: (0, 0))],
        out_specs=pl.BlockSpec((None, tm, LANES), lambda bi, i: (bi, i, 0)),
        compiler_params=_cparams("parallel", "parallel"),
        name="router_top2",
    )(x, nw, shift, scale, rw_pad)


def _top2(logits, lane):
    neg = -jnp.inf
    logits = jnp.where(lane < N_EXPERTS, logits, neg)
    m1 = jnp.max(logits, axis=-1, keepdims=True)
    i1 = jnp.min(jnp.where(logits == m1, lane, LANES), axis=-1, keepdims=True)
    rest = jnp.where(lane == i1, neg, logits)
    m2 = jnp.max(rest, axis=-1, keepdims=True)
    i2 = jnp.min(jnp.where(rest == m2, lane, LANES), axis=-1, keepdims=True)
    e2 = jnp.exp(m2 - m1)
    den = 1.0 + e2
    return i1, i2, 1.0 / den, e2 / den


def _router_logits(h, rw):
    h_hi, h_lo = _split_bf16(h, 2)
    w_hi, w_lo = _split_bf16(rw, 2)
    return _dot(h_hi, w_hi) + _dot(h_hi, w_lo) + _dot(h_lo, w_hi)


def _pack_bf16_pairs(x):
    w = x.shape[1] // 2
    lo = lax.bitcast_convert_type(x[:, :w].astype(BF16).astype(F32), jnp.uint32)
    hi = lax.bitcast_convert_type(x[:, w:].astype(BF16).astype(F32), jnp.uint32)
    return hi | (lo >> 16)


def _unpack_bf16_pairs(p):
    lo = lax.bitcast_convert_type(p << 16, F32)
    hi = lax.bitcast_convert_type(p & jnp.uint32(0xFFFF0000), F32)
    return jnp.concatenate([lo, hi], axis=1)


META_I1, META_I2, META_G1, META_G2, META_R1, META_R2 = range(6)


def _route_kernel(x_ref, nw_ref, sh_ref, sc_ref, rw_ref, meta_ref, cnt_ref, carry):
    @pl.when(pl.program_id(0) == 0)
    def _():
        carry[...] = jnp.zeros(carry.shape, F32)

    h = _norm_mod(x_ref[...], nw_ref[...], sh_ref[...], sc_ref[...])
    logits = _router_logits(h, rw_ref[...])
    tm = logits.shape[0]
    lane = lax.broadcasted_iota(jnp.int32, logits.shape, 1)
    i1, i2, g1, g2 = _top2(logits, lane)
    oh1 = lane == i1
    oh2 = lane == i2
    r = lax.broadcasted_iota(jnp.int32, (tm, tm), 0)
    c = lax.broadcasted_iota(jnp.int32, (tm, tm), 1)
    earlier = jnp.where(c < r, 1.0, 0.0).astype(BF16)
    f1 = jnp.where(oh1, 1.0, 0.0)
    f2 = jnp.where(oh2, 1.0, 0.0)
    before1 = _dot(earlier, f1.astype(BF16))
    before2 = _dot(earlier, f2.astype(BF16))
    c1 = jnp.sum(f1, axis=0, keepdims=True)
    c2 = jnp.sum(f2, axis=0, keepdims=True)
    base = carry[...]
    rank1 = jnp.sum(jnp.where(oh1, before1 + base, 0.0), axis=-1, keepdims=True)
    rank2 = jnp.sum(jnp.where(oh2, before2 + base + c1, 0.0), axis=-1, keepdims=True)
    carry[...] = base + c1 + c2
    cnt_ref[...] = carry[...]
    meta = jnp.zeros(logits.shape, F32)
    for slot, val in ((META_I1, i1.astype(F32)), (META_I2, i2.astype(F32)), (META_G1, g1), (META_G2, g2),
                      (META_R1, rank1), (META_R2, rank2)):
        meta = jnp.where(lane == slot, val, meta)
    meta_ref[...] = meta


def _dispatch_kernel(nt, pos_ref, prev_ref, x_ref, nw_ref, sh_ref, sc_ref, zeros_ref, xs_ref, hbuf, sem):
    del zeros_ref
    i = pl.program_id(0)
    tm = hbuf.shape[1]
    slot = i % 2

    def row_copy(sl, r, p):
        return pltpu.make_async_copy(hbuf.at[sl, pl.ds(r, 1)], xs_ref.at[pl.ds(p, 1)], sem.at[sl])

    def drain(sl, p_ref):
        def body(r, carry):
            row_copy(sl, r, p_ref[0, r]).wait()
            row_copy(sl, r, p_ref[0, tm + r]).wait()
            return carry
        lax.fori_loop(0, tm, body, 0)

    hbuf[slot] = _pack_bf16_pairs(_norm_mod(x_ref[...], nw_ref[...], sh_ref[...], sc_ref[...]))

    def issue(r, carry):
        row_copy(slot, r, pos_ref[0, r]).start()
        row_copy(slot, r, pos_ref[0, tm + r]).start()
        return carry

    lax.fori_loop(0, tm, issue, 0)

    @pl.when(i > 0)
    def _():
        drain(1 - slot, prev_ref)

    @pl.when(i == nt - 1)
    def _():
        drain(slot, pos_ref)


def _expert_kernel(n_f, te_ref, na_ref, xs_ref, wg_ref, wu_ref, wd_ref, ys_ref, h_scr, acc_scr):
    i = pl.program_id(0)
    f = pl.program_id(1)

    @pl.when(i < na_ref[0])
    def _():
        @pl.when(f == 0)
        def _():
            h_scr[...] = _unpack_bf16_pairs(xs_ref[...]).astype(BF16)

        h = h_scr[...]
        a = _silu(_dot(h, wg_ref[...])) * _dot(h, wu_ref[...])
        part = _dot(a.astype(BF16), wd_ref[...])

        @pl.when(f == 0)
        def _():
            acc_scr[...] = part

        @pl.when(f > 0)
        def _():
            acc_scr[...] += part

        @pl.when(f == n_f - 1)
        def _():
            ys_ref[...] = _pack_bf16_pairs(acc_scr[...])

    @pl.when((i >= na_ref[0]) & (f == 0))
    def _():
        ys_ref[...] = jnp.zeros(ys_ref.shape, ys_ref.dtype)


def _combine_kernel(nt, pos_ref, next_ref, ys_ref, meta_ref, x_ref, gate_ref, nw_ref, o_ref, buf, sem):
    i = pl.program_id(0)
    tm = x_ref.shape[0]
    slot = i % 2

    def row_copy(sl, k, r, p):
        return pltpu.make_async_copy(ys_ref.at[pl.ds(p, 1)], buf.at[sl, k, pl.ds(r, 1)], sem.at[sl])

    def issue(sl, p_ref):
        def body(r, carry):
            row_copy(sl, 0, r, p_ref[0, r]).start()
            row_copy(sl, 1, r, p_ref[0, tm + r]).start()
            return carry
        lax.fori_loop(0, tm, body, 0)

    @pl.when(i == 0)
    def _():
        issue(slot, pos_ref)

    @pl.when(i + 1 < nt)
    def _():
        issue(1 - slot, next_ref)

    def drain(r, carry):
        row_copy(slot, 0, r, pos_ref[0, r]).wait()
        row_copy(slot, 1, r, pos_ref[0, tm + r]).wait()
        return carry

    lax.fori_loop(0, tm, drain, 0)
    meta = meta_ref[...]
    lane = lax.broadcasted_iota(jnp.int32, meta.shape, 1)
    g1 = jnp.sum(jnp.where(lane == META_G1, meta, 0.0), axis=-1, keepdims=True)
    g2 = jnp.sum(jnp.where(lane == META_G2, meta, 0.0), axis=-1, keepdims=True)
    y = g1 * _unpack_bf16_pairs(buf[slot, 0]) + g2 * _unpack_bf16_pairs(buf[slot, 1])
    o_ref[...] = x_ref[...] + gate_ref[...] * _rms(y, nw_ref[...])


def _moe_sparse(x, nw1, shift, scale, rw_pad, w_gu, w_down, gate, nw2):
    b, s, d = x.shape
    n = b * s
    tm = _pick(s, (512, 256, 128))
    tpb = s // tm
    nt = n // tm
    n_e, _, f2 = w_gu.shape
    ff = f2 // 2
    tf = _pick(ff, (512, 256))
    n_f = ff // tf
    te_rows = EXPERT_TILE
    n_rows = 2 * n + n_e * te_rows
    nte = n_rows // te_rows

    xrow = pl.BlockSpec((None, tm, d), lambda i: (i // tpb, i % tpb, 0))
    vec = pl.BlockSpec((None, 1, d), lambda i: (i // tpb, 0, 0))
    const = lambda shp: pl.BlockSpec(shp, lambda i: (0, 0))
    mrow = pl.BlockSpec((None, tm, LANES), lambda i: (i // tpb, i % tpb, 0))
    meta, cnt = pl.pallas_call(
        _route_kernel,
        out_shape=[jax.ShapeDtypeStruct((b, s, LANES), F32), jax.ShapeDtypeStruct((1, LANES), F32)],
        grid=(nt,),
        in_specs=[xrow, const((1, d)), vec, vec, const((d, LANES))],
        out_specs=[mrow, const((1, LANES))],
        scratch_shapes=[pltpu.VMEM((1, LANES), F32)],
        compiler_params=_cparams("arbitrary"),
        name="route_top2_rank",
    )(x, nw1, shift, scale, rw_pad)

    m2 = meta.reshape(n, LANES)
    i1 = m2[:, META_I1].astype(jnp.int32)
    i2 = m2[:, META_I2].astype(jnp.int32)
    counts = cnt[0, :n_e].astype(jnp.int32)
    padded = (counts + te_rows - 1) // te_rows * te_rows
    ends = jnp.cumsum(padded)
    starts = ends - padded
    pos1 = jnp.take(starts, i1) + m2[:, META_R1].astype(jnp.int32)
    pos2 = jnp.take(starts, i2) + m2[:, META_R2].astype(jnp.int32)
    pos = jnp.concatenate([pos1.reshape(nt, 1, tm), pos2.reshape(nt, 1, tm)], axis=2)
    n_active = (ends[-1] // te_rows).astype(jnp.int32).reshape(1)
    tile_idx = jnp.minimum(jnp.arange(nte, dtype=jnp.int32), n_active[0] - 1)
    tile_expert = jnp.minimum(jnp.sum((tile_idx * te_rows)[:, None] >= ends[None, :], axis=1), n_e - 1).astype(jnp.int32)

    def pos_block(index):
        return pl.BlockSpec((None, 1, 2 * tm), lambda i: (index(i), 0, 0), memory_space=pltpu.SMEM)

    pos_spec = pos_block(lambda i: i)
    any_spec = pl.BlockSpec(memory_space=pl.ANY)
    dp = d // 2
    xs = pl.pallas_call(
        functools.partial(_dispatch_kernel, nt),
        out_shape=jax.ShapeDtypeStruct((n_rows, dp), jnp.uint32),
        grid=(nt,),
        in_specs=[pos_spec, pos_block(lambda i: jnp.maximum(i - 1, 0)), xrow, const((1, d)), vec, vec, any_spec],
        out_specs=any_spec,
        scratch_shapes=[pltpu.VMEM((2, tm, dp), jnp.uint32), pltpu.SemaphoreType.DMA((2,))],
        input_output_aliases={6: 0},
        compiler_params=_cparams("arbitrary"),
        name="moe_dispatch",
    )(pos, pos, x, nw1, shift, scale, jnp.zeros((n_rows, dp), jnp.uint32))

    def live(i, na):
        return jnp.minimum(i, na[0] - 1)

    def fidx(i, f, na):
        return jnp.where(i < na[0], f, n_f - 1)

    ys = pl.pallas_call(
        functools.partial(_expert_kernel, n_f),
        out_shape=jax.ShapeDtypeStruct((n_rows, dp), jnp.uint32),
        grid_spec=pltpu.PrefetchScalarGridSpec(
            num_scalar_prefetch=2,
            grid=(nte, n_f),
            in_specs=[pl.BlockSpec((te_rows, dp), lambda i, f, te, na: (live(i, na), 0)),
                      pl.BlockSpec((None, d, tf), lambda i, f, te, na: (te[i], 0, fidx(i, f, na))),
                      pl.BlockSpec((None, d, tf), lambda i, f, te, na: (te[i], 0, fidx(i, f, na) + n_f)),
                      pl.BlockSpec((None, tf, d), lambda i, f, te, na: (te[i], fidx(i, f, na), 0))],
            out_specs=pl.BlockSpec((te_rows, dp), lambda i, f, te, na: (i, 0)),
            scratch_shapes=[pltpu.VMEM((te_rows, d), BF16), pltpu.VMEM((te_rows, d), F32)]),
        compiler_params=_cparams("arbitrary", "arbitrary"),
        name="moe_experts",
    )(tile_expert, n_active, xs, w_gu, w_gu, w_down)

    return pl.pallas_call(
        functools.partial(_combine_kernel, nt),
        out_shape=jax.ShapeDtypeStruct((b, s, d), F32),
        grid=(nt,),
        in_specs=[pos_spec, pos_block(lambda i: jnp.minimum(i + 1, nt - 1)), any_spec, mrow, xrow, vec,
                  const((1, d))],
        out_specs=xrow,
        scratch_shapes=[pltpu.VMEM((2, 2, tm, dp), jnp.uint32), pltpu.SemaphoreType.DMA((2,))],
        compiler_params=_cparams("arbitrary"),
        name="moe_combine",
    )(pos, pos, ys, meta, x, gate, nw2)


def _conv_kernel(nt, prev_ref, cur_ref, next_ref, w_ref, b_ref, xs_ref, bm_ref, cm_ref, buf):
    i = pl.program_id(1)
    tm = cur_ref.shape[0]
    halo = SUBLANES
    buf[halo:halo + tm, :] = cur_ref[...]
    buf[0:halo, :] = jnp.where(i > 0, prev_ref[...], 0.0)
    buf[halo + tm:2 * halo + tm, :] = jnp.where(i < nt - 1, next_ref[...], 0.0)
    ext = buf[...]
    n_ext = ext.shape[0]
    down = lambda v: pltpu.roll(v, 1, 0)
    up = lambda v: pltpu.roll(v, n_ext - 1, 0)
    w = [w_ref[j:j + 1, :] for j in range(D_CONV)]
    acc = (w[2] * ext + down(w[1] * ext + down(w[0] * ext)) + up(w[3] * ext + up(w[4] * ext)))
    y = _silu(acc[halo:halo + tm, :] + b_ref[...])
    d_inner = xs_ref.shape[1]
    gn = bm_ref.shape[1]
    xs_ref[...] = y[:, :d_inner]
    bm_ref[...] = y[:, d_inner:d_inner + gn]
    cm_ref[...] = y[:, d_inner + gn:]


def _conv_silu(xbc, conv_w, conv_b, d_inner):
    b, s, c = xbc.shape
    gn = (c - d_inner) // 2
    tm = _pick(s, (512, 256, 128))
    nt = s // tm
    hb = tm // SUBLANES
    last = s // SUBLANES - 1
    row = lambda w: pl.BlockSpec((None, tm, w), lambda bi, i: (bi, i, 0))
    return pl.pallas_call(
        functools.partial(_conv_kernel, nt),
        out_shape=[jax.ShapeDtypeStruct((b, s, d_inner), F32),
                   jax.ShapeDtypeStruct((b, s, gn), F32), jax.ShapeDtypeStruct((b, s, gn), F32)],
        grid=(b, nt),
        in_specs=[pl.BlockSpec((None, SUBLANES, c), lambda bi, i: (bi, jnp.maximum(i * hb - 1, 0), 0)),
                  row(c),
                  pl.BlockSpec((None, SUBLANES, c), lambda bi, i: (bi, jnp.minimum((i + 1) * hb, last), 0)),
                  pl.BlockSpec((D_CONV, c), lambda bi, i: (0, 0)),
                  pl.BlockSpec((1, c), lambda bi, i: (0, 0))],
        out_specs=[row(d_inner), row(gn), row(gn)],
        scratch_shapes=[pltpu.VMEM((tm + 2 * SUBLANES, c), F32)],
        compiler_params=_cparams("parallel", "parallel"),
        name="dwconv_silu",
    )(xbc, xbc, xbc, conv_w, conv_b)


def _softplus(x):
    return jnp.maximum(x, 0.0) + jnp.log(1.0 + jnp.exp(-jnp.abs(x)))


def _ssd_kernel(direction, nc, xs_ref, bm_ref, cm_ref, dt_ref, dtb_ref, a_ref, h0_ref, e_ref,
                y_ref, hT_ref, st_scr):
    c = pl.program_id(1)

    @pl.when(c == 0)
    def _():
        st_scr[...] = h0_ref[...]

    q = CHUNK
    fwd = direction == 0
    dt = _softplus(dt_ref[...] + dtb_ref[...])
    d_a = dt * a_ref[...]
    row = lax.broadcasted_iota(jnp.int32, (q, q), 0)
    col = lax.broadcasted_iota(jnp.int32, (q, q), 1)
    causal = (row >= col) if fwd else (row <= col)
    tri = jnp.where(causal, 1.0, 0.0).astype(BF16)
    a_cum = _dot_split_rhs(tri, d_a, 3)
    a_cum_t = a_cum.T
    expand = e_ref[...]
    dt_x = _dot_split_lhs(dt, expand, 2)
    a_x = _dot_split_lhs(a_cum, expand, 2)
    last = q - 1 if fwd else 0
    a_last = a_x[last:last + 1, :]
    xdt = xs_ref[...] * dt_x
    xdt_b = xdt.astype(BF16)
    w_state = (jnp.exp(a_last - a_x) * xdt).astype(BF16)
    exp_a = jnp.exp(a_x)
    exp_last = jnp.exp(a_last)
    low = lax.broadcasted_iota(jnp.int32, (q, LANES), 1) < SSM_HEAD_DIM
    gw = HEADS_PER_GROUP * SSM_HEAD_DIM
    for g in range(N_GROUPS):
        gs = slice(g * gw, (g + 1) * gw)
        b_g = bm_ref[:, g * D_STATE:(g + 1) * D_STATE]
        c_b = cm_ref[:, g * D_STATE:(g + 1) * D_STATE].astype(BF16)
        cb = lax.dot_general(c_b, b_g.astype(BF16), (((1,), (1,)), ((), ())), preferred_element_type=F32)
        st_g = st_scr[:, gs]
        y_off = _dot(c_b, st_g.astype(BF16)) * exp_a[:, gs]
        s_new = _dot(b_g.T.astype(BF16), w_state[:, gs])
        for pr in range(HEADS_PER_GROUP // 2):
            ms = []
            for j in range(2):
                hl = direction * N_SSM_HEADS + g * HEADS_PER_GROUP + 2 * pr + j
                seg = a_cum[:, hl:hl + 1] - a_cum_t[hl:hl + 1, :]
                dec = jnp.exp(jnp.where(causal, seg, -jnp.inf))
                ms.append((cb * dec).astype(BF16))
            xp = xdt_b[:, g * gw + pr * LANES:g * gw + (pr + 1) * LANES]
            zero = jnp.zeros_like(xp)
            rhs = jnp.concatenate([jnp.where(low, xp, zero), jnp.where(low, zero, xp)], axis=0)
            y_ref[:, g * gw + pr * LANES:g * gw + (pr + 1) * LANES] = (
                _dot(jnp.concatenate(ms, axis=1), rhs) + y_off[:, pr * LANES:(pr + 1) * LANES]).astype(y_ref.dtype)
        st_scr[:, gs] = exp_last[:, gs] * st_g + s_new

    @pl.when(c == nc - 1)
    def _():
        hT_ref[...] = st_scr[...]


def _ssd_scan(direction, xs, bm, cm, dt_raw, dt_bias, a_neg, h0):
    b, s, d_inner = xs.shape
    gn = bm.shape[2]
    nc = s // CHUNK
    heads = np.arange(d_inner) // SSM_HEAD_DIM + direction * N_SSM_HEADS
    expand = jnp.asarray(np.tile(np.arange(LANES)[:, None] == heads[None, :], (2, 1)), BF16)
    if direction == 0:
        cidx = lambda bi, c: (bi, c, 0)
    else:
        cidx = lambda bi, c: (bi, nc - 1 - c, 0)
    const = lambda bi, c: (0, 0)
    st_spec = pl.BlockSpec((None, D_STATE, d_inner), lambda bi, c: (bi, 0, 0))
    return pl.pallas_call(
        functools.partial(_ssd_kernel, direction, nc),
        out_shape=[jax.ShapeDtypeStruct((b, s, d_inner), BF16),
                   jax.ShapeDtypeStruct((b, D_STATE, d_inner), F32)],
        grid=(b, nc),
        in_specs=[pl.BlockSpec((None, CHUNK, d_inner), cidx),
                  pl.BlockSpec((None, CHUNK, gn), cidx), pl.BlockSpec((None, CHUNK, gn), cidx),
                  pl.BlockSpec((None, CHUNK, LANES), cidx),
                  pl.BlockSpec((1, LANES), const), pl.BlockSpec((1, LANES), const),
                  st_spec, pl.BlockSpec((2 * LANES, d_inner), const)],
        out_specs=[pl.BlockSpec((None, CHUNK, d_inner), cidx), st_spec],
        scratch_shapes=[pltpu.VMEM((D_STATE, d_inner), F32)],
        compiler_params=_cparams("parallel", "arbitrary"),
        name="ssd_scan",
    )(xs, bm, cm, dt_raw, dt_bias, a_neg, h0, expand)


def _ssdout_kernel(yf_ref, yb_ref, xs_ref, z_ref, dsk_ref, gnw_ref, w_ref, x_ref, gate_ref, nw_ref, o_ref):
    y = yf_ref[...].astype(F32) + yb_ref[...].astype(F32) + dsk_ref[...] * xs_ref[...]
    gy = y * _silu(z_ref[...].astype(F32))
    gnw = gnw_ref[...]
    gw = gy.shape[1] // N_GROUPS
    parts = []
    for g in range(N_GROUPS):
        parts.append(_rms(gy[:, g * gw:(g + 1) * gw], gnw[:, g * gw:(g + 1) * gw]).astype(BF16))
    out = _dot(jnp.concatenate(parts, axis=1), w_ref[...])
    o_ref[...] = x_ref[...] + gate_ref[...] * _rms(out, nw_ref[...])


def _ssd_out(yf, yb, xs, z, d_skip, gnw, w_out, x, gate, nw):
    b, s, d = x.shape
    d_inner = xs.shape[2]
    tm = _pick(s, (256, 128))
    row = lambda w: pl.BlockSpec((None, tm, w), lambda bi, i: (bi, i, 0))
    const = lambda shp: pl.BlockSpec(shp, lambda bi, i: (0, 0))
    return pl.pallas_call(
        _ssdout_kernel,
        out_shape=jax.ShapeDtypeStruct((b, s, d), F32),
        grid=(b, s // tm),
        in_specs=[row(d_inner), row(d_inner), row(d_inner), row(d_inner),
                  const((1, d_inner)), const((1, d_inner)), const(w_out.shape), row(d),
                  pl.BlockSpec((None, 1, d), lambda bi, i: (bi, 0, 0)), const((1, d))],
        out_specs=row(d),
        compiler_params=_cparams("parallel", "parallel"),
        name="ssd_gate_out_residual",
    )(yf, yb, xs, z, d_skip, gnw, w_out, x, gate, nw)


def _rope_tables(seq):
    rows = seq // GRID_W
    row = jnp.repeat(jnp.arange(rows, dtype=F32), GRID_W)
    col = jnp.tile(jnp.arange(GRID_W, dtype=F32), rows)
    freqs = ROPE_THETA ** (-jnp.arange(0, AXIS_ROT_DIM, 2, dtype=F32) / AXIS_ROT_DIM)
    ang = jnp.concatenate([row[:, None] * freqs, col[:, None] * freqs], axis=-1)
    cos = jnp.repeat(jnp.cos(ang), 2, axis=-1)
    sin = jnp.repeat(jnp.sin(ang), 2, axis=-1) * jnp.tile(jnp.asarray([-1.0, 1.0], F32), HEAD_DIM // 2)
    return jnp.tile(cos, (1, LANES // HEAD_DIM)), jnp.tile(sin, (1, LANES // HEAD_DIM))


def _mod_vectors(mods_i, batch):
    d = mods_i.shape[1] // 6
    mx = [mods_i[:batch, k * d:(k + 1) * d].reshape(batch, 1, d) for k in range(6)]
    mc = [jnp.broadcast_to(mods_i[batch, k * d:(k + 1) * d].reshape(1, 1, d), (batch, 1, d)) for k in range(6)]
    return mx, mc


def _even_mixer(x, ctx, nw0, nw1, mx, mc, w_in, qnw, knw, w_out, rope, need_ctx):
    w_u = w_in[:, :F_WIDTH].astype(BF16)
    w_qkv = w_in[:, F_WIDTH:].astype(BF16)
    qw = jnp.tile(qnw.reshape(1, HEAD_DIM), (1, LANES // HEAD_DIM))
    kw = jnp.tile(knw.reshape(1, HEAD_DIM), (1, LANES // HEAD_DIM))
    w_f = w_out[:F_WIDTH].astype(BF16)
    d = w_out.shape[1]
    w_a = (w_out[F_WIDTH:].reshape(N_KV_HEADS, GQ, HEAD_DIM, d).transpose(1, 0, 2, 3)
           .reshape(Q_WIDTH, d).astype(BF16))
    lc = ctx.shape[1]
    ones = jnp.ones((lc, LANES), F32)
    zeros = jnp.zeros((lc, LANES), F32)

    u_c, qkv_c = _norm_mod_linear(ctx, nw0, mc[0], mc[1], [w_u, w_qkv], [F32, F32])
    q_c, k_c, v_c = _qk_prepare(qkv_c, ones, zeros, qw, kw)
    u_x, qkv_x = _norm_mod_linear(x, nw0, mx[0], mx[1], [w_u, w_qkv], [F32, F32])
    q_x, k_x, v_x = _qk_prepare(qkv_x, rope[0], rope[1], qw, kw)
    k_all = jnp.concatenate([k_c, k_x], axis=1)
    v_all = jnp.concatenate([v_c, v_x], axis=1)
    a_x = _attention(q_x, k_all, v_all)
    f_x = _fourier_mix_long(u_x)
    x_new = _mixer_out(f_x, a_x, w_f, w_a, x, mx[2], nw1)
    if not need_ctx:
        return x_new, ctx
    a_c = _attention(q_c, k_c, v_c)
    f_c = _fourier_mix_short(u_c)
    return x_new, _mixer_out(f_c, a_c, w_f, w_a, ctx, mc[2], nw1)


def _ssd_branch(h_in, nw0, shift, scale, w_z, w_xbc, w_dt, conv_w, conv_b, dt_bias, a_neg, h0_f, h0_b):
    z, xbc, dt_raw = _norm_mod_linear(h_in, nw0, shift, scale, [w_z, w_xbc, w_dt], [BF16, F32, F32], tm_max=256)
    xs, bm, cm = _conv_silu(xbc, conv_w, conv_b, w_z.shape[1])
    y_f, s_f = _ssd_scan(0, xs, bm, cm, dt_raw, dt_bias, a_neg, h0_f)
    y_b, s_b = _ssd_scan(1, xs, bm, cm, dt_raw, dt_bias, a_neg, h0_b)
    return z, xs, y_f, y_b, s_f, s_b


def _odd_mixer(x, ctx, nw0, nw1, mx, mc, w_in, conv_w, conv_b, dt_bias, a_log, d_skip, gnw, w_out, need_ctx):
    d_inner = w_out.shape[0]
    conv_dim = conv_w.shape[1]
    w_z = w_in[:, :d_inner].astype(BF16)
    w_xbc = w_in[:, d_inner:d_inner + conv_dim].astype(BF16)
    n_dt = w_in.shape[1] - d_inner - conv_dim
    w_dt = jnp.pad(w_in[:, d_inner + conv_dim:], ((0, 0), (0, LANES - n_dt))).astype(BF16)
    dtb = jnp.pad(dt_bias.reshape(1, n_dt), ((0, 0), (0, LANES - n_dt)))
    a_neg = jnp.pad(-jnp.exp(a_log.astype(F32)).reshape(1, n_dt), ((0, 0), (0, LANES - n_dt)))
    cb = conv_b.reshape(1, conv_dim)
    dsk = jnp.repeat(d_skip, SSM_HEAD_DIM).reshape(1, d_inner)
    gw = gnw.reshape(1, d_inner)
    w_o = w_out.astype(BF16)
    b = x.shape[0]
    zeros = jnp.zeros((b, D_STATE, d_inner), F32)

    z_c, xs_c, yf_c, yb_c, s_f, s_b = _ssd_branch(ctx, nw0, mc[0], mc[1], w_z, w_xbc, w_dt, conv_w, cb,
                                                  dtb, a_neg, zeros, zeros)
    z_x, xs_x, yf_x, yb_x, _, _ = _ssd_branch(x, nw0, mx[0], mx[1], w_z, w_xbc, w_dt, conv_w, cb,
                                              dtb, a_neg, s_f, s_b)
    x_new = _ssd_out(yf_x, yb_x, xs_x, z_x, dsk, gw, w_o, x, mx[2], nw1)
    if not need_ctx:
        return x_new, ctx
    return x_new, _ssd_out(yf_c, yb_c, xs_c, z_c, dsk, gw, w_o, ctx, mc[2], nw1)


def kernel(x, c, ctx, c_ctx, ada_w, ada_b, norm_w, mix_in_w, q_norm_w, k_norm_w, mix_out_w, ffn_gu_w, ffn_down_w, ssm_in_w, ssm_conv_w, ssm_conv_b, ssm_dt_bias, ssm_a_log, ssm_d, ssm_norm_w, ssm_out_w, router_w, moe_gu_w, moe_down_w):
    batch, seq, d = x.shape
    depth = ada_w.shape[0]
    rope = _rope_tables(seq)
    cvec = jnp.zeros((SUBLANES, d), F32).at[:batch].set(c).at[batch].set(c_ctx)
    mods = _modulation(cvec, ada_w, ada_b)
    for i in range(depth):
        last = i == depth - 1
        j = i // 2
        mx, mc = _mod_vectors(mods[i], batch)
        nw = [norm_w[i, k].reshape(1, d) for k in range(4)]
        if i % 2 == 0:
            x, ctx_mid = _even_mixer(x, ctx, nw[0], nw[1], mx, mc, mix_in_w[j], q_norm_w[j], k_norm_w[j],
                                     mix_out_w[j], rope, not last)
            w_gu = ffn_gu_w[j].astype(BF16)[None]
            w_dn = ffn_down_w[j].astype(BF16)[None]
            x = _ffn(x, nw[2], mx[3], mx[4], w_gu, w_dn, mx[5], nw[3])
            if not last:
                ctx = _ffn(ctx_mid, nw[2], mc[3], mc[4], w_gu, w_dn, mc[5], nw[3])
        else:
            x, ctx_mid = _odd_mixer(x, ctx, nw[0], nw[1], mx, mc, ssm_in_w[j], ssm_conv_w[j], ssm_conv_b[j],
                                    ssm_dt_bias[j], ssm_a_log[j], ssm_d[j], ssm_norm_w[j], ssm_out_w[j],
                                    not last)
            rw = jnp.pad(router_w[j], ((0, 0), (0, LANES - N_EXPERTS)))
            w_gu = moe_gu_w[j].astype(BF16)
            w_dn = moe_down_w[j].astype(BF16)
            x = _moe_sparse(x, nw[2], mx[3], mx[4], rw, w_gu, w_dn, mx[5], nw[3])
            if not last:
                comb_c = _router(ctx_mid, nw[2], mc[3], mc[4], rw)
                ctx = _ffn(ctx_mid, nw[2], mc[3], mc[4], w_gu, w_dn, mc[5], nw[3], comb_c)
    return x
```

```python
import functools

import numpy as np
import jax
import jax.numpy as jnp
from jax import lax
from jax.experimental import pallas as pl
from jax.experimental.pallas import tpu as pltpu

F32 = jnp.float32
BF16 = jnp.bfloat16

EPS = 1e-6
GRID_W = 64
HEAD_DIM = 64
N_HEADS = 8
N_KV_HEADS = 2
GQ = N_HEADS // N_KV_HEADS
F_GROUPS = 4
F_GROUP_DIM = 128
F_WIDTH = F_GROUPS * F_GROUP_DIM
Q_WIDTH = N_HEADS * HEAD_DIM
KV_WIDTH = N_KV_HEADS * HEAD_DIM
ROPE_THETA = 10000.0
AXIS_ROT_DIM = HEAD_DIM // 2
ATTN_SCALE = HEAD_DIM ** -0.5
LOG2_E = 1.4426950408889634
SSM_HEAD_DIM = 64
N_SSM_HEADS = 32
N_GROUPS = 4
HEADS_PER_GROUP = N_SSM_HEADS // N_GROUPS
D_STATE = 128
D_CONV = 5
CONV_PAD = D_CONV // 2
CHUNK = 128
N_EXPERTS = 8
EXPERT_TILE = 1024
LANES = 128
SUBLANES = 8
FFT_N2 = 128
VMEM_LIMIT = 56 * 1024 * 1024


def _cparams(*sem):
    return pltpu.CompilerParams(dimension_semantics=sem, vmem_limit_bytes=VMEM_LIMIT)


def _pick(n, candidates):
    for c in candidates:
        if n % c == 0:
            return c
    return n


def _sigmoid(x):
    return 1.0 / (1.0 + jnp.exp(-x))


def _silu(x):
    return x * _sigmoid(x)


def _rms(x, w):
    return x * lax.rsqrt(jnp.mean(x * x, axis=-1, keepdims=True) + EPS) * w


def _norm_mod(x, nw, shift, scale):
    return _rms(x, nw) * (1.0 + scale) + shift


def _dot(a, b):
    return jnp.dot(a, b, preferred_element_type=F32)


def _split_bf16(x, parts):
    out = []
    r = x
    for _ in range(parts):
        p = r.astype(BF16)
        out.append(p)
        r = r - p.astype(F32)
    return out


def _dot_split_lhs(x, m_stacked, parts):
    return _dot(jnp.concatenate(_split_bf16(x, parts), axis=1), m_stacked)


def _dot_split_rhs(m, x, parts):
    acc = None
    for p in _split_bf16(x, parts):
        t = _dot(m, p)
        acc = t if acc is None else acc + t
    return acc


def _mod_kernel(c_ref, w_ref, b_ref, o_ref):
    s = _silu(c_ref[...])
    o_ref[...] = jnp.dot(s, w_ref[...], preferred_element_type=F32,
                         precision=lax.Precision.HIGHEST) + b_ref[...]


def _modulation(cvec, ada_w, ada_b):
    depth, d, n = ada_w.shape
    tn = _pick(n, (1536, 1024, 512))
    rows = cvec.shape[0]
    return pl.pallas_call(
        _mod_kernel,
        out_shape=jax.ShapeDtypeStruct((depth, rows, n), F32),
        grid=(depth, n // tn),
        in_specs=[pl.BlockSpec((rows, d), lambda i, j: (0, 0)),
                  pl.BlockSpec((None, d, tn), lambda i, j: (i, 0, j)),
                  pl.BlockSpec((None, 1, tn), lambda i, j: (i, 0, j))],
        out_specs=pl.BlockSpec((None, rows, tn), lambda i, j: (i, 0, j)),
        compiler_params=_cparams("parallel", "parallel"),
        name="adaln_modulation",
    )(cvec, ada_w, ada_b.reshape(depth, 1, n))


def _nml_kernel(n_w, x_ref, nw_ref, sh_ref, sc_ref, *refs):
    h = _norm_mod(x_ref[...], nw_ref[...], sh_ref[...], sc_ref[...]).astype(BF16)
    for w_ref, o_ref in zip(refs[:n_w], refs[n_w:]):
        o_ref[...] = _dot(h, w_ref[...]).astype(o_ref.dtype)


def _norm_mod_linear(x, nw, shift, scale, weights, out_dtypes, tm_max=512):
    b, s, d = x.shape
    tm = _pick(s, (tm_max, 256, 128))
    n_w = len(weights)
    vec = pl.BlockSpec((None, 1, d), lambda bi, i: (bi, 0, 0))
    in_specs = [pl.BlockSpec((None, tm, d), lambda bi, i: (bi, i, 0)),
                pl.BlockSpec((1, d), lambda bi, i: (0, 0)), vec, vec]
    in_specs += [pl.BlockSpec(w.shape, lambda bi, i: (0, 0)) for w in weights]
    out_shape = [jax.ShapeDtypeStruct((b, s, w.shape[1]), dt) for w, dt in zip(weights, out_dtypes)]
    out_specs = [pl.BlockSpec((None, tm, w.shape[1]), lambda bi, i: (bi, i, 0)) for w in weights]
    return pl.pallas_call(
        functools.partial(_nml_kernel, n_w),
        out_shape=out_shape,
        grid=(b, s // tm),
        in_specs=in_specs,
        out_specs=out_specs,
        compiler_params=_cparams("parallel", "parallel"),
        name="norm_mod_linear",
    )(x, nw, shift, scale, *weights)


def _qkprep_kernel(qkv_ref, cos_ref, sin_ref, qw_ref, kw_ref, bd_ref, q_ref, k_ref, v_ref):
    tm = qkv_ref.shape[0]
    lane = lax.broadcasted_iota(jnp.int32, (tm, LANES), 1)
    even = (lane % 2) == 0
    low = lane < HEAD_DIM
    cos = cos_ref[...]
    sin = sin_ref[...]
    bd = bd_ref[...]

    def prep(xb, w):
        ms = _dot_split_lhs(xb * xb, bd, 2)
        y = xb * lax.rsqrt(ms + EPS) * w
        swapped = jnp.where(even, pltpu.roll(y, LANES - 1, 1), pltpu.roll(y, 1, 1))
        return y * cos + swapped * sin

    qw = qw_ref[...]
    for cblk in range(Q_WIDTH // LANES):
        r = prep(qkv_ref[:, cblk * LANES:(cblk + 1) * LANES], qw) * (ATTN_SCALE * LOG2_E)
        r_sw = pltpu.roll(r, HEAD_DIM, 1)
        kvh = (2 * cblk) // GQ
        keep = low if kvh == 0 else jnp.logical_not(low)
        for j in range(2):
            src = r if j == kvh else r_sw
            q_ref[2 * cblk + j] = jnp.where(keep, src, 0.0).astype(q_ref.dtype)
    k_ref[...] = prep(qkv_ref[:, Q_WIDTH:Q_WIDTH + KV_WIDTH], kw_ref[...]).astype(k_ref.dtype)
    v_ref[...] = qkv_ref[:, Q_WIDTH + KV_WIDTH:Q_WIDTH + 2 * KV_WIDTH].astype(v_ref.dtype)


def _qk_prepare(qkv, cos_t, sin_t, qw, kw):
    b, s, _ = qkv.shape
    tm = _pick(s, (512, 256, 128))
    bd = np.kron(np.eye(2, dtype=np.float32), np.full((HEAD_DIM, HEAD_DIM), 1.0 / HEAD_DIM, np.float32))
    bd = np.tile(bd, (2, 1))
    row = lambda bi, i: (i, 0)
    const = lambda bi, i: (0, 0)
    return pl.pallas_call(
        _qkprep_kernel,
        out_shape=[jax.ShapeDtypeStruct((b, N_HEADS, s, LANES), BF16),
                   jax.ShapeDtypeStruct((b, s, KV_WIDTH), BF16),
                   jax.ShapeDtypeStruct((b, s, KV_WIDTH), BF16)],
        grid=(b, s // tm),
        in_specs=[pl.BlockSpec((None, tm, qkv.shape[2]), lambda bi, i: (bi, i, 0)),
                  pl.BlockSpec((tm, LANES), row), pl.BlockSpec((tm, LANES), row),
                  pl.BlockSpec((1, LANES), const), pl.BlockSpec((1, LANES), const),
                  pl.BlockSpec((2 * LANES, LANES), const)],
        out_specs=[pl.BlockSpec((None, N_HEADS, tm, LANES), lambda bi, i: (bi, 0, i, 0)),
                   pl.BlockSpec((None, tm, KV_WIDTH), lambda bi, i: (bi, i, 0)),
                   pl.BlockSpec((None, tm, KV_WIDTH), lambda bi, i: (bi, i, 0))],
        compiler_params=_cparams("parallel", "parallel"),
        name="qk_norm_rope",
    )(qkv, cos_t, sin_t, qw, kw, jnp.asarray(bd, BF16))


def _attn_kernel(nk, q_ref, k_ref, v_ref, o_ref, m_scr, l_scr, acc_scr):
    kv = pl.program_id(2)

    @pl.when(kv == 0)
    def _():
        m_scr[...] = jnp.full(m_scr.shape, -jnp.inf, F32)
        l_scr[...] = jnp.zeros(l_scr.shape, F32)
        acc_scr[...] = jnp.zeros(acc_scr.shape, F32)

    k = k_ref[...]
    v = v_ref[...]
    n_t = k.shape[0] // LANES

    def head(hd, carry):
        s = lax.dot_general(q_ref[hd], k, (((1,), (1,)), ((), ())), preferred_element_type=F32)
        tiles = [s[:, t * LANES:(t + 1) * LANES] for t in range(n_t)]
        smax = tiles[0]
        for t in range(1, n_t):
            smax = jnp.maximum(smax, tiles[t])
        m_prev = m_scr[hd]
        m_new = jnp.maximum(m_prev, jnp.max(smax, axis=-1, keepdims=True))
        alpha = jnp.exp2(m_prev - m_new)
        ps = []
        lsum = None
        for t in range(n_t):
            p_t = jnp.exp2(tiles[t] - m_new)
            lsum = p_t if lsum is None else lsum + p_t
            ps.append(p_t.astype(BF16))
        l_scr[hd] = alpha * l_scr[hd] + lsum
        acc_scr[hd] = alpha * acc_scr[hd] + _dot(jnp.concatenate(ps, axis=1), v)
        m_scr[hd] = m_new
        return carry

    lax.fori_loop(0, N_HEADS, head, 0, unroll=8)

    @pl.when(kv == nk - 1)
    def _():
        tq = o_ref.shape[0]
        low = lax.broadcasted_iota(jnp.int32, (tq, LANES), 1) < HEAD_DIM
        for g in range(GQ):
            o0 = acc_scr[g] / jnp.sum(l_scr[g], axis=-1, keepdims=True)
            o1 = acc_scr[GQ + g] / jnp.sum(l_scr[GQ + g], axis=-1, keepdims=True)
            o_ref[:, g * LANES:(g + 1) * LANES] = jnp.where(low, o0, o1).astype(o_ref.dtype)


def _attention(q, k, v):
    b, _, s, _ = q.shape
    skv = k.shape[1]
    tq = _pick(s, (1024, 512, 256, 128))
    tk = _pick(skv, (1280, 640, 256, 128))
    nk = skv // tk
    return pl.pallas_call(
        functools.partial(_attn_kernel, nk),
        out_shape=jax.ShapeDtypeStruct((b, s, Q_WIDTH), BF16),
        grid=(b, s // tq, nk),
        in_specs=[pl.BlockSpec((None, N_HEADS, tq, LANES), lambda bi, i, j: (bi, 0, i, 0)),
                  pl.BlockSpec((None, tk, KV_WIDTH), lambda bi, i, j: (bi, j, 0)),
                  pl.BlockSpec((None, tk, KV_WIDTH), lambda bi, i, j: (bi, j, 0))],
        out_specs=pl.BlockSpec((None, tq, Q_WIDTH), lambda bi, i, j: (bi, i, 0)),
        scratch_shapes=[pltpu.VMEM((N_HEADS, tq, LANES), F32), pltpu.VMEM((N_HEADS, tq, LANES), F32),
                        pltpu.VMEM((N_HEADS, tq, LANES), F32)],
        compiler_params=_cparams("parallel", "parallel", "arbitrary"),
        name="flash_attention",
    )(q, k, v)


def _dft_cos_sin(n):
    idx = np.arange(n, dtype=np.int64)
    ang = 2.0 * np.pi * ((idx[:, None] * idx[None, :]) % n).astype(np.float64) / n
    return np.cos(ang), np.sin(ang)


def _fft1_kernel(x_ref, w_ref, yr_ref, yi_ref):
    n1 = x_ref.shape[0]
    fw = yr_ref.shape[2]
    x = jnp.concatenate([x_ref[:, jj, :] for jj in range(x_ref.shape[1])], axis=1)
    y = _dot(w_ref[...], x.astype(BF16))
    for jj in range(yr_ref.shape[1]):
        yr_ref[:, jj, :] = y[:n1, jj * fw:(jj + 1) * fw]
        yi_ref[:, jj, :] = y[n1:, jj * fw:(jj + 1) * fw]


def _fft2_kernel(kb, scale, yr_ref, yi_ref, tc_ref, ts_ref, w2_ref, wc_ref, o_ref):
    n2 = FFT_N2
    w2 = w2_ref[...]
    wc = wc_ref[...]
    for j in range(kb):
        yr = yr_ref[j]
        yi = yi_ref[j]
        c = tc_ref[:, j:j + 1]
        s = ts_ref[:, j:j + 1]
        z = jnp.concatenate([yr * c + yi * s, yi * c - yr * s], axis=0).astype(BF16)
        gq = _dot(w2, z)
        for g in range(F_GROUPS):
            cols = slice(g * F_GROUP_DIM, (g + 1) * F_GROUP_DIM)
            gg = jnp.concatenate([gq[:n2, cols], gq[n2:, cols]], axis=1).astype(BF16)
            o_ref[:, j, g * F_GROUP_DIM:(g + 1) * F_GROUP_DIM] = _dot(gg, wc) * scale


def _fourier_mix_long(u):
    b, l, fw = u.shape
    n2 = FFT_N2
    n1 = l // n2
    cols = n2 * fw
    c1, s1 = _dft_cos_sin(n1)
    w1 = jnp.asarray(np.concatenate([c1, -s1], axis=0), BF16)
    c2, s2 = _dft_cos_sin(n2)
    w2 = jnp.asarray(np.block([[c2, s2], [-s2, c2]]), BF16)
    cc, sc = _dft_cos_sin(F_GROUP_DIM)
    wc = jnp.asarray(np.concatenate([cc, sc], axis=0), BF16)
    k1 = np.arange(n1, dtype=np.int64)
    nn = np.arange(n2, dtype=np.int64)
    ang = 2.0 * np.pi * ((nn[:, None] * k1[None, :]) % l).astype(np.float64) / l
    kb = SUBLANES
    tw_c = jnp.asarray(np.cos(ang).reshape(n2, n1 // kb, kb).transpose(1, 0, 2), F32)
    tw_s = jnp.asarray(np.sin(ang).reshape(n2, n1 // kb, kb).transpose(1, 0, 2), F32)
    tn = _pick(cols, (4096,))
    x2 = u.reshape(b, n1, n2, fw)
    yr, yi = pl.pallas_call(
        _fft1_kernel,
        out_shape=[jax.ShapeDtypeStruct((b, n1, n2, fw), F32)] * 2,
        grid=(b, cols // tn),
        in_specs=[pl.BlockSpec((None, n1, tn // fw, fw), lambda bi, j: (bi, 0, j, 0)),
                  pl.BlockSpec((2 * n1, n1), lambda bi, j: (0, 0))],
        out_specs=[pl.BlockSpec((None, n1, tn // fw, fw), lambda bi, j: (bi, 0, j, 0))] * 2,
        compiler_params=_cparams("parallel", "parallel"),
        name="fft_stage1",
    )(x2, w1)
    scale = float(1.0 / np.sqrt(float(l) * F_GROUP_DIM))
    yblk = pl.BlockSpec((None, kb, n2, fw), lambda bi, j: (bi, j, 0, 0))
    tblk = pl.BlockSpec((None, n2, kb), lambda bi, j: (j, 0, 0))
    out = pl.pallas_call(
        functools.partial(_fft2_kernel, kb, scale),
        out_shape=jax.ShapeDtypeStruct((b, n2, n1, fw), F32),
        grid=(b, n1 // kb),
        in_specs=[yblk, yblk, tblk, tblk,
                  pl.BlockSpec((2 * n2, 2 * n2), lambda bi, j: (0, 0)),
                  pl.BlockSpec((2 * F_GROUP_DIM, F_GROUP_DIM), lambda bi, j: (0, 0))],
        out_specs=pl.BlockSpec((None, n2, kb, fw), lambda bi, j: (bi, 0, j, 0)),
        compiler_params=_cparams("parallel", "parallel"),
        name="fft_stage2",
    )(yr, yi, tw_c, tw_s, w2, wc)
    return out.reshape(b, l, fw)


def _fftc_kernel(scale, u_ref, wl_ref, wc_ref, o_ref):
    fw = u_ref.shape[1]
    ab = _dot(u_ref[...].astype(BF16), wc_ref[...])
    st = jnp.concatenate([ab[:, :fw], ab[:, fw:]], axis=0).astype(BF16)
    o_ref[...] = _dot(wl_ref[...], st) * scale


def _fourier_mix_short(u):
    b, l, fw = u.shape
    cl, sl = _dft_cos_sin(l)
    wl = jnp.asarray(np.concatenate([cl, -sl], axis=1), BF16)
    cc, sc = _dft_cos_sin(F_GROUP_DIM)
    eye = np.eye(F_GROUPS)
    wc = jnp.asarray(np.concatenate([np.kron(eye, cc), np.kron(eye, sc)], axis=1), BF16)
    scale = float(1.0 / np.sqrt(float(l) * F_GROUP_DIM))
    return pl.pallas_call(
        functools.partial(_fftc_kernel, scale),
        out_shape=jax.ShapeDtypeStruct((b, l, fw), F32),
        grid=(b,),
        in_specs=[pl.BlockSpec((None, l, fw), lambda bi: (bi, 0, 0)),
                  pl.BlockSpec((l, 2 * l), lambda bi: (0, 0)),
                  pl.BlockSpec((fw, 2 * fw), lambda bi: (0, 0))],
        out_specs=pl.BlockSpec((None, l, fw), lambda bi: (bi, 0, 0)),
        compiler_params=_cparams("parallel"),
        name="fft_direct",
    )(u, wl, wc)


def _mixout_kernel(f_ref, a_ref, wf_ref, wa_ref, x_ref, gate_ref, nw_ref, o_ref):
    y = _dot(f_ref[...].astype(BF16), wf_ref[...]) + _dot(a_ref[...].astype(BF16), wa_ref[...])
    o_ref[...] = x_ref[...] + gate_ref[...] * _rms(y, nw_ref[...])


def _mixer_out(f, a, wf, wa, x, gate, nw):
    b, s, d = x.shape
    tm = _pick(s, (512, 256, 128))
    row = lambda w: pl.BlockSpec((None, tm, w), lambda bi, i: (bi, i, 0))
    const = lambda shp: pl.BlockSpec(shp, lambda bi, i: (0, 0))
    return pl.pallas_call(
        _mixout_kernel,
        out_shape=jax.ShapeDtypeStruct((b, s, d), F32),
        grid=(b, s // tm),
        in_specs=[row(f.shape[2]), row(a.shape[2]), const(wf.shape), const(wa.shape), row(d),
                  pl.BlockSpec((None, 1, d), lambda bi, i: (bi, 0, 0)), const((1, d))],
        out_specs=row(d),
        compiler_params=_cparams("parallel", "parallel"),
        name="mixer_out_residual",
    )(f, a, wf, wa, x, gate, nw)


def _ffn_kernel(n_e, n_f, use_comb, *refs):
    if use_comb:
        (x_ref, nw1_ref, sh_ref, sc_ref, wg_ref, wu_ref, wd_ref, gate_ref, nw2_ref, comb_ref,
         o_ref, h_scr, acc_scr) = refs
    else:
        (x_ref, nw1_ref, sh_ref, sc_ref, wg_ref, wu_ref, wd_ref, gate_ref, nw2_ref,
         o_ref, h_scr, acc_scr) = refs
    e = pl.program_id(2)
    f = pl.program_id(3)

    @pl.when((e == 0) & (f == 0))
    def _():
        h_scr[...] = _norm_mod(x_ref[...], nw1_ref[...], sh_ref[...], sc_ref[...]).astype(BF16)
        acc_scr[...] = jnp.zeros(acc_scr.shape, F32)

    h = h_scr[...]
    a = _silu(_dot(h, wg_ref[...])) * _dot(h, wu_ref[...])
    if use_comb:
        comb = comb_ref[...]
        lane = lax.broadcasted_iota(jnp.int32, comb.shape, 1)
        a = a * jnp.sum(jnp.where(lane == e, comb, 0.0), axis=-1, keepdims=True)
    acc_scr[...] += _dot(a.astype(BF16), wd_ref[...])

    @pl.when((e == n_e - 1) & (f == n_f - 1))
    def _():
        o_ref[...] = x_ref[...] + gate_ref[...] * _rms(acc_scr[...], nw2_ref[...])


def _ffn(x, nw1, shift, scale, w_gu, w_down, gate, nw2, comb=None):
    b, s, d = x.shape
    n_e, _, f2 = w_gu.shape
    ff = f2 // 2
    tm = _pick(s, (1024, 512, 256))
    tf = _pick(ff, (512, 256))
    n_f = ff // tf
    vec = pl.BlockSpec((None, 1, d), lambda bi, i, e, f: (bi, 0, 0))
    const = pl.BlockSpec((1, d), lambda bi, i, e, f: (0, 0))
    row = pl.BlockSpec((None, tm, d), lambda bi, i, e, f: (bi, i, 0))
    in_specs = [row, const, vec, vec,
                pl.BlockSpec((None, d, tf), lambda bi, i, e, f: (e, 0, f)),
                pl.BlockSpec((None, d, tf), lambda bi, i, e, f: (e, 0, f + n_f)),
                pl.BlockSpec((None, tf, d), lambda bi, i, e, f: (e, f, 0)),
                vec, const]
    args = [x, nw1, shift, scale, w_gu, w_gu, w_down, gate, nw2]
    if comb is not None:
        in_specs.append(pl.BlockSpec((None, tm, LANES), lambda bi, i, e, f: (bi, i, 0)))
        args.append(comb)
    return pl.pallas_call(
        functools.partial(_ffn_kernel, n_e, n_f, comb is not None),
        out_shape=jax.ShapeDtypeStruct((b, s, d), F32),
        grid=(b, s // tm, n_e, n_f),
        in_specs=in_specs,
        out_specs=row,
        scratch_shapes=[pltpu.VMEM((tm, d), BF16), pltpu.VMEM((tm, d), F32)],
        compiler_params=_cparams("parallel", "parallel", "arbitrary", "arbitrary"),
        name="swiglu_residual",
    )(*args)


def _router_kernel(x_ref, nw_ref, sh_ref, sc_ref, rw_ref, comb_ref):
    h = _norm_mod(x_ref[...], nw_ref[...], sh_ref[...], sc_ref[...])
    logits = _router_logits(h, rw_ref[...])
    lane = lax.broadcasted_iota(jnp.int32, logits.shape, 1)
    i1, i2, g1, g2 = _top2(logits, lane)
    comb_ref[...] = jnp.where(lane == i1, g1, 0.0) + jnp.where(lane == i2, g2, 0.0)


def _router(x, nw, shift, scale, rw_pad):
    b, s, d = x.shape
    tm = _pick(s, (512, 256, 128))
    vec = pl.BlockSpec((None, 1, d), lambda bi, i: (bi, 0, 0))
    return pl.pallas_call(
        _router_kernel,
        out_shape=jax.ShapeDtypeStruct((b, s, LANES), F32),
        grid=(b, s // tm),
        in_specs=[pl.BlockSpec((None, tm, d), lambda bi, i: (bi, i, 0)),
                  pl.BlockSpec((1, d), lambda bi, i: (0, 0)), vec, vec,
                  pl.BlockSpec((d, LANES), lambda bi, i: (0, 0))],
        out_specs=pl.BlockSpec((None, tm, LANES), lambda bi, i: (bi, i, 0)),
        compiler_params=_cparams("parallel", "parallel"),
        name="router_top2",
    )(x, nw, shift, scale, rw_pad)


def _top2(logits, lane):
    neg = -jnp.inf
    logits = jnp.where(lane < N_EXPERTS, logits, neg)
    m1 = jnp.max(logits, axis=-1, keepdims=True)
    i1 = jnp.min(jnp.where(logits == m1, lane, LANES), axis=-1, keepdims=True)
    rest = jnp.where(lane == i1, neg, logits)
    m2 = jnp.max(rest, axis=-1, keepdims=True)
    i2 = jnp.min(jnp.where(rest == m2, lane, LANES), axis=-1, keepdims=True)
    e2 = jnp.exp(m2 - m1)
    den = 1.0 + e2
    return i1, i2, 1.0 / den, e2 / den


def _router_logits(h, rw):
    h_hi, h_lo = _split_bf16(h, 2)
    w_hi, w_lo = _split_bf16(rw, 2)
    return _dot(h_hi, w_hi) + _dot(h_hi, w_lo) + _dot(h_lo, w_hi)


META_I1, META_I2, META_G1, META_G2, META_R1, META_R2 = range(6)


def _route_kernel(x_ref, nw_ref, sh_ref, sc_ref, rw_ref, meta_ref, cnt_ref, carry):
    @pl.when(pl.program_id(0) == 0)
    def _():
        carry[...] = jnp.zeros(carry.shape, F32)

    h = _norm_mod(x_ref[...], nw_ref[...], sh_ref[...], sc_ref[...])
    logits = _router_logits(h, rw_ref[...])
    tm = logits.shape[0]
    lane = lax.broadcasted_iota(jnp.int32, logits.shape, 1)
    i1, i2, g1, g2 = _top2(logits, lane)
    oh1 = lane == i1
    oh2 = lane == i2
    r = lax.broadcasted_iota(jnp.int32, (tm, tm), 0)
    c = lax.broadcasted_iota(jnp.int32, (tm, tm), 1)
    earlier = jnp.where(c < r, 1.0, 0.0).astype(BF16)
    f1 = jnp.where(oh1, 1.0, 0.0)
    f2 = jnp.where(oh2, 1.0, 0.0)
    before1 = _dot(earlier, f1.astype(BF16))
    before2 = _dot(earlier, f2.astype(BF16))
    c1 = jnp.sum(f1, axis=0, keepdims=True)
    c2 = jnp.sum(f2, axis=0, keepdims=True)
    base = carry[...]
    rank1 = jnp.sum(jnp.where(oh1, before1 + base, 0.0), axis=-1, keepdims=True)
    rank2 = jnp.sum(jnp.where(oh2, before2 + base + c1, 0.0), axis=-1, keepdims=True)
    carry[...] = base + c1 + c2
    cnt_ref[...] = carry[...]
    meta = jnp.zeros(logits.shape, F32)
    for slot, val in ((META_I1, i1.astype(F32)), (META_I2, i2.astype(F32)), (META_G1, g1), (META_G2, g2),
                      (META_R1, rank1), (META_R2, rank2)):
        meta = jnp.where(lane == slot, val, meta)
    meta_ref[...] = meta


def _dispatch_kernel(nt, pos_ref, prev_ref, x_ref, nw_ref, sh_ref, sc_ref, zeros_ref, xs_ref, hbuf, sem):
    del zeros_ref
    i = pl.program_id(0)
    tm = hbuf.shape[1]
    slot = i % 2

    def row_copy(sl, r, p):
        return pltpu.make_async_copy(hbuf.at[sl, pl.ds(r, 1)], xs_ref.at[pl.ds(p, 1)], sem.at[sl])

    def drain(sl, p_ref):
        def body(r, carry):
            row_copy(sl, r, p_ref[0, r]).wait()
            row_copy(sl, r, p_ref[0, tm + r]).wait()
            return carry
        lax.fori_loop(0, tm, body, 0)

    hbuf[slot] = _norm_mod(x_ref[...], nw_ref[...], sh_ref[...], sc_ref[...])

    def issue(r, carry):
        row_copy(slot, r, pos_ref[0, r]).start(priority=0)
        row_copy(slot, r, pos_ref[0, tm + r]).start(priority=1)
        return carry

    lax.fori_loop(0, tm, issue, 0, unroll=4)

    @pl.when(i > 0)
    def _():
        drain(1 - slot, prev_ref)

    @pl.when(i == nt - 1)
    def _():
        drain(slot, pos_ref)


def _expert_kernel(n_f, te_ref, na_ref, xs_ref, wg_ref, wu_ref, wd_ref, ys_ref, h_scr):
    i = pl.program_id(0)
    f = pl.program_id(1)

    @pl.when(i < na_ref[0])
    def _():
        @pl.when(f == 0)
        def _():
            h_scr[...] = xs_ref[...].astype(BF16)

        h = h_scr[...]
        a = _silu(_dot(h, wg_ref[...])) * _dot(h, wu_ref[...])
        part = _dot(a.astype(BF16), wd_ref[...])

        @pl.when(f == 0)
        def _():
            ys_ref[...] = part

        @pl.when(f > 0)
        def _():
            ys_ref[...] += part

    @pl.when((i >= na_ref[0]) & (f == 0))
    def _():
        ys_ref[...] = jnp.zeros(ys_ref.shape, F32)


def _combine_kernel(nt, pos_ref, next_ref, ys_ref, meta_ref, x_ref, gate_ref, nw_ref, o_ref, buf, sem):
    i = pl.program_id(0)
    tm = x_ref.shape[0]
    slot = i % 2

    def row_copy(sl, k, r, p):
        return pltpu.make_async_copy(ys_ref.at[pl.ds(p, 1)], buf.at[sl, k, pl.ds(r, 1)], sem.at[sl])

    def issue(sl, p_ref):
        def body(r, carry):
            row_copy(sl, 0, r, p_ref[0, r]).start(priority=0)
            row_copy(sl, 1, r, p_ref[0, tm + r]).start(priority=1)
            return carry
        lax.fori_loop(0, tm, body, 0, unroll=4)

    @pl.when(i == 0)
    def _():
        issue(slot, pos_ref)

    @pl.when(i + 1 < nt)
    def _():
        issue(1 - slot, next_ref)

    def drain(r, carry):
        row_copy(slot, 0, r, pos_ref[0, r]).wait()
        row_copy(slot, 1, r, pos_ref[0, tm + r]).wait()
        return carry

    lax.fori_loop(0, tm, drain, 0)
    meta = meta_ref[...]
    lane = lax.broadcasted_iota(jnp.int32, meta.shape, 1)
    g1 = jnp.sum(jnp.where(lane == META_G1, meta, 0.0), axis=-1, keepdims=True)
    g2 = jnp.sum(jnp.where(lane == META_G2, meta, 0.0), axis=-1, keepdims=True)
    y = g1 * buf[slot, 0] + g2 * buf[slot, 1]
    o_ref[...] = x_ref[...] + gate_ref[...] * _rms(y, nw_ref[...])


def _moe_sparse(x, nw1, shift, scale, rw_pad, w_gu, w_down, gate, nw2):
    b, s, d = x.shape
    n = b * s
    tm = _pick(s, (512, 256, 128))
    tpb = s // tm
    nt = n // tm
    n_e, _, f2 = w_gu.shape
    ff = f2 // 2
    tf = _pick(ff, (512, 256))
    n_f = ff // tf
    te_rows = EXPERT_TILE
    n_rows = 2 * n + n_e * te_rows
    nte = n_rows // te_rows

    xrow = pl.BlockSpec((None, tm, d), lambda i: (i // tpb, i % tpb, 0))
    vec = pl.BlockSpec((None, 1, d), lambda i: (i // tpb, 0, 0))
    const = lambda shp: pl.BlockSpec(shp, lambda i: (0, 0))
    mrow = pl.BlockSpec((None, tm, LANES), lambda i: (i // tpb, i % tpb, 0))
    meta, cnt = pl.pallas_call(
        _route_kernel,
        out_shape=[jax.ShapeDtypeStruct((b, s, LANES), F32), jax.ShapeDtypeStruct((1, LANES), F32)],
        grid=(nt,),
        in_specs=[xrow, const((1, d)), vec, vec, const((d, LANES))],
        out_specs=[mrow, const((1, LANES))],
        scratch_shapes=[pltpu.VMEM((1, LANES), F32)],
        compiler_params=_cparams("arbitrary"),
        name="route_top2_rank",
    )(x, nw1, shift, scale, rw_pad)

    m2 = meta.reshape(n, LANES)
    i1 = m2[:, META_I1].astype(jnp.int32)
    i2 = m2[:, META_I2].astype(jnp.int32)
    counts = cnt[0, :n_e].astype(jnp.int32)
    padded = (counts + te_rows - 1) // te_rows * te_rows
    ends = jnp.cumsum(padded)
    starts = ends - padded
    pos1 = jnp.take(starts, i1) + m2[:, META_R1].astype(jnp.int32)
    pos2 = jnp.take(starts, i2) + m2[:, META_R2].astype(jnp.int32)
    pos = jnp.concatenate([pos1.reshape(nt, 1, tm), pos2.reshape(nt, 1, tm)], axis=2)
    n_active = (ends[-1] // te_rows).astype(jnp.int32).reshape(1)
    tile_idx = jnp.minimum(jnp.arange(nte, dtype=jnp.int32), n_active[0] - 1)
    tile_expert = jnp.minimum(jnp.sum((tile_idx * te_rows)[:, None] >= ends[None, :], axis=1), n_e - 1).astype(jnp.int32)

    def pos_block(index):
        return pl.BlockSpec((None, 1, 2 * tm), lambda i: (index(i), 0, 0), memory_space=pltpu.SMEM)

    pos_spec = pos_block(lambda i: i)
    any_spec = pl.BlockSpec(memory_space=pl.ANY)
    xs = pl.pallas_call(
        functools.partial(_dispatch_kernel, nt),
        out_shape=jax.ShapeDtypeStruct((n_rows, d), F32),
        grid=(nt,),
        in_specs=[pos_spec, pos_block(lambda i: jnp.maximum(i - 1, 0)), xrow, const((1, d)), vec, vec, any_spec],
        out_specs=any_spec,
        scratch_shapes=[pltpu.VMEM((2, tm, d), F32), pltpu.SemaphoreType.DMA((2,))],
        input_output_aliases={6: 0},
        compiler_params=_cparams("arbitrary"),
        name="moe_dispatch",
    )(pos, pos, x, nw1, shift, scale, jnp.zeros((n_rows, d), F32))

    def live(i, na):
        return jnp.minimum(i, na[0] - 1)

    def fidx(i, f, na):
        return jnp.where(i < na[0], f, n_f - 1)

    ys = pl.pallas_call(
        functools.partial(_expert_kernel, n_f),
        out_shape=jax.ShapeDtypeStruct((n_rows, d), F32),
        grid_spec=pltpu.PrefetchScalarGridSpec(
            num_scalar_prefetch=2,
            grid=(nte, n_f),
            in_specs=[pl.BlockSpec((te_rows, d), lambda i, f, te, na: (live(i, na), 0)),
                      pl.BlockSpec((None, d, tf), lambda i, f, te, na: (te[i], 0, fidx(i, f, na))),
                      pl.BlockSpec((None, d, tf), lambda i, f, te, na: (te[i], 0, fidx(i, f, na) + n_f)),
                      pl.BlockSpec((None, tf, d), lambda i, f, te, na: (te[i], fidx(i, f, na), 0))],
            out_specs=pl.BlockSpec((te_rows, d), lambda i, f, te, na: (i, 0)),
            scratch_shapes=[pltpu.VMEM((te_rows, d), BF16)]),
        compiler_params=_cparams("arbitrary", "arbitrary"),
        name="moe_experts",
    )(tile_expert, n_active, xs, w_gu, w_gu, w_down)

    return pl.pallas_call(
        functools.partial(_combine_kernel, nt),
        out_shape=jax.ShapeDtypeStruct((b, s, d), F32),
        grid=(nt,),
        in_specs=[pos_spec, pos_block(lambda i: jnp.minimum(i + 1, nt - 1)), any_spec, mrow, xrow, vec,
                  const((1, d))],
        out_specs=xrow,
        scratch_shapes=[pltpu.VMEM((2, 2, tm, d), F32), pltpu.SemaphoreType.DMA((2,))],
        compiler_params=_cparams("arbitrary"),
        name="moe_combine",
    )(pos, pos, ys, meta, x, gate, nw2)


def _conv_kernel(nt, prev_ref, cur_ref, next_ref, w_ref, b_ref, xs_ref, bm_ref, cm_ref, buf):
    i = pl.program_id(1)
    tm = cur_ref.shape[0]
    halo = SUBLANES
    buf[halo:halo + tm, :] = cur_ref[...]
    buf[0:halo, :] = jnp.where(i > 0, prev_ref[...], 0.0)
    buf[halo + tm:2 * halo + tm, :] = jnp.where(i < nt - 1, next_ref[...], 0.0)
    ext = buf[...]
    n_ext = ext.shape[0]
    down = lambda v: pltpu.roll(v, 1, 0)
    up = lambda v: pltpu.roll(v, n_ext - 1, 0)
    w = [w_ref[j:j + 1, :] for j in range(D_CONV)]
    acc = (w[2] * ext + down(w[1] * ext + down(w[0] * ext)) + up(w[3] * ext + up(w[4] * ext)))
    y = _silu(acc[halo:halo + tm, :] + b_ref[...])
    d_inner = xs_ref.shape[1]
    gn = bm_ref.shape[1]
    xs_ref[...] = y[:, :d_inner]
    bm_ref[...] = y[:, d_inner:d_inner + gn]
    cm_ref[...] = y[:, d_inner + gn:]


def _conv_silu(xbc, conv_w, conv_b, d_inner):
    b, s, c = xbc.shape
    gn = (c - d_inner) // 2
    tm = _pick(s, (512, 256, 128))
    nt = s // tm
    hb = tm // SUBLANES
    last = s // SUBLANES - 1
    row = lambda w: pl.BlockSpec((None, tm, w), lambda bi, i: (bi, i, 0))
    return pl.pallas_call(
        functools.partial(_conv_kernel, nt),
        out_shape=[jax.ShapeDtypeStruct((b, s, d_inner), F32),
                   jax.ShapeDtypeStruct((b, s, gn), F32), jax.ShapeDtypeStruct((b, s, gn), F32)],
        grid=(b, nt),
        in_specs=[pl.BlockSpec((None, SUBLANES, c), lambda bi, i: (bi, jnp.maximum(i * hb - 1, 0), 0)),
                  row(c),
                  pl.BlockSpec((None, SUBLANES, c), lambda bi, i: (bi, jnp.minimum((i + 1) * hb, last), 0)),
                  pl.BlockSpec((D_CONV, c), lambda bi, i: (0, 0)),
                  pl.BlockSpec((1, c), lambda bi, i: (0, 0))],
        out_specs=[row(d_inner), row(gn), row(gn)],
        scratch_shapes=[pltpu.VMEM((tm + 2 * SUBLANES, c), F32)],
        compiler_params=_cparams("parallel", "parallel"),
        name="dwconv_silu",
    )(xbc, xbc, xbc, conv_w, conv_b)


def _softplus(x):
    return jnp.maximum(x, 0.0) + jnp.log(1.0 + jnp.exp(-jnp.abs(x)))


def _ssd_kernel(direction, nc, xs_ref, bm_ref, cm_ref, dt_ref, dtb_ref, a_ref, h0_ref, e_ref,
                y_ref, hT_ref, st_scr):
    c = pl.program_id(1)

    @pl.when(c == 0)
    def _():
        st_scr[...] = h0_ref[...]

    q = CHUNK
    fwd = direction == 0
    dt = _softplus(dt_ref[...] + dtb_ref[...])
    d_a = dt * a_ref[...]
    row = lax.broadcasted_iota(jnp.int32, (q, q), 0)
    col = lax.broadcasted_iota(jnp.int32, (q, q), 1)
    causal = (row >= col) if fwd else (row <= col)
    tri = jnp.where(causal, 1.0, 0.0).astype(BF16)
    a_cum = _dot_split_rhs(tri, d_a, 3)
    a_cum_t = a_cum.T
    expand = e_ref[...]
    dt_x = _dot_split_lhs(dt, expand, 2)
    a_x = _dot_split_lhs(a_cum, expand, 2)
    last = q - 1 if fwd else 0
    a_last = a_x[last:last + 1, :]
    xdt = xs_ref[...] * dt_x
    xdt_b = xdt.astype(BF16)
    w_state = (jnp.exp(a_last - a_x) * xdt).astype(BF16)
    exp_a = jnp.exp(a_x)
    exp_last = jnp.exp(a_last)
    low = lax.broadcasted_iota(jnp.int32, (q, LANES), 1) < SSM_HEAD_DIM
    gw = HEADS_PER_GROUP * SSM_HEAD_DIM
    for g in range(N_GROUPS):
        gs = slice(g * gw, (g + 1) * gw)
        b_g = bm_ref[:, g * D_STATE:(g + 1) * D_STATE]
        c_b = cm_ref[:, g * D_STATE:(g + 1) * D_STATE].astype(BF16)
        cb = lax.dot_general(c_b, b_g.astype(BF16), (((1,), (1,)), ((), ())), preferred_element_type=F32)
        st_g = st_scr[:, gs]
        y_off = _dot(c_b, st_g.astype(BF16)) * exp_a[:, gs]
        s_new = _dot(b_g.T.astype(BF16), w_state[:, gs])
        for pr in range(HEADS_PER_GROUP // 2):
            ms = []
            for j in range(2):
                hl = direction * N_SSM_HEADS + g * HEADS_PER_GROUP + 2 * pr + j
                seg = a_cum[:, hl:hl + 1] - a_cum_t[hl:hl + 1, :]
                dec = jnp.exp(jnp.where(causal, seg, -jnp.inf))
                ms.append((cb * dec).astype(BF16))
            xp = xdt_b[:, g * gw + pr * LANES:g * gw + (pr + 1) * LANES]
            zero = jnp.zeros_like(xp)
            rhs = jnp.concatenate([jnp.where(low, xp, zero), jnp.where(low, zero, xp)], axis=0)
            y_ref[:, g * gw + pr * LANES:g * gw + (pr + 1) * LANES] = (
                _dot(jnp.concatenate(ms, axis=1), rhs) + y_off[:, pr * LANES:(pr + 1) * LANES]).astype(y_ref.dtype)
        st_scr[:, gs] = exp_last[:, gs] * st_g + s_new

    @pl.when(c == nc - 1)
    def _():
        hT_ref[...] = st_scr[...]


def _ssd_scan(direction, xs, bm, cm, dt_raw, dt_bias, a_neg, h0):
    b, s, d_inner = xs.shape
    gn = bm.shape[2]
    nc = s // CHUNK
    heads = np.arange(d_inner) // SSM_HEAD_DIM + direction * N_SSM_HEADS
    expand = jnp.asarray(np.tile(np.arange(LANES)[:, None] == heads[None, :], (2, 1)), BF16)
    if direction == 0:
        cidx = lambda bi, c: (bi, c, 0)
    else:
        cidx = lambda bi, c: (bi, nc - 1 - c, 0)
    const = lambda bi, c: (0, 0)
    st_spec = pl.BlockSpec((None, D_STATE, d_inner), lambda bi, c: (bi, 0, 0))
    return pl.pallas_call(
        functools.partial(_ssd_kernel, direction, nc),
        out_shape=[jax.ShapeDtypeStruct((b, s, d_inner), BF16),
                   jax.ShapeDtypeStruct((b, D_STATE, d_inner), F32)],
        grid=(b, nc),
        in_specs=[pl.BlockSpec((None, CHUNK, d_inner), cidx),
                  pl.BlockSpec((None, CHUNK, gn), cidx), pl.BlockSpec((None, CHUNK, gn), cidx),
                  pl.BlockSpec((None, CHUNK, LANES), cidx),
                  pl.BlockSpec((1, LANES), const), pl.BlockSpec((1, LANES), const),
                  st_spec, pl.BlockSpec((2 * LANES, d_inner), const)],
        out_specs=[pl.BlockSpec((None, CHUNK, d_inner), cidx), st_spec],
        scratch_shapes=[pltpu.VMEM((D_STATE, d_inner), F32)],
        compiler_params=_cparams("parallel", "arbitrary"),
        name="ssd_scan",
    )(xs, bm, cm, dt_raw, dt_bias, a_neg, h0, expand)


def _ssdout_kernel(yf_ref, yb_ref, xs_ref, z_ref, dsk_ref, gnw_ref, w_ref, x_ref, gate_ref, nw_ref, o_ref):
    y = yf_ref[...].astype(F32) + yb_ref[...].astype(F32) + dsk_ref[...] * xs_ref[...]
    gy = y * _silu(z_ref[...].astype(F32))
    gnw = gnw_ref[...]
    gw = gy.shape[1] // N_GROUPS
    parts = []
    for g in range(N_GROUPS):
        parts.append(_rms(gy[:, g * gw:(g + 1) * gw], gnw[:, g * gw:(g + 1) * gw]).astype(BF16))
    out = _dot(jnp.concatenate(parts, axis=1), w_ref[...])
    o_ref[...] = x_ref[...] + gate_ref[...] * _rms(out, nw_ref[...])


def _ssd_out(yf, yb, xs, z, d_skip, gnw, w_out, x, gate, nw):
    b, s, d = x.shape
    d_inner = xs.shape[2]
    tm = _pick(s, (256, 128))
    row = lambda w: pl.BlockSpec((None, tm, w), lambda bi, i: (bi, i, 0))
    const = lambda shp: pl.BlockSpec(shp, lambda bi, i: (0, 0))
    return pl.pallas_call(
        _ssdout_kernel,
        out_shape=jax.ShapeDtypeStruct((b, s, d), F32),
        grid=(b, s // tm),
        in_specs=[row(d_inner), row(d_inner), row(d_inner), row(d_inner),
                  const((1, d_inner)), const((1, d_inner)), const(w_out.shape), row(d),
                  pl.BlockSpec((None, 1, d), lambda bi, i: (bi, 0, 0)), const((1, d))],
        out_specs=row(d),
        compiler_params=_cparams("parallel", "parallel"),
        name="ssd_gate_out_residual",
    )(yf, yb, xs, z, d_skip, gnw, w_out, x, gate, nw)


def _rope_tables(seq):
    rows = seq // GRID_W
    row = jnp.repeat(jnp.arange(rows, dtype=F32), GRID_W)
    col = jnp.tile(jnp.arange(GRID_W, dtype=F32), rows)
    freqs = ROPE_THETA ** (-jnp.arange(0, AXIS_ROT_DIM, 2, dtype=F32) / AXIS_ROT_DIM)
    ang = jnp.concatenate([row[:, None] * freqs, col[:, None] * freqs], axis=-1)
    cos = jnp.repeat(jnp.cos(ang), 2, axis=-1)
    sin = jnp.repeat(jnp.sin(ang), 2, axis=-1) * jnp.tile(jnp.asarray([-1.0, 1.0], F32), HEAD_DIM // 2)
    return jnp.tile(cos, (1, LANES // HEAD_DIM)), jnp.tile(sin, (1, LANES // HEAD_DIM))


def _mod_vectors(mods_i, batch):
    d = mods_i.shape[1] // 6
    mx = [mods_i[:batch, k * d:(k + 1) * d].reshape(batch, 1, d) for k in range(6)]
    mc = [jnp.broadcast_to(mods_i[batch, k * d:(k + 1) * d].reshape(1, 1, d), (batch, 1, d)) for k in range(6)]
    return mx, mc


def _even_mixer(x, ctx, nw0, nw1, mx, mc, w_in, qnw, knw, w_out, rope, need_ctx):
    w_u = w_in[:, :F_WIDTH].astype(BF16)
    w_qkv = w_in[:, F_WIDTH:].astype(BF16)
    qw = jnp.tile(qnw.reshape(1, HEAD_DIM), (1, LANES // HEAD_DIM))
    kw = jnp.tile(knw.reshape(1, HEAD_DIM), (1, LANES // HEAD_DIM))
    w_f = w_out[:F_WIDTH].astype(BF16)
    d = w_out.shape[1]
    w_a = (w_out[F_WIDTH:].reshape(N_KV_HEADS, GQ, HEAD_DIM, d).transpose(1, 0, 2, 3)
           .reshape(Q_WIDTH, d).astype(BF16))
    lc = ctx.shape[1]
    ones = jnp.ones((lc, LANES), F32)
    zeros = jnp.zeros((lc, LANES), F32)

    u_c, qkv_c = _norm_mod_linear(ctx, nw0, mc[0], mc[1], [w_u, w_qkv], [F32, F32])
    q_c, k_c, v_c = _qk_prepare(qkv_c, ones, zeros, qw, kw)
    u_x, qkv_x = _norm_mod_linear(x, nw0, mx[0], mx[1], [w_u, w_qkv], [F32, F32])
    q_x, k_x, v_x = _qk_prepare(qkv_x, rope[0], rope[1], qw, kw)
    k_all = jnp.concatenate([k_c, k_x], axis=1)
    v_all = jnp.concatenate([v_c, v_x], axis=1)
    a_x = _attention(q_x, k_all, v_all)
    f_x = _fourier_mix_long(u_x)
    x_new = _mixer_out(f_x, a_x, w_f, w_a, x, mx[2], nw1)
    if not need_ctx:
        return x_new, ctx
    a_c = _attention(q_c, k_c, v_c)
    f_c = _fourier_mix_short(u_c)
    return x_new, _mixer_out(f_c, a_c, w_f, w_a, ctx, mc[2], nw1)


def _ssd_branch(h_in, nw0, shift, scale, w_z, w_xbc, w_dt, conv_w, conv_b, dt_bias, a_neg, h0_f, h0_b):
    z, xbc, dt_raw = _norm_mod_linear(h_in, nw0, shift, scale, [w_z, w_xbc, w_dt], [BF16, F32, F32], tm_max=256)
    xs, bm, cm = _conv_silu(xbc, conv_w, conv_b, w_z.shape[1])
    y_f, s_f = _ssd_scan(0, xs, bm, cm, dt_raw, dt_bias, a_neg, h0_f)
    y_b, s_b = _ssd_scan(1, xs, bm, cm, dt_raw, dt_bias, a_neg, h0_b)
    return z, xs, y_f, y_b, s_f, s_b


def _odd_mixer(x, ctx, nw0, nw1, mx, mc, w_in, conv_w, conv_b, dt_bias, a_log, d_skip, gnw, w_out, need_ctx):
    d_inner = w_out.shape[0]
    conv_dim = conv_w.shape[1]
    w_z = w_in[:, :d_inner].astype(BF16)
    w_xbc = w_in[:, d_inner:d_inner + conv_dim].astype(BF16)
    n_dt = w_in.shape[1] - d_inner - conv_dim
    w_dt = jnp.pad(w_in[:, d_inner + conv_dim:], ((0, 0), (0, LANES - n_dt))).astype(BF16)
    dtb = jnp.pad(dt_bias.reshape(1, n_dt), ((0, 0), (0, LANES - n_dt)))
    a_neg = jnp.pad(-jnp.exp(a_log.astype(F32)).reshape(1, n_dt), ((0, 0), (0, LANES - n_dt)))
    cb = conv_b.reshape(1, conv_dim)
    dsk = jnp.repeat(d_skip, SSM_HEAD_DIM).reshape(1, d_inner)
    gw = gnw.reshape(1, d_inner)
    w_o = w_out.astype(BF16)
    b = x.shape[0]
    zeros = jnp.zeros((b, D_STATE, d_inner), F32)

    z_c, xs_c, yf_c, yb_c, s_f, s_b = _ssd_branch(ctx, nw0, mc[0], mc[1], w_z, w_xbc, w_dt, conv_w, cb,
                                                  dtb, a_neg, zeros, zeros)
    z_x, xs_x, yf_x, yb_x, _, _ = _ssd_branch(x, nw0, mx[0], mx[1], w_z, w_xbc, w_dt, conv_w, cb,
                                              dtb, a_neg, s_f, s_b)
    x_new = _ssd_out(yf_x, yb_x, xs_x, z_x, dsk, gw, w_o, x, mx[2], nw1)
    if not need_ctx:
        return x_new, ctx
    return x_new, _ssd_out(yf_c, yb_c, xs_c, z_c, dsk, gw, w_o, ctx, mc[2], nw1)


def kernel(x, c, ctx, c_ctx, ada_w, ada_b, norm_w, mix_in_w, q_norm_w, k_norm_w, mix_out_w, ffn_gu_w, ffn_down_w, ssm_in_w, ssm_conv_w, ssm_conv_b, ssm_dt_bias, ssm_a_log, ssm_d, ssm_norm_w, ssm_out_w, router_w, moe_gu_w, moe_down_w):
    batch, seq, d = x.shape
    depth = ada_w.shape[0]
    rope = _rope_tables(seq)
    cvec = jnp.zeros((SUBLANES, d), F32).at[:batch].set(c).at[batch].set(c_ctx)
    mods = _modulation(cvec, ada_w, ada_b)
    for i in range(depth):
        last = i == depth - 1
        j = i // 2
        mx, mc = _mod_vectors(mods[i], batch)
        nw = [norm_w[i, k].reshape(1, d) for k in range(4)]
        if i % 2 == 0:
            x, ctx_mid = _even_mixer(x, ctx, nw[0], nw[1], mx, mc, mix_in_w[j], q_norm_w[j], k_norm_w[j],
                                     mix_out_w[j], rope, not last)
            w_gu = ffn_gu_w[j].astype(BF16)[None]
            w_dn = ffn_down_w[j].astype(BF16)[None]
            x = _ffn(x, nw[2], mx[3], mx[4], w_gu, w_dn, mx[5], nw[3])
            if not last:
                ctx = _ffn(ctx_mid, nw[2], mc[3], mc[4], w_gu, w_dn, mc[5], nw[3])
        else:
            x, ctx_mid = _odd_mixer(x, ctx, nw[0], nw[1], mx, mc, ssm_in_w[j], ssm_conv_w[j], ssm_conv_b[j],
                                    ssm_dt_bias[j], ssm_a_log[j], ssm_d[j], ssm_norm_w[j], ssm_out_w[j],
                                    not last)
            rw = jnp.pad(router_w[j], ((0, 0), (0, LANES - N_EXPERTS)))
            w_gu = moe_gu_w[j].astype(BF16)
            w_dn = moe_down_w[j].astype(BF16)
            x = _moe_sparse(x, nw[2], mx[3], mx[4], rw, w_gu, w_dn, mx[5], nw[3])
            if not last:
                comb_c = _router(ctx_mid, nw[2], mc[3], mc[4], rw)
                ctx = _ffn(ctx_mid, nw[2], mc[3], mc[4], w_gu, w_dn, mc[5], nw[3], comb_c)
    return x
```

```python
import functools

import numpy as np
import jax
import jax.numpy as jnp
from jax import lax
from jax.experimental import pallas as pl
from jax.experimental.pallas import tpu as pltpu

F32 = jnp.float32
BF16 = jnp.bfloat16

EPS = 1e-6
GRID_W = 64
HEAD_DIM = 64
N_HEADS = 8
N_KV_HEADS = 2
GQ = N_HEADS // N_KV_HEADS
F_GROUPS = 4
F_GROUP_DIM = 128
F_WIDTH = F_GROUPS * F_GROUP_DIM
Q_WIDTH = N_HEADS * HEAD_DIM
KV_WIDTH = N_KV_HEADS * HEAD_DIM
ROPE_THETA = 10000.0
AXIS_ROT_DIM = HEAD_DIM // 2
ATTN_SCALE = HEAD_DIM ** -0.5
LOG2_E = 1.4426950408889634
SSM_HEAD_DIM = 64
N_SSM_HEADS = 32
N_GROUPS = 4
HEADS_PER_GROUP = N_SSM_HEADS // N_GROUPS
D_STATE = 128
D_CONV = 5
CONV_PAD = D_CONV // 2
CONV_ROWS = 128
CHUNK = 128
N_EXPERTS = 8
EXPERT_TILE = 1024
LANES = 128
SUBLANES = 8
FFT_N2 = 128
VMEM_LIMIT = 56 * 1024 * 1024


def _cparams(*sem):
    return pltpu.CompilerParams(dimension_semantics=sem, vmem_limit_bytes=VMEM_LIMIT)


def _pick(n, candidates):
    for c in candidates:
        if n % c == 0:
            return c
    return n


def _sigmoid(x):
    return 1.0 / (1.0 + jnp.exp(-x))


def _silu(x):
    return x * _sigmoid(x)


def _rms(x, w):
    return x * lax.rsqrt(jnp.mean(x * x, axis=-1, keepdims=True) + EPS) * w


def _norm_mod(x, nw, shift, scale):
    return _rms(x, nw) * (1.0 + scale) + shift


def _dot(a, b):
    return jnp.dot(a, b, preferred_element_type=F32)


def _split_bf16(x, parts):
    out = []
    r = x
    for _ in range(parts):
        p = r.astype(BF16)
        out.append(p)
        r = r - p.astype(F32)
    return out


def _dot_split_lhs(x, m_stacked, parts):
    return _dot(jnp.concatenate(_split_bf16(x, parts), axis=1), m_stacked)


def _dot_split_rhs(m, x, parts):
    acc = None
    for p in _split_bf16(x, parts):
        t = _dot(m, p)
        acc = t if acc is None else acc + t
    return acc


def _mod_kernel(c_ref, w_ref, b_ref, o_ref):
    s = _silu(c_ref[...])
    o_ref[...] = jnp.dot(s, w_ref[...], preferred_element_type=F32,
                         precision=lax.Precision.HIGHEST) + b_ref[...]


def _modulation(cvec, ada_w, ada_b):
    depth, d, n = ada_w.shape
    tn = _pick(n, (1536, 1024, 512))
    rows = cvec.shape[0]
    return pl.pallas_call(
        _mod_kernel,
        out_shape=jax.ShapeDtypeStruct((depth, rows, n), F32),
        grid=(depth, n // tn),
        in_specs=[pl.BlockSpec((rows, d), lambda i, j: (0, 0)),
                  pl.BlockSpec((None, d, tn), lambda i, j: (i, 0, j)),
                  pl.BlockSpec((None, 1, tn), lambda i, j: (i, 0, j))],
        out_specs=pl.BlockSpec((None, rows, tn), lambda i, j: (i, 0, j)),
        compiler_params=_cparams("parallel", "parallel"),
        name="adaln_modulation",
    )(cvec, ada_w, ada_b.reshape(depth, 1, n))


def _nml_kernel(n_w, x_ref, nw_ref, sh_ref, sc_ref, *refs):
    h = _norm_mod(x_ref[...], nw_ref[...], sh_ref[...], sc_ref[...]).astype(BF16)
    for w_ref, o_ref in zip(refs[:n_w], refs[n_w:]):
        o_ref[...] = _dot(h, w_ref[...]).astype(o_ref.dtype)


def _norm_mod_linear(x, nw, shift, scale, weights, out_dtypes, tm_max=512):
    b, s, d = x.shape
    tm = _pick(s, (tm_max, 256, 128))
    n_w = len(weights)
    vec = pl.BlockSpec((None, 1, d), lambda bi, i: (bi, 0, 0))
    in_specs = [pl.BlockSpec((None, tm, d), lambda bi, i: (bi, i, 0)),
                pl.BlockSpec((1, d), lambda bi, i: (0, 0)), vec, vec]
    in_specs += [pl.BlockSpec(w.shape, lambda bi, i: (0, 0)) for w in weights]
    out_shape = [jax.ShapeDtypeStruct((b, s, w.shape[1]), dt) for w, dt in zip(weights, out_dtypes)]
    out_specs = [pl.BlockSpec((None, tm, w.shape[1]), lambda bi, i: (bi, i, 0)) for w in weights]
    return pl.pallas_call(
        functools.partial(_nml_kernel, n_w),
        out_shape=out_shape,
        grid=(b, s // tm),
        in_specs=in_specs,
        out_specs=out_specs,
        compiler_params=_cparams("parallel", "parallel"),
        name="norm_mod_linear",
    )(x, nw, shift, scale, *weights)


def _qkprep_kernel(qkv_ref, cos_ref, sin_ref, qw_ref, kw_ref, bd_ref, q_ref, k_ref, v_ref):
    tm = qkv_ref.shape[0]
    lane = lax.broadcasted_iota(jnp.int32, (tm, LANES), 1)
    even = (lane % 2) == 0
    low = lane < HEAD_DIM
    cos = cos_ref[...]
    sin = sin_ref[...]
    bd = bd_ref[...]

    def prep(xb, w):
        ms = _dot_split_lhs(xb * xb, bd, 2)
        y = xb * lax.rsqrt(ms + EPS) * w
        swapped = jnp.where(even, pltpu.roll(y, LANES - 1, 1), pltpu.roll(y, 1, 1))
        return y * cos + swapped * sin

    qw = qw_ref[...]
    for cblk in range(Q_WIDTH // LANES):
        r = prep(qkv_ref[:, cblk * LANES:(cblk + 1) * LANES], qw) * (ATTN_SCALE * LOG2_E)
        r_sw = pltpu.roll(r, HEAD_DIM, 1)
        kvh = (2 * cblk) // GQ
        keep = low if kvh == 0 else jnp.logical_not(low)
        for j in range(2):
            src = r if j == kvh else r_sw
            q_ref[2 * cblk + j] = jnp.where(keep, src, 0.0).astype(q_ref.dtype)
    k_ref[...] = prep(qkv_ref[:, Q_WIDTH:Q_WIDTH + KV_WIDTH], kw_ref[...]).astype(k_ref.dtype)
    v_ref[...] = qkv_ref[:, Q_WIDTH + KV_WIDTH:Q_WIDTH + 2 * KV_WIDTH].astype(v_ref.dtype)


def _qk_prepare(qkv, cos_t, sin_t, qw, kw):
    b, s, _ = qkv.shape
    tm = _pick(s, (512, 256, 128))
    bd = np.kron(np.eye(2, dtype=np.float32), np.full((HEAD_DIM, HEAD_DIM), 1.0 / HEAD_DIM, np.float32))
    bd = np.tile(bd, (2, 1))
    row = lambda bi, i: (i, 0)
    const = lambda bi, i: (0, 0)
    return pl.pallas_call(
        _qkprep_kernel,
        out_shape=[jax.ShapeDtypeStruct((b, N_HEADS, s, LANES), BF16),
                   jax.ShapeDtypeStruct((b, s, KV_WIDTH), BF16),
                   jax.ShapeDtypeStruct((b, s, KV_WIDTH), BF16)],
        grid=(b, s // tm),
        in_specs=[pl.BlockSpec((None, tm, qkv.shape[2]), lambda bi, i: (bi, i, 0)),
                  pl.BlockSpec((tm, LANES), row), pl.BlockSpec((tm, LANES), row),
                  pl.BlockSpec((1, LANES), const), pl.BlockSpec((1, LANES), const),
                  pl.BlockSpec((2 * LANES, LANES), const)],
        out_specs=[pl.BlockSpec((None, N_HEADS, tm, LANES), lambda bi, i: (bi, 0, i, 0)),
                   pl.BlockSpec((None, tm, KV_WIDTH), lambda bi, i: (bi, i, 0)),
                   pl.BlockSpec((None, tm, KV_WIDTH), lambda bi, i: (bi, i, 0))],
        compiler_params=_cparams("parallel", "parallel"),
        name="qk_norm_rope",
    )(qkv, cos_t, sin_t, qw, kw, jnp.asarray(bd, BF16))


def _attn_kernel(nk, q_ref, k_ref, v_ref, o_ref, m_scr, l_scr, acc_scr):
    kv = pl.program_id(2)

    @pl.when(kv == 0)
    def _():
        m_scr[...] = jnp.full(m_scr.shape, -jnp.inf, F32)
        l_scr[...] = jnp.zeros(l_scr.shape, F32)
        acc_scr[...] = jnp.zeros(acc_scr.shape, F32)

    k = k_ref[...]
    v = v_ref[...]
    n_t = k.shape[0] // LANES

    def head(hd, carry):
        s = lax.dot_general(q_ref[hd], k, (((1,), (1,)), ((), ())), preferred_element_type=F32)
        tiles = [s[:, t * LANES:(t + 1) * LANES] for t in range(n_t)]
        smax = tiles[0]
        for t in range(1, n_t):
            smax = jnp.maximum(smax, tiles[t])
        m_prev = m_scr[hd]
        m_new = jnp.maximum(m_prev, jnp.max(smax, axis=-1, keepdims=True))
        alpha = jnp.exp2(m_prev - m_new)
        ps = []
        lsum = None
        for t in range(n_t):
            p_t = jnp.exp2(tiles[t] - m_new)
            lsum = p_t if lsum is None else lsum + p_t
            ps.append(p_t.astype(BF16))
        l_scr[hd] = alpha * l_scr[hd] + lsum
        acc_scr[hd] = alpha * acc_scr[hd] + _dot(jnp.concatenate(ps, axis=1), v)
        m_scr[hd] = m_new
        return carry

    lax.fori_loop(0, N_HEADS, head, 0, unroll=8)

    @pl.when(kv == nk - 1)
    def _():
        tq = o_ref.shape[0]
        low = lax.broadcasted_iota(jnp.int32, (tq, LANES), 1) < HEAD_DIM
        for g in range(GQ):
            o0 = acc_scr[g] / jnp.sum(l_scr[g], axis=-1, keepdims=True)
            o1 = acc_scr[GQ + g] / jnp.sum(l_scr[GQ + g], axis=-1, keepdims=True)
            o_ref[:, g * LANES:(g + 1) * LANES] = jnp.where(low, o0, o1).astype(o_ref.dtype)


def _attention(q, k, v):
    b, _, s, _ = q.shape
    skv = k.shape[1]
    tq = _pick(s, (1024, 512, 256, 128))
    tk = _pick(skv, (1280, 640, 256, 128))
    nk = skv // tk
    return pl.pallas_call(
        functools.partial(_attn_kernel, nk),
        out_shape=jax.ShapeDtypeStruct((b, s, Q_WIDTH), BF16),
        grid=(b, s // tq, nk),
        in_specs=[pl.BlockSpec((None, N_HEADS, tq, LANES), lambda bi, i, j: (bi, 0, i, 0)),
                  pl.BlockSpec((None, tk, KV_WIDTH), lambda bi, i, j: (bi, j, 0)),
                  pl.BlockSpec((None, tk, KV_WIDTH), lambda bi, i, j: (bi, j, 0))],
        out_specs=pl.BlockSpec((None, tq, Q_WIDTH), lambda bi, i, j: (bi, i, 0)),
        scratch_shapes=[pltpu.VMEM((N_HEADS, tq, LANES), F32), pltpu.VMEM((N_HEADS, tq, LANES), F32),
                        pltpu.VMEM((N_HEADS, tq, LANES), F32)],
        compiler_params=_cparams("parallel", "parallel", "arbitrary"),
        name="flash_attention",
    )(q, k, v)


def _dft_cos_sin(n):
    idx = np.arange(n, dtype=np.int64)
    ang = 2.0 * np.pi * ((idx[:, None] * idx[None, :]) % n).astype(np.float64) / n
    return np.cos(ang), np.sin(ang)


def _fft1_kernel(x_ref, w_ref, yr_ref, yi_ref):
    n1 = x_ref.shape[0]
    fw = yr_ref.shape[2]
    x = jnp.concatenate([x_ref[:, jj, :] for jj in range(x_ref.shape[1])], axis=1)
    y = _dot(w_ref[...], x.astype(BF16))
    for jj in range(yr_ref.shape[1]):
        yr_ref[:, jj, :] = y[:n1, jj * fw:(jj + 1) * fw]
        yi_ref[:, jj, :] = y[n1:, jj * fw:(jj + 1) * fw]


def _fft2_kernel(kb, scale, yr_ref, yi_ref, tc_ref, ts_ref, w2_ref, wc_ref, o_ref):
    n2 = FFT_N2
    w2 = w2_ref[...]
    wc = wc_ref[...]
    for j in range(kb):
        yr = yr_ref[j]
        yi = yi_ref[j]
        c = tc_ref[:, j:j + 1]
        s = ts_ref[:, j:j + 1]
        z = jnp.concatenate([yr * c + yi * s, yi * c - yr * s], axis=0).astype(BF16)
        gq = _dot(w2, z)
        for g in range(F_GROUPS):
            cols = slice(g * F_GROUP_DIM, (g + 1) * F_GROUP_DIM)
            gg = jnp.concatenate([gq[:n2, cols], gq[n2:, cols]], axis=1).astype(BF16)
            o_ref[:, j, g * F_GROUP_DIM:(g + 1) * F_GROUP_DIM] = _dot(gg, wc) * scale


def _fourier_mix_long(u):
    b, l, fw = u.shape
    n2 = FFT_N2
    n1 = l // n2
    cols = n2 * fw
    c1, s1 = _dft_cos_sin(n1)
    w1 = jnp.asarray(np.concatenate([c1, -s1], axis=0), BF16)
    c2, s2 = _dft_cos_sin(n2)
    w2 = jnp.asarray(np.block([[c2, s2], [-s2, c2]]), BF16)
    cc, sc = _dft_cos_sin(F_GROUP_DIM)
    wc = jnp.asarray(np.concatenate([cc, sc], axis=0), BF16)
    k1 = np.arange(n1, dtype=np.int64)
    nn = np.arange(n2, dtype=np.int64)
    ang = 2.0 * np.pi * ((nn[:, None] * k1[None, :]) % l).astype(np.float64) / l
    kb = SUBLANES
    tw_c = jnp.asarray(np.cos(ang).reshape(n2, n1 // kb, kb).transpose(1, 0, 2), F32)
    tw_s = jnp.asarray(np.sin(ang).reshape(n2, n1 // kb, kb).transpose(1, 0, 2), F32)
    tn = _pick(cols, (4096,))
    x2 = u.reshape(b, n1, n2, fw)
    yr, yi = pl.pallas_call(
        _fft1_kernel,
        out_shape=[jax.ShapeDtypeStruct((b, n1, n2, fw), F32)] * 2,
        grid=(b, cols // tn),
        in_specs=[pl.BlockSpec((None, n1, tn // fw, fw), lambda bi, j: (bi, 0, j, 0)),
                  pl.BlockSpec((2 * n1, n1), lambda bi, j: (0, 0))],
        out_specs=[pl.BlockSpec((None, n1, tn // fw, fw), lambda bi, j: (bi, 0, j, 0))] * 2,
        compiler_params=_cparams("parallel", "parallel"),
        name="fft_stage1",
    )(x2, w1)
    scale = float(1.0 / np.sqrt(float(l) * F_GROUP_DIM))
    yblk = pl.BlockSpec((None, kb, n2, fw), lambda bi, j: (bi, j, 0, 0))
    tblk = pl.BlockSpec((None, n2, kb), lambda bi, j: (j, 0, 0))
    out = pl.pallas_call(
        functools.partial(_fft2_kernel, kb, scale),
        out_shape=jax.ShapeDtypeStruct((b, n2, n1, fw), F32),
        grid=(b, n1 // kb),
        in_specs=[yblk, yblk, tblk, tblk,
                  pl.BlockSpec((2 * n2, 2 * n2), lambda bi, j: (0, 0)),
                  pl.BlockSpec((2 * F_GROUP_DIM, F_GROUP_DIM), lambda bi, j: (0, 0))],
        out_specs=pl.BlockSpec((None, n2, kb, fw), lambda bi, j: (bi, 0, j, 0)),
        compiler_params=_cparams("parallel", "parallel"),
        name="fft_stage2",
    )(yr, yi, tw_c, tw_s, w2, wc)
    return out.reshape(b, l, fw)


def _fftc_kernel(scale, u_ref, wl_ref, wc_ref, o_ref):
    fw = u_ref.shape[1]
    ab = _dot(u_ref[...].astype(BF16), wc_ref[...])
    st = jnp.concatenate([ab[:, :fw], ab[:, fw:]], axis=0).astype(BF16)
    o_ref[...] = _dot(wl_ref[...], st) * scale


def _fourier_mix_short(u):
    b, l, fw = u.shape
    cl, sl = _dft_cos_sin(l)
    wl = jnp.asarray(np.concatenate([cl, -sl], axis=1), BF16)
    cc, sc = _dft_cos_sin(F_GROUP_DIM)
    eye = np.eye(F_GROUPS)
    wc = jnp.asarray(np.concatenate([np.kron(eye, cc), np.kron(eye, sc)], axis=1), BF16)
    scale = float(1.0 / np.sqrt(float(l) * F_GROUP_DIM))
    return pl.pallas_call(
        functools.partial(_fftc_kernel, scale),
        out_shape=jax.ShapeDtypeStruct((b, l, fw), F32),
        grid=(b,),
        in_specs=[pl.BlockSpec((None, l, fw), lambda bi: (bi, 0, 0)),
                  pl.BlockSpec((l, 2 * l), lambda bi: (0, 0)),
                  pl.BlockSpec((fw, 2 * fw), lambda bi: (0, 0))],
        out_specs=pl.BlockSpec((None, l, fw), lambda bi: (bi, 0, 0)),
        compiler_params=_cparams("parallel"),
        name="fft_direct",
    )(u, wl, wc)


def _mixout_kernel(f_ref, a_ref, wf_ref, wa_ref, x_ref, gate_ref, nw_ref, o_ref):
    y = _dot(f_ref[...].astype(BF16), wf_ref[...]) + _dot(a_ref[...].astype(BF16), wa_ref[...])
    o_ref[...] = x_ref[...] + gate_ref[...] * _rms(y, nw_ref[...])


def _mixer_out(f, a, wf, wa, x, gate, nw):
    b, s, d = x.shape
    tm = _pick(s, (512, 256, 128))
    row = lambda w: pl.BlockSpec((None, tm, w), lambda bi, i: (bi, i, 0))
    const = lambda shp: pl.BlockSpec(shp, lambda bi, i: (0, 0))
    return pl.pallas_call(
        _mixout_kernel,
        out_shape=jax.ShapeDtypeStruct((b, s, d), F32),
        grid=(b, s // tm),
        in_specs=[row(f.shape[2]), row(a.shape[2]), const(wf.shape), const(wa.shape), row(d),
                  pl.BlockSpec((None, 1, d), lambda bi, i: (bi, 0, 0)), const((1, d))],
        out_specs=row(d),
        compiler_params=_cparams("parallel", "parallel"),
        name="mixer_out_residual",
    )(f, a, wf, wa, x, gate, nw)


def _ffn_kernel(n_e, n_f, use_comb, *refs):
    if use_comb:
        (x_ref, nw1_ref, sh_ref, sc_ref, wg_ref, wu_ref, wd_ref, gate_ref, nw2_ref, comb_ref,
         o_ref, h_scr, acc_scr) = refs
    else:
        (x_ref, nw1_ref, sh_ref, sc_ref, wg_ref, wu_ref, wd_ref, gate_ref, nw2_ref,
         o_ref, h_scr, acc_scr) = refs
    e = pl.program_id(2)
    f = pl.program_id(3)

    @pl.when((e == 0) & (f == 0))
    def _():
        h_scr[...] = _norm_mod(x_ref[...], nw1_ref[...], sh_ref[...], sc_ref[...]).astype(BF16)
        acc_scr[...] = jnp.zeros(acc_scr.shape, F32)

    h = h_scr[...]
    a = _silu(_dot(h, wg_ref[...])) * _dot(h, wu_ref[...])
    if use_comb:
        comb = comb_ref[...]
        lane = lax.broadcasted_iota(jnp.int32, comb.shape, 1)
        a = a * jnp.sum(jnp.where(lane == e, comb, 0.0), axis=-1, keepdims=True)
    acc_scr[...] += _dot(a.astype(BF16), wd_ref[...])

    @pl.when((e == n_e - 1) & (f == n_f - 1))
    def _():
        o_ref[...] = x_ref[...] + gate_ref[...] * _rms(acc_scr[...], nw2_ref[...])


def _ffn(x, nw1, shift, scale, w_gu, w_down, gate, nw2, comb=None):
    b, s, d = x.shape
    n_e, _, f2 = w_gu.shape
    ff = f2 // 2
    tm = _pick(s, (1024, 512, 256))
    tf = _pick(ff, (512, 256))
    n_f = ff // tf
    vec = pl.BlockSpec((None, 1, d), lambda bi, i, e, f: (bi, 0, 0))
    const = pl.BlockSpec((1, d), lambda bi, i, e, f: (0, 0))
    row = pl.BlockSpec((None, tm, d), lambda bi, i, e, f: (bi, i, 0))
    in_specs = [row, const, vec, vec,
                pl.BlockSpec((None, d, tf), lambda bi, i, e, f: (e, 0, f)),
                pl.BlockSpec((None, d, tf), lambda bi, i, e, f: (e, 0, f + n_f)),
                pl.BlockSpec((None, tf, d), lambda bi, i, e, f: (e, f, 0)),
                vec, const]
    args = [x, nw1, shift, scale, w_gu, w_gu, w_down, gate, nw2]
    if comb is not None:
        in_specs.append(pl.BlockSpec((None, tm, LANES), lambda bi, i, e, f: (bi, i, 0)))
        args.append(comb)
    return pl.pallas_call(
        functools.partial(_ffn_kernel, n_e, n_f, comb is not None),
        out_shape=jax.ShapeDtypeStruct((b, s, d), F32),
        grid=(b, s // tm, n_e, n_f),
        in_specs=in_specs,
        out_specs=row,
        scratch_shapes=[pltpu.VMEM((tm, d), BF16), pltpu.VMEM((tm, d), F32)],
        compiler_params=_cparams("parallel", "parallel", "arbitrary", "arbitrary"),
        name="swiglu_residual",
    )(*args)


def _router_kernel(x_ref, nw_ref, sh_ref, sc_ref, rw_ref, comb_ref):
    h = _norm_mod(x_ref[...], nw_ref[...], sh_ref[...], sc_ref[...])
    logits = _router_logits(h, rw_ref[...])
    lane = lax.broadcasted_iota(jnp.int32, logits.shape, 1)
    i1, i2, g1, g2 = _top2(logits, lane)
    comb_ref[...] = jnp.where(lane == i1, g1, 0.0) + jnp.where(lane == i2, g2, 0.0)


def _router(x, nw, shift, scale, rw_pad):
    b, s, d = x.shape
    tm = _pick(s, (512, 256, 128))
    vec = pl.BlockSpec((None, 1, d), lambda bi, i: (bi, 0, 0))
    return pl.pallas_call(
        _router_kernel,
        out_shape=jax.ShapeDtypeStruct((b, s, LANES), F32),
        grid=(b, s // tm),
        in_specs=[pl.BlockSpec((None, tm, d), lambda bi, i: (bi, i, 0)),
                  pl.BlockSpec((1, d), lambda bi, i: (0, 0)), vec, vec,
                  pl.BlockSpec((d, LANES), lambda bi, i: (0, 0))],
        out_specs=pl.BlockSpec((None, tm, LANES), lambda bi, i: (bi, i, 0)),
        compiler_params=_cparams("parallel", "parallel"),
        name="router_top2",
    )(x, nw, shift, scale, rw_pad)


def _top2(logits, lane):
    neg = -jnp.inf
    logits = jnp.where(lane < N_EXPERTS, logits, neg)
    m1 = jnp.max(logits, axis=-1, keepdims=True)
    i1 = jnp.min(jnp.where(logits == m1, lane, LANES), axis=-1, keepdims=True)
    rest = jnp.where(lane == i1, neg, logits)
    m2 = jnp.max(rest, axis=-1, keepdims=True)
    i2 = jnp.min(jnp.where(rest == m2, lane, LANES), axis=-1, keepdims=True)
    e2 = jnp.exp(m2 - m1)
    den = 1.0 + e2
    return i1, i2, 1.0 / den, e2 / den


def _router_logits(h, rw):
    h_hi, h_lo = _split_bf16(h, 2)
    w_hi, w_lo = _split_bf16(rw, 2)
    return _dot(h_hi, w_hi) + _dot(h_hi, w_lo) + _dot(h_lo, w_hi)


META_I1, META_I2, META_G1, META_G2, META_R1, META_R2 = range(6)


def _route_kernel(x_ref, nw_ref, sh_ref, sc_ref, rw_ref, meta_ref, cnt_ref, carry):
    @pl.when(pl.program_id(0) == 0)
    def _():
        carry[...] = jnp.zeros(carry.shape, F32)

    h = _norm_mod(x_ref[...], nw_ref[...], sh_ref[...], sc_ref[...])
    logits = _router_logits(h, rw_ref[...])
    tm = logits.shape[0]
    lane = lax.broadcasted_iota(jnp.int32, logits.shape, 1)
    i1, i2, g1, g2 = _top2(logits, lane)
    oh1 = lane == i1
    oh2 = lane == i2
    r = lax.broadcasted_iota(jnp.int32, (tm, tm), 0)
    c = lax.broadcasted_iota(jnp.int32, (tm, tm), 1)
    earlier = jnp.where(c < r, 1.0, 0.0).astype(BF16)
    f1 = jnp.where(oh1, 1.0, 0.0)
    f2 = jnp.where(oh2, 1.0, 0.0)
    before1 = _dot(earlier, f1.astype(BF16))
    before2 = _dot(earlier, f2.astype(BF16))
    c1 = jnp.sum(f1, axis=0, keepdims=True)
    c2 = jnp.sum(f2, axis=0, keepdims=True)
    base = carry[...]
    rank1 = jnp.sum(jnp.where(oh1, before1 + base, 0.0), axis=-1, keepdims=True)
    rank2 = jnp.sum(jnp.where(oh2, before2 + base + c1, 0.0), axis=-1, keepdims=True)
    carry[...] = base + c1 + c2
    cnt_ref[...] = carry[...]
    meta = jnp.zeros(logits.shape, F32)
    for slot, val in ((META_I1, i1.astype(F32)), (META_I2, i2.astype(F32)), (META_G1, g1), (META_G2, g2),
                      (META_R1, rank1), (META_R2, rank2)):
        meta = jnp.where(lane == slot, val, meta)
    meta_ref[...] = meta


def _dispatch_kernel(nt, pos_ref, prev_ref, x_ref, nw_ref, sh_ref, sc_ref, zeros_ref, xs_ref, hbuf, sem):
    del zeros_ref
    i = pl.program_id(0)
    tm = hbuf.shape[1]
    slot = i % 2

    def row_copy(sl, r, p):
        return pltpu.make_async_copy(hbuf.at[sl, pl.ds(r, 1)], xs_ref.at[pl.ds(p, 1)], sem.at[sl])

    def drain(sl, p_ref):
        def body(r, carry):
            row_copy(sl, r, p_ref[0, r]).wait()
            row_copy(sl, r, p_ref[0, tm + r]).wait()
            return carry
        lax.fori_loop(0, tm, body, 0)

    hbuf[slot] = _norm_mod(x_ref[...], nw_ref[...], sh_ref[...], sc_ref[...])

    def issue(r, carry):
        row_copy(slot, r, pos_ref[0, r]).start(priority=0)
        row_copy(slot, r, pos_ref[0, tm + r]).start(priority=1)
        return carry

    lax.fori_loop(0, tm, issue, 0, unroll=4)

    @pl.when(i > 0)
    def _():
        drain(1 - slot, prev_ref)

    @pl.when(i == nt - 1)
    def _():
        drain(slot, pos_ref)


def _expert_kernel(n_f, te_ref, na_ref, xs_ref, wg_ref, wu_ref, wd_ref, ys_ref, h_scr):
    i = pl.program_id(0)
    f = pl.program_id(1)

    @pl.when(i < na_ref[0])
    def _():
        @pl.when(f == 0)
        def _():
            h_scr[...] = xs_ref[...].astype(BF16)

        h = h_scr[...]
        a = _silu(_dot(h, wg_ref[...])) * _dot(h, wu_ref[...])
        part = _dot(a.astype(BF16), wd_ref[...])

        @pl.when(f == 0)
        def _():
            ys_ref[...] = part

        @pl.when(f > 0)
        def _():
            ys_ref[...] += part

    @pl.when((i >= na_ref[0]) & (f == 0))
    def _():
        ys_ref[...] = jnp.zeros(ys_ref.shape, F32)


def _combine_kernel(nt, pos_ref, next_ref, ys_ref, meta_ref, x_ref, gate_ref, nw_ref, o_ref, buf, sem):
    i = pl.program_id(0)
    tm = x_ref.shape[0]
    slot = i % 2

    def row_copy(sl, k, r, p):
        return pltpu.make_async_copy(ys_ref.at[pl.ds(p, 1)], buf.at[sl, k, pl.ds(r, 1)], sem.at[sl])

    def issue(sl, p_ref):
        def body(r, carry):
            row_copy(sl, 0, r, p_ref[0, r]).start(priority=0)
            row_copy(sl, 1, r, p_ref[0, tm + r]).start(priority=1)
            return carry
        lax.fori_loop(0, tm, body, 0, unroll=4)

    @pl.when(i == 0)
    def _():
        issue(slot, pos_ref)

    @pl.when(i + 1 < nt)
    def _():
        issue(1 - slot, next_ref)

    def drain(r, carry):
        row_copy(slot, 0, r, pos_ref[0, r]).wait()
        row_copy(slot, 1, r, pos_ref[0, tm + r]).wait()
        return carry

    lax.fori_loop(0, tm, drain, 0)
    meta = meta_ref[...]
    lane = lax.broadcasted_iota(jnp.int32, meta.shape, 1)
    g1 = jnp.sum(jnp.where(lane == META_G1, meta, 0.0), axis=-1, keepdims=True)
    g2 = jnp.sum(jnp.where(lane == META_G2, meta, 0.0), axis=-1, keepdims=True)
    y = g1 * buf[slot, 0] + g2 * buf[slot, 1]
    o_ref[...] = x_ref[...] + gate_ref[...] * _rms(y, nw_ref[...])


def _moe_sparse(x, nw1, shift, scale, rw_pad, w_gu, w_down, gate, nw2):
    b, s, d = x.shape
    n = b * s
    tm = _pick(s, (512, 256, 128))
    tpb = s // tm
    nt = n // tm
    n_e, _, f2 = w_gu.shape
    ff = f2 // 2
    tf = _pick(ff, (512, 256))
    n_f = ff // tf
    te_rows = EXPERT_TILE
    n_rows = 2 * n + n_e * te_rows
    nte = n_rows // te_rows

    xrow = pl.BlockSpec((None, tm, d), lambda i: (i // tpb, i % tpb, 0))
    vec = pl.BlockSpec((None, 1, d), lambda i: (i // tpb, 0, 0))
    const = lambda shp: pl.BlockSpec(shp, lambda i: (0, 0))
    mrow = pl.BlockSpec((None, tm, LANES), lambda i: (i // tpb, i % tpb, 0))
    meta, cnt = pl.pallas_call(
        _route_kernel,
        out_shape=[jax.ShapeDtypeStruct((b, s, LANES), F32), jax.ShapeDtypeStruct((1, LANES), F32)],
        grid=(nt,),
        in_specs=[xrow, const((1, d)), vec, vec, const((d, LANES))],
        out_specs=[mrow, const((1, LANES))],
        scratch_shapes=[pltpu.VMEM((1, LANES), F32)],
        compiler_params=_cparams("arbitrary"),
        name="route_top2_rank",
    )(x, nw1, shift, scale, rw_pad)

    m2 = meta.reshape(n, LANES)
    i1 = m2[:, META_I1].astype(jnp.int32)
    i2 = m2[:, META_I2].astype(jnp.int32)
    counts = cnt[0, :n_e].astype(jnp.int32)
    padded = (counts + te_rows - 1) // te_rows * te_rows
    ends = jnp.cumsum(padded)
    starts = ends - padded
    pos1 = jnp.take(starts, i1) + m2[:, META_R1].astype(jnp.int32)
    pos2 = jnp.take(starts, i2) + m2[:, META_R2].astype(jnp.int32)
    pos = jnp.concatenate([pos1.reshape(nt, 1, tm), pos2.reshape(nt, 1, tm)], axis=2)
    n_active = (ends[-1] // te_rows).astype(jnp.int32).reshape(1)
    tile_idx = jnp.minimum(jnp.arange(nte, dtype=jnp.int32), n_active[0] - 1)
    tile_expert = jnp.minimum(jnp.sum((tile_idx * te_rows)[:, None] >= ends[None, :], axis=1), n_e - 1).astype(jnp.int32)

    def pos_block(index):
        return pl.BlockSpec((None, 1, 2 * tm), lambda i: (index(i), 0, 0), memory_space=pltpu.SMEM)

    pos_spec = pos_block(lambda i: i)
    any_spec = pl.BlockSpec(memory_space=pl.ANY)
    xs = pl.pallas_call(
        functools.partial(_dispatch_kernel, nt),
        out_shape=jax.ShapeDtypeStruct((n_rows, d), F32),
        grid=(nt,),
        in_specs=[pos_spec, pos_block(lambda i: jnp.maximum(i - 1, 0)), xrow, const((1, d)), vec, vec, any_spec],
        out_specs=any_spec,
        scratch_shapes=[pltpu.VMEM((2, tm, d), F32), pltpu.SemaphoreType.DMA((2,))],
        input_output_aliases={6: 0},
        compiler_params=_cparams("arbitrary"),
        name="moe_dispatch",
    )(pos, pos, x, nw1, shift, scale, jnp.zeros((n_rows, d), F32))

    def live(i, na):
        return jnp.minimum(i, na[0] - 1)

    def fidx(i, f, na):
        return jnp.where(i < na[0], f, n_f - 1)

    ys = pl.pallas_call(
        functools.partial(_expert_kernel, n_f),
        out_shape=jax.ShapeDtypeStruct((n_rows, d), F32),
        grid_spec=pltpu.PrefetchScalarGridSpec(
            num_scalar_prefetch=2,
            grid=(nte, n_f),
            in_specs=[pl.BlockSpec((te_rows, d), lambda i, f, te, na: (live(i, na), 0)),
                      pl.BlockSpec((None, d, tf), lambda i, f, te, na: (te[i], 0, fidx(i, f, na))),
                      pl.BlockSpec((None, d, tf), lambda i, f, te, na: (te[i], 0, fidx(i, f, na) + n_f)),
                      pl.BlockSpec((None, tf, d), lambda i, f, te, na: (te[i], fidx(i, f, na), 0))],
            out_specs=pl.BlockSpec((te_rows, d), lambda i, f, te, na: (i, 0)),
            scratch_shapes=[pltpu.VMEM((te_rows, d), BF16)]),
        compiler_params=_cparams("arbitrary", "arbitrary"),
        name="moe_experts",
    )(tile_expert, n_active, xs, w_gu, w_gu, w_down)

    return pl.pallas_call(
        functools.partial(_combine_kernel, nt),
        out_shape=jax.ShapeDtypeStruct((b, s, d), F32),
        grid=(nt,),
        in_specs=[pos_spec, pos_block(lambda i: jnp.minimum(i + 1, nt - 1)), any_spec, mrow, xrow, vec,
                  const((1, d))],
        out_specs=xrow,
        scratch_shapes=[pltpu.VMEM((2, 2, tm, d), F32), pltpu.SemaphoreType.DMA((2,))],
        compiler_params=_cparams("arbitrary"),
        name="moe_combine",
    )(pos, pos, ys, meta, x, gate, nw2)


def _conv_kernel(nt, prev_ref, cur_ref, next_ref, w_ref, b_ref, sel_ref, xs_ref, bm_ref, cm_ref, buf):
    i = pl.program_id(1)
    tm = cur_ref.shape[0]
    halo = SUBLANES
    buf[halo:halo + tm, :] = cur_ref[...]
    buf[0:halo, :] = jnp.where(i > 0, prev_ref[...], 0.0)
    buf[halo + tm:2 * halo + tm, :] = jnp.where(i < nt - 1, next_ref[...], 0.0)
    w = [w_ref[j:j + 1, :] for j in range(D_CONV)]
    sel = sel_ref[...]
    bias = b_ref[...]
    d_inner = xs_ref.shape[1]
    gn = bm_ref.shape[1]
    for r in range(tm // CONV_ROWS):
        win = buf[r * CONV_ROWS:(r + 1) * CONV_ROWS + 2 * halo, :]
        stacked = jnp.concatenate([(w[j] * win).astype(BF16) for j in range(D_CONV)], axis=0)
        y = _silu(_dot(sel, stacked) + bias)
        rows = slice(r * CONV_ROWS, (r + 1) * CONV_ROWS)
        xs_ref[rows, :] = y[:, :d_inner]
        bm_ref[rows, :] = y[:, d_inner:d_inner + gn]
        cm_ref[rows, :] = y[:, d_inner + gn:]


def _conv_silu(xbc, conv_w, conv_b, d_inner):
    b, s, c = xbc.shape
    gn = (c - d_inner) // 2
    tm = _pick(s, (512, 256, 128))
    nt = s // tm
    hb = tm // SUBLANES
    last = s // SUBLANES - 1
    row = lambda w: pl.BlockSpec((None, tm, w), lambda bi, i: (bi, i, 0))
    win = CONV_ROWS + 2 * SUBLANES
    sel = np.zeros((CONV_ROWS, D_CONV * win), np.float32)
    for j in range(D_CONV):
        sel[np.arange(CONV_ROWS), j * win + np.arange(CONV_ROWS) + SUBLANES - CONV_PAD + j] = 1.0
    return pl.pallas_call(
        functools.partial(_conv_kernel, nt),
        out_shape=[jax.ShapeDtypeStruct((b, s, d_inner), F32),
                   jax.ShapeDtypeStruct((b, s, gn), F32), jax.ShapeDtypeStruct((b, s, gn), F32)],
        grid=(b, nt),
        in_specs=[pl.BlockSpec((None, SUBLANES, c), lambda bi, i: (bi, jnp.maximum(i * hb - 1, 0), 0)),
                  row(c),
                  pl.BlockSpec((None, SUBLANES, c), lambda bi, i: (bi, jnp.minimum((i + 1) * hb, last), 0)),
                  pl.BlockSpec((D_CONV, c), lambda bi, i: (0, 0)),
                  pl.BlockSpec((1, c), lambda bi, i: (0, 0)),
                  pl.BlockSpec(sel.shape, lambda bi, i: (0, 0))],
        out_specs=[row(d_inner), row(gn), row(gn)],
        scratch_shapes=[pltpu.VMEM((tm + 2 * SUBLANES, c), F32)],
        compiler_params=_cparams("parallel", "parallel"),
        name="dwconv_silu",
    )(xbc, xbc, xbc, conv_w, conv_b, jnp.asarray(sel, BF16))


def _softplus(x):
    return jnp.maximum(x, 0.0) + jnp.log(1.0 + jnp.exp(-jnp.abs(x)))


def _ssd_kernel(direction, nc, xs_ref, bm_ref, cm_ref, dt_ref, dtb_ref, a_ref, h0_ref, e_ref,
                y_ref, hT_ref, st_scr):
    c = pl.program_id(1)

    @pl.when(c == 0)
    def _():
        st_scr[...] = h0_ref[...]

    q = CHUNK
    fwd = direction == 0
    dt = _softplus(dt_ref[...] + dtb_ref[...])
    d_a = dt * a_ref[...]
    row = lax.broadcasted_iota(jnp.int32, (q, q), 0)
    col = lax.broadcasted_iota(jnp.int32, (q, q), 1)
    causal = (row >= col) if fwd else (row <= col)
    tri = jnp.where(causal, 1.0, 0.0).astype(BF16)
    a_cum = _dot_split_rhs(tri, d_a, 3)
    a_cum_t = a_cum.T
    expand = e_ref[...]
    dt_x = _dot_split_lhs(dt, expand, 2)
    a_x = _dot_split_lhs(a_cum, expand, 2)
    last = q - 1 if fwd else 0
    a_last = a_x[last:last + 1, :]
    xdt = xs_ref[...] * dt_x
    xdt_b = xdt.astype(BF16)
    w_state = (jnp.exp(a_last - a_x) * xdt).astype(BF16)
    exp_a = jnp.exp(a_x)
    exp_last = jnp.exp(a_last)
    low = lax.broadcasted_iota(jnp.int32, (q, LANES), 1) < SSM_HEAD_DIM
    gw = HEADS_PER_GROUP * SSM_HEAD_DIM
    for g in range(N_GROUPS):
        gs = slice(g * gw, (g + 1) * gw)
        b_g = bm_ref[:, g * D_STATE:(g + 1) * D_STATE]
        c_b = cm_ref[:, g * D_STATE:(g + 1) * D_STATE].astype(BF16)
        cb = lax.dot_general(c_b, b_g.astype(BF16), (((1,), (1,)), ((), ())), preferred_element_type=F32)
        st_g = st_scr[:, gs]
        y_off = _dot(c_b, st_g.astype(BF16)) * exp_a[:, gs]
        s_new = _dot(b_g.T.astype(BF16), w_state[:, gs])
        for pr in range(HEADS_PER_GROUP // 2):
            ms = []
            for j in range(2):
                hl = direction * N_SSM_HEADS + g * HEADS_PER_GROUP + 2 * pr + j
                seg = a_cum[:, hl:hl + 1] - a_cum_t[hl:hl + 1, :]
                dec = jnp.exp(jnp.where(causal, seg, -jnp.inf))
                ms.append((cb * dec).astype(BF16))
            xp = xdt_b[:, g * gw + pr * LANES:g * gw + (pr + 1) * LANES]
            zero = jnp.zeros_like(xp)
            rhs = jnp.concatenate([jnp.where(low, xp, zero), jnp.where(low, zero, xp)], axis=0)
            y_ref[:, g * gw + pr * LANES:g * gw + (pr + 1) * LANES] = (
                _dot(jnp.concatenate(ms, axis=1), rhs) + y_off[:, pr * LANES:(pr + 1) * LANES]).astype(y_ref.dtype)
        st_scr[:, gs] = exp_last[:, gs] * st_g + s_new

    @pl.when(c == nc - 1)
    def _():
        hT_ref[...] = st_scr[...]


def _ssd_scan(direction, xs, bm, cm, dt_raw, dt_bias, a_neg, h0):
    b, s, d_inner = xs.shape
    gn = bm.shape[2]
    nc = s // CHUNK
    heads = np.arange(d_inner) // SSM_HEAD_DIM + direction * N_SSM_HEADS
    expand = jnp.asarray(np.tile(np.arange(LANES)[:, None] == heads[None, :], (2, 1)), BF16)
    if direction == 0:
        cidx = lambda bi, c: (bi, c, 0)
    else:
        cidx = lambda bi, c: (bi, nc - 1 - c, 0)
    const = lambda bi, c: (0, 0)
    st_spec = pl.BlockSpec((None, D_STATE, d_inner), lambda bi, c: (bi, 0, 0))
    return pl.pallas_call(
        functools.partial(_ssd_kernel, direction, nc),
        out_shape=[jax.ShapeDtypeStruct((b, s, d_inner), BF16),
                   jax.ShapeDtypeStruct((b, D_STATE, d_inner), F32)],
        grid=(b, nc),
        in_specs=[pl.BlockSpec((None, CHUNK, d_inner), cidx),
                  pl.BlockSpec((None, CHUNK, gn), cidx), pl.BlockSpec((None, CHUNK, gn), cidx),
                  pl.BlockSpec((None, CHUNK, LANES), cidx),
                  pl.BlockSpec((1, LANES), const), pl.BlockSpec((1, LANES), const),
                  st_spec, pl.BlockSpec((2 * LANES, d_inner), const)],
        out_specs=[pl.BlockSpec((None, CHUNK, d_inner), cidx), st_spec],
        scratch_shapes=[pltpu.VMEM((D_STATE, d_inner), F32)],
        compiler_params=_cparams("parallel", "arbitrary"),
        name="ssd_scan",
    )(xs, bm, cm, dt_raw, dt_bias, a_neg, h0, expand)


def _ssdout_kernel(yf_ref, yb_ref, xs_ref, z_ref, dsk_ref, gnw_ref, w_ref, x_ref, gate_ref, nw_ref, o_ref):
    y = yf_ref[...].astype(F32) + yb_ref[...].astype(F32) + dsk_ref[...] * xs_ref[...]
    gy = y * _silu(z_ref[...].astype(F32))
    gnw = gnw_ref[...]
    gw = gy.shape[1] // N_GROUPS
    parts = []
    for g in range(N_GROUPS):
        parts.append(_rms(gy[:, g * gw:(g + 1) * gw], gnw[:, g * gw:(g + 1) * gw]).astype(BF16))
    out = _dot(jnp.concatenate(parts, axis=1), w_ref[...])
    o_ref[...] = x_ref[...] + gate_ref[...] * _rms(out, nw_ref[...])


def _ssd_out(yf, yb, xs, z, d_skip, gnw, w_out, x, gate, nw):
    b, s, d = x.shape
    d_inner = xs.shape[2]
    tm = _pick(s, (256, 128))
    row = lambda w: pl.BlockSpec((None, tm, w), lambda bi, i: (bi, i, 0))
    const = lambda shp: pl.BlockSpec(shp, lambda bi, i: (0, 0))
    return pl.pallas_call(
        _ssdout_kernel,
        out_shape=jax.ShapeDtypeStruct((b, s, d), F32),
        grid=(b, s // tm),
        in_specs=[row(d_inner), row(d_inner), row(d_inner), row(d_inner),
                  const((1, d_inner)), const((1, d_inner)), const(w_out.shape), row(d),
                  pl.BlockSpec((None, 1, d), lambda bi, i: (bi, 0, 0)), const((1, d))],
        out_specs=row(d),
        compiler_params=_cparams("parallel", "parallel"),
        name="ssd_gate_out_residual",
    )(yf, yb, xs, z, d_skip, gnw, w_out, x, gate, nw)


def _rope_tables(seq):
    rows = seq // GRID_W
    row = jnp.repeat(jnp.arange(rows, dtype=F32), GRID_W)
    col = jnp.tile(jnp.arange(GRID_W, dtype=F32), rows)
    freqs = ROPE_THETA ** (-jnp.arange(0, AXIS_ROT_DIM, 2, dtype=F32) / AXIS_ROT_DIM)
    ang = jnp.concatenate([row[:, None] * freqs, col[:, None] * freqs], axis=-1)
    cos = jnp.repeat(jnp.cos(ang), 2, axis=-1)
    sin = jnp.repeat(jnp.sin(ang), 2, axis=-1) * jnp.tile(jnp.asarray([-1.0, 1.0], F32), HEAD_DIM // 2)
    return jnp.tile(cos, (1, LANES // HEAD_DIM)), jnp.tile(sin, (1, LANES // HEAD_DIM))


def _mod_vectors(mods_i, batch):
    d = mods_i.shape[1] // 6
    mx = [mods_i[:batch, k * d:(k + 1) * d].reshape(batch, 1, d) for k in range(6)]
    mc = [jnp.broadcast_to(mods_i[batch, k * d:(k + 1) * d].reshape(1, 1, d), (batch, 1, d)) for k in range(6)]
    return mx, mc


def _even_mixer(x, ctx, nw0, nw1, mx, mc, w_in, qnw, knw, w_out, rope, need_ctx):
    w_u = w_in[:, :F_WIDTH].astype(BF16)
    w_qkv = w_in[:, F_WIDTH:].astype(BF16)
    qw = jnp.tile(qnw.reshape(1, HEAD_DIM), (1, LANES // HEAD_DIM))
    kw = jnp.tile(knw.reshape(1, HEAD_DIM), (1, LANES // HEAD_DIM))
    w_f = w_out[:F_WIDTH].astype(BF16)
    d = w_out.shape[1]
    w_a = (w_out[F_WIDTH:].reshape(N_KV_HEADS, GQ, HEAD_DIM, d).transpose(1, 0, 2, 3)
           .reshape(Q_WIDTH, d).astype(BF16))
    lc = ctx.shape[1]
    ones = jnp.ones((lc, LANES), F32)
    zeros = jnp.zeros((lc, LANES), F32)

    u_c, qkv_c = _norm_mod_linear(ctx, nw0, mc[0], mc[1], [w_u, w_qkv], [F32, F32])
    q_c, k_c, v_c = _qk_prepare(qkv_c, ones, zeros, qw, kw)
    u_x, qkv_x = _norm_mod_linear(x, nw0, mx[0], mx[1], [w_u, w_qkv], [F32, F32])
    q_x, k_x, v_x = _qk_prepare(qkv_x, rope[0], rope[1], qw, kw)
    k_all = jnp.concatenate([k_c, k_x], axis=1)
    v_all = jnp.concatenate([v_c, v_x], axis=1)
    a_x = _attention(q_x, k_all, v_all)
    f_x = _fourier_mix_long(u_x)
    x_new = _mixer_out(f_x, a_x, w_f, w_a, x, mx[2], nw1)
    if not need_ctx:
        return x_new, ctx
    a_c = _attention(q_c, k_c, v_c)
    f_c = _fourier_mix_short(u_c)
    return x_new, _mixer_out(f_c, a_c, w_f, w_a, ctx, mc[2], nw1)


def _ssd_branch(h_in, nw0, shift, scale, w_z, w_xbc, w_dt, conv_w, conv_b, dt_bias, a_neg, h0_f, h0_b):
    z, xbc, dt_raw = _norm_mod_linear(h_in, nw0, shift, scale, [w_z, w_xbc, w_dt], [BF16, F32, F32], tm_max=256)
    xs, bm, cm = _conv_silu(xbc, conv_w, conv_b, w_z.shape[1])
    y_f, s_f = _ssd_scan(0, xs, bm, cm, dt_raw, dt_bias, a_neg, h0_f)
    y_b, s_b = _ssd_scan(1, xs, bm, cm, dt_raw, dt_bias, a_neg, h0_b)
    return z, xs, y_f, y_b, s_f, s_b


def _odd_mixer(x, ctx, nw0, nw1, mx, mc, w_in, conv_w, conv_b, dt_bias, a_log, d_skip, gnw, w_out, need_ctx):
    d_inner = w_out.shape[0]
    conv_dim = conv_w.shape[1]
    w_z = w_in[:, :d_inner].astype(BF16)
    w_xbc = w_in[:, d_inner:d_inner + conv_dim].astype(BF16)
    n_dt = w_in.shape[1] - d_inner - conv_dim
    w_dt = jnp.pad(w_in[:, d_inner + conv_dim:], ((0, 0), (0, LANES - n_dt))).astype(BF16)
    dtb = jnp.pad(dt_bias.reshape(1, n_dt), ((0, 0), (0, LANES - n_dt)))
    a_neg = jnp.pad(-jnp.exp(a_log.astype(F32)).reshape(1, n_dt), ((0, 0), (0, LANES - n_dt)))
    cb = conv_b.reshape(1, conv_dim)
    dsk = jnp.repeat(d_skip, SSM_HEAD_DIM).reshape(1, d_inner)
    gw = gnw.reshape(1, d_inner)
    w_o = w_out.astype(BF16)
    b = x.shape[0]
    zeros = jnp.zeros((b, D_STATE, d_inner), F32)

    z_c, xs_c, yf_c, yb_c, s_f, s_b = _ssd_branch(ctx, nw0, mc[0], mc[1], w_z, w_xbc, w_dt, conv_w, cb,
                                                  dtb, a_neg, zeros, zeros)
    z_x, xs_x, yf_x, yb_x, _, _ = _ssd_branch(x, nw0, mx[0], mx[1], w_z, w_xbc, w_dt, conv_w, cb,
                                              dtb, a_neg, s_f, s_b)
    x_new = _ssd_out(yf_x, yb_x, xs_x, z_x, dsk, gw, w_o, x, mx[2], nw1)
    if not need_ctx:
        return x_new, ctx
    return x_new, _ssd_out(yf_c, yb_c, xs_c, z_c, dsk, gw, w_o, ctx, mc[2], nw1)


def kernel(x, c, ctx, c_ctx, ada_w, ada_b, norm_w, mix_in_w, q_norm_w, k_norm_w, mix_out_w, ffn_gu_w, ffn_down_w, ssm_in_w, ssm_conv_w, ssm_conv_b, ssm_dt_bias, ssm_a_log, ssm_d, ssm_norm_w, ssm_out_w, router_w, moe_gu_w, moe_down_w):
    batch, seq, d = x.shape
    depth = ada_w.shape[0]
    rope = _rope_tables(seq)
    cvec = jnp.zeros((SUBLANES, d), F32).at[:batch].set(c).at[batch].set(c_ctx)
    mods = _modulation(cvec, ada_w, ada_b)
    for i in range(depth):
        last = i == depth - 1
        j = i // 2
        mx, mc = _mod_vectors(mods[i], batch)
        nw = [norm_w[i, k].reshape(1, d) for k in range(4)]
        if i % 2 == 0:
            x, ctx_mid = _even_mixer(x, ctx, nw[0], nw[1], mx, mc, mix_in_w[j], q_norm_w[j], k_norm_w[j],
                                     mix_out_w[j], rope, not last)
            w_gu = ffn_gu_w[j].astype(BF16)[None]
            w_dn = ffn_down_w[j].astype(BF16)[None]
            x = _ffn(x, nw[2], mx[3], mx[4], w_gu, w_dn, mx[5], nw[3])
            if not last:
                ctx = _ffn(ctx_mid, nw[2], mc[3], mc[4], w_gu, w_dn, mc[5], nw[3])
        else:
            x, ctx_mid = _odd_mixer(x, ctx, nw[0], nw[1], mx, mc, ssm_in_w[j], ssm_conv_w[j], ssm_conv_b[j],
                                    ssm_dt_bias[j], ssm_a_log[j], ssm_d[j], ssm_norm_w[j], ssm_out_w[j],
                                    not last)
            rw = jnp.pad(router_w[j], ((0, 0), (0, LANES - N_EXPERTS)))
            w_gu = moe_gu_w[j].astype(BF16)
            w_dn = moe_down_w[j].astype(BF16)
            x = _moe_sparse(x, nw[2], mx[3], mx[4], rw, w_gu, w_dn, mx[5], nw[3])
            if not last:
                comb_c = _router(ctx_mid, nw[2], mc[3], mc[4], rw)
                ctx = _ffn(ctx_mid, nw[2], mc[3], mc[4], w_gu, w_dn, mc[5], nw[3], comb_c)
    return x
```
